```python
import jax, jax.numpy as jnp
from jax import lax
import numpy as np

D_MODEL = 1024
BATCH = 8
SEQ = 8192
DEPTH = 2

N_META = 16
N_BRANCH = 4
GLA_HEADS = 4
GLA_DK = 32
GLA_DV = 64
GLA_GATE_RANK = 16
GLA_TAU = 16.0
MLA_HEADS = 4
MLA_Q_RANK = 256
MLA_KV_RANK = 128
MLA_NOPE_DIM = 64
MLA_ROPE_DIM = 32
MLA_V_DIM = 64
ROPE_THETA = 10000.0
Q_BLOCK = 128
GDN_HEADS = 4
GDN_DK = 64
GDN_DV = 64
GDN_CONV = 4
RWKV_HEADS = 4
RWKV_N = 64
RWKV_W_RANK = 64
RWKV_A_RANK = 64
RWKV_G_RANK = 160
RWKV_WIDTH = RWKV_HEADS * RWKV_N
RWKV_COLS = 3 * RWKV_WIDTH + RWKV_W_RANK + RWKV_A_RANK + RWKV_G_RANK
RWKV_LN_EPS = RWKV_N * 1e-5
CHUNK = 64
LEAD_PAD = (-N_META) % CHUNK
D_FF = 2816
N_EXPERTS = 8
TOP_K = 2
D_FF_EXPERT = 3584
MOE_BLOCK = 512
N_DENSE = (DEPTH + 1) // 2
N_MOE = DEPTH // 2
BRANCH_WIDTH = 256
NORM_EPS = 1e-6
L2_EPS = 1e-6
NEG_INF = -1e30
IN_WIDTHS = (
    GLA_HEADS * GLA_DK,
    GLA_HEADS * GLA_DK,
    GLA_HEADS * GLA_DV,
    GLA_GATE_RANK,
    GLA_HEADS * GLA_DV,
    MLA_Q_RANK,
    MLA_KV_RANK,
    MLA_ROPE_DIM,
    GDN_HEADS * (2 * GDN_DK + GDN_DV),
    GDN_HEADS * GDN_DV,
    GDN_HEADS,
    GDN_HEADS,
    RWKV_COLS,
    N_BRANCH * D_MODEL,
)
IN_WIDTH = 7384

kernel_name = 'hybrid_gated_four_mixer_block'


def rms_norm(x, w):
    xf = x.astype(jnp.float32)
    y = xf * lax.rsqrt(jnp.mean(xf * xf, axis=-1, keepdims=True) + NORM_EPS)
    return (y * w.astype(jnp.float32)).astype(x.dtype)


def l2_normalize(x):
    return x * lax.rsqrt(jnp.sum(x * x, axis=-1, keepdims=True) + L2_EPS)


def pad_front(a, n):
    return jnp.pad(a, ((0, 0), (n, 0)) + ((0, 0),) * (a.ndim - 2))


def to_chunks(a):
    b, t = a.shape[:2]
    nc = (t + LEAD_PAD) // CHUNK
    a = pad_front(a, LEAD_PAD).reshape((b, nc, CHUNK) + a.shape[2:])
    return a.transpose((1, 0, 3, 2) + tuple(range(4, a.ndim)))


def from_chunks(o):
    nc, b, h, c, d = o.shape
    return o.transpose(1, 0, 3, 2, 4).reshape(b, nc * c, h, d)[:, LEAD_PAD:]


def gla_chunked(q, k, v, log_a):
    b, _, h, dk = q.shape
    dv = v.shape[-1]
    qc, kc, vc, gc = (to_chunks(a) for a in (q, k, v, log_a))
    gc = jnp.cumsum(gc, axis=-2)
    causal = jnp.tril(jnp.ones((CHUNK, CHUNK), bool))

    def step(state, xs):
        qi, ki, vi, gi = xs
        rel = jnp.exp(jnp.where(causal[:, :, None], gi[:, :, :, None, :] - gi[:, :, None, :, :], NEG_INF))
        scores = jnp.einsum('bhid,bhjd,bhijd->bhij', qi, ki, rel)
        o = jnp.einsum('bhij,bhjv->bhiv', scores, vi) + jnp.einsum('bhid,bhdv->bhiv', qi * jnp.exp(gi), state)
        g_last = gi[:, :, -1:, :]
        k_end = ki * jnp.exp(g_last - gi)
        state = state * jnp.swapaxes(jnp.exp(g_last), -1, -2) + jnp.einsum('bhjd,bhjv->bhdv', k_end, vi)
        return state, o

    s0 = jnp.zeros((b, h, dk, dv), jnp.float32)
    _, o = lax.scan(step, s0, (qc, kc, vc, gc))
    return from_chunks(o)


def gla_branch(q, k, v, gk_down, out_gate, gate_up, gate_bias, norm_w):
    b, t, _ = q.shape
    f32 = jnp.float32
    log_a = jax.nn.log_sigmoid((gk_down @ gate_up + gate_bias).astype(f32)) / GLA_TAU
    q = q.astype(f32).reshape(b, t, GLA_HEADS, GLA_DK) * GLA_DK ** -0.5
    k = k.astype(f32).reshape(b, t, GLA_HEADS, GLA_DK)
    v = v.astype(f32).reshape(b, t, GLA_HEADS, GLA_DV)
    o = gla_chunked(q, k, v, log_a.reshape(b, t, GLA_HEADS, GLA_DK))
    o = rms_norm(o, norm_w) * jax.nn.silu(out_gate.astype(f32).reshape(b, t, GLA_HEADS, GLA_DV))
    return o.reshape(b, t, GLA_HEADS * GLA_DV)


def rope(x, cos, sin):
    x1, x2 = jnp.split(x, 2, axis=-1)
    return jnp.concatenate([x1 * cos - x2 * sin, x2 * cos + x1 * sin], axis=-1)


def mla_branch(q_c, kv_c, k_pe, q_norm, w_uq, kv_norm, w_ukv):
    b, t, _ = q_c.shape
    f32 = jnp.float32
    q = jnp.einsum('btr,rc->btc', rms_norm(q_c, q_norm), w_uq).astype(f32)
    q = q.reshape(b, t, MLA_HEADS, MLA_NOPE_DIM + MLA_ROPE_DIM)
    kv = jnp.einsum('btr,rc->btc', rms_norm(kv_c, kv_norm), w_ukv).astype(f32)
    kv = kv.reshape(b, t, MLA_HEADS, MLA_NOPE_DIM + MLA_V_DIM)
    q_nope, q_pe = q[..., :MLA_NOPE_DIM], q[..., MLA_NOPE_DIM:]
    k_nope, v = kv[..., :MLA_NOPE_DIM], kv[..., MLA_NOPE_DIM:]
    pos = jnp.arange(t, dtype=f32)
    inv_freq = ROPE_THETA ** (-jnp.arange(0, MLA_ROPE_DIM, 2, dtype=f32) / MLA_ROPE_DIM)
    ang = pos[:, None] * inv_freq[None, :]
    cos, sin = jnp.cos(ang), jnp.sin(ang)
    q_pe = rope(q_pe, cos[:, None, :], sin[:, None, :])
    k_pe = rope(k_pe.astype(f32), cos, sin)
    scale = (MLA_NOPE_DIM + MLA_ROPE_DIM) ** -0.5
    q_pad = (-t) % Q_BLOCK
    n_blk = (t + q_pad) // Q_BLOCK

    def blocks(a):
        return pad_front(a, q_pad).reshape((b, n_blk, Q_BLOCK) + a.shape[2:]).swapaxes(0, 1)

    key_pos = jnp.arange(t)

    def attend(xs):
        qn, qr, start = xs
        s = jnp.einsum('bqhd,bkhd->bhqk', qn, k_nope) + jnp.einsum('bqhr,bkr->bhqk', qr, k_pe)
        q_pos = start + jnp.arange(Q_BLOCK) - q_pad
        s = jnp.where(key_pos[None, :] <= q_pos[:, None], s * scale, NEG_INF)
        p = jax.nn.softmax(s, axis=-1)
        return jnp.einsum('bhqk,bkhv->bqhv', p, v)

    o = lax.map(attend, (blocks(q_nope), blocks(q_pe), jnp.arange(n_blk) * Q_BLOCK))
    o = o.swapaxes(0, 1).reshape(b, n_blk * Q_BLOCK, MLA_HEADS, MLA_V_DIM)[:, q_pad:]
    return o.reshape(b, t, MLA_HEADS * MLA_V_DIM)


def causal_dwconv(x, w):
    k = w.shape[0]
    return lax.conv_general_dilated(x, w[:, None, :], window_strides=(1,), padding=((k - 1, 0),),
                                    dimension_numbers=('NWC', 'WIO', 'NWC'),
                                    feature_group_count=x.shape[-1])


def gdn_chunked(q, k, v, beta, g):
    b, _, h, dk = q.shape
    dv = v.shape[-1]
    qc, kc, vc = (to_chunks(a) for a in (q, k, v))
    bc, gc = to_chunks(beta), to_chunks(g)
    gc = jnp.cumsum(gc, axis=-1)
    incl = jnp.tril(jnp.ones((CHUNK, CHUNK), bool))
    strict = jnp.tril(jnp.ones((CHUNK, CHUNK), bool), -1)
    decay = jnp.exp(jnp.where(incl, gc[..., :, None] - gc[..., None, :], NEG_INF))
    kb = kc * bc[..., None]
    m = jnp.where(strict, jnp.einsum('nbhid,nbhjd->nbhij', kb, kc) * decay, 0.0)
    a_mat = m + jnp.eye(CHUNK, dtype=m.dtype)
    rhs = jnp.concatenate([vc * bc[..., None], kb * jnp.exp(gc)[..., None]], axis=-1)
    sol = lax.linalg.triangular_solve(a_mat, rhs, left_side=True, lower=True, unit_diagonal=True)
    u, w = sol[..., :dv], sol[..., dv:]
    attn = jnp.where(incl, jnp.einsum('nbhid,nbhjd->nbhij', qc, kc) * decay, 0.0)
    q_dec = qc * jnp.exp(gc)[..., None]
    k_end = kc * jnp.exp(gc[..., -1:] - gc)[..., None]
    g_tot = jnp.exp(gc[..., -1])

    def step(state, xs):
        u_i, w_i, attn_i, qd_i, ke_i, gt_i = xs
        v_new = u_i - jnp.einsum('bhcd,bhdv->bhcv', w_i, state)
        o = jnp.einsum('bhcd,bhdv->bhcv', qd_i, state) + jnp.einsum('bhij,bhjv->bhiv', attn_i, v_new)
        state = state * gt_i[..., None, None] + jnp.einsum('bhjd,bhjv->bhdv', ke_i, v_new)
        return state, o

    s0 = jnp.zeros((b, h, dk, dv), jnp.float32)
    _, o = lax.scan(step, s0, (u, w, attn, q_dec, k_end, g_tot))
    return from_chunks(o)


def gdn_branch(qkv, z, beta_logit, a_logit, conv_w, a_log, dt_bias, norm_w):
    b, t, _ = qkv.shape
    f32 = jnp.float32
    qkv = jax.nn.silu(causal_dwconv(qkv.astype(f32), conv_w.astype(f32)))
    q, k, v = jnp.split(qkv, [GDN_HEADS * GDN_DK, 2 * GDN_HEADS * GDN_DK], axis=-1)
    q = l2_normalize(q.reshape(b, t, GDN_HEADS, GDN_DK)) * GDN_DK ** -0.5
    k = l2_normalize(k.reshape(b, t, GDN_HEADS, GDN_DK))
    v = v.reshape(b, t, GDN_HEADS, GDN_DV)
    beta = jax.nn.sigmoid(beta_logit.astype(f32))
    g = -jnp.exp(a_log.astype(f32)) * jax.nn.softplus(a_logit.astype(f32) + dt_bias.astype(f32))
    o = gdn_chunked(q, k, v, beta, g)
    o = rms_norm(o, norm_w) * jax.nn.silu(z.astype(f32).reshape(b, t, GDN_HEADS, GDN_DV))
    return o.reshape(b, t, GDN_HEADS * GDN_DV)


def rwkv_branch(cols, mu, w0, w2, a0, a2, g2, k_k, k_a, r_k, ln_w, ln_b):
    b, t, _ = cols.shape
    f32 = jnp.float32
    z = cols.astype(f32)
    z = z + (pad_front(z[:, :-1], 1) - z) * mu
    offs = [RWKV_WIDTH, 2 * RWKV_WIDTH, 3 * RWKV_WIDTH, 3 * RWKV_WIDTH + RWKV_W_RANK,
            3 * RWKV_WIDTH + RWKV_W_RANK + RWKV_A_RANK]
    r, k, v, wd, ad, gd = jnp.split(z, offs, axis=-1)
    w_log = -jax.nn.softplus(-(w0 + jnp.tanh(wd) @ w2)) - 0.5
    decay = jnp.exp(-jnp.exp(w_log))
    a = jax.nn.sigmoid(a0 + ad @ a2)
    g = jax.nn.sigmoid(gd) @ g2

    def heads(u):
        return u.reshape(b, t, RWKV_HEADS, RWKV_N)

    kk = l2_normalize(heads(k * k_k))
    k = k * (1.0 + (a - 1.0) * k_a)
    r_h, k_h, v_h = heads(r), heads(k), heads(v)
    xs = tuple(jnp.swapaxes(u, 0, 1) for u in (r_h, heads(decay), k_h, v_h, kk, kk * heads(a)))

    def step(state, inp):
        r_t, w_t, k_t, v_t, kk_t, kka_t = inp
        removal = jnp.einsum('bhvk,bhk->bhv', state, kk_t)
        state = (state * w_t[:, :, None, :] - removal[..., None] * kka_t[:, :, None, :]
                 + v_t[..., None] * k_t[:, :, None, :])
        return state, jnp.einsum('bhvk,bhk->bhv', state, r_t)

    s0 = jnp.zeros((b, RWKV_HEADS, RWKV_N, RWKV_N), f32)
    _, y = lax.scan(step, s0, xs)
    y = jnp.swapaxes(y, 0, 1)
    mean = jnp.mean(y, axis=-1, keepdims=True)
    var = jnp.mean(jnp.square(y - mean), axis=-1, keepdims=True)
    y = ((y - mean) * lax.rsqrt(var + RWKV_LN_EPS) * ln_w.reshape(RWKV_HEADS, RWKV_N)
         + ln_b.reshape(RWKV_HEADS, RWKV_N))
    y = y + jnp.sum(r_h * k_h * r_k, axis=-1, keepdims=True) * v_h
    return y.reshape(b, t, RWKV_WIDTH) * g


def token_mixing(h, w_in, gla_gate_up, gla_gate_bias, gla_norm, mla_q_norm, mla_w_uq, mla_kv_norm,
                 mla_w_ukv, gdn_conv, gdn_a_log, gdn_dt_bias, gdn_norm, rwkv_mu, rwkv_w0, rwkv_w2,
                 rwkv_a0, rwkv_a2, rwkv_g2, rwkv_k_k, rwkv_k_a, rwkv_r_k, rwkv_ln_w, rwkv_ln_b,
                 w_branch, w_out):
    b, t, d = h.shape
    offsets = np.cumsum(IN_WIDTHS)[:-1].tolist()
    (gla_q, gla_k, gla_v, gla_gk, gla_og, mla_qc, mla_kvc, mla_kpe, gdn_qkv, gdn_z, gdn_beta,
     gdn_a, rwkv_cols, gate_logits) = jnp.split(jnp.einsum('btd,dc->btc', h, w_in), offsets, axis=-1)
    branches = (
        gla_branch(gla_q, gla_k, gla_v, gla_gk, gla_og, gla_gate_up, gla_gate_bias, gla_norm),
        mla_branch(mla_qc, mla_kvc, mla_kpe, mla_q_norm, mla_w_uq, mla_kv_norm, mla_w_ukv),
        gdn_branch(gdn_qkv, gdn_z, gdn_beta, gdn_a, gdn_conv, gdn_a_log, gdn_dt_bias, gdn_norm),
        rwkv_branch(rwkv_cols, rwkv_mu, rwkv_w0, rwkv_w2, rwkv_a0, rwkv_a2, rwkv_g2, rwkv_k_k,
                    rwkv_k_a, rwkv_r_k, rwkv_ln_w, rwkv_ln_b),
    )
    gates = jax.nn.sigmoid(gate_logits.astype(jnp.float32)).reshape(b, t, N_BRANCH, d)
    merged = jnp.zeros((b, t, d), jnp.float32)
    for i, y in enumerate(branches):
        proj = jnp.einsum('btc,cd->btd', y.astype(h.dtype), w_branch[i]).astype(jnp.float32)
        merged = merged + gates[:, :, i] * proj
    return jnp.einsum('btd,de->bte', merged.astype(h.dtype), w_out)


def swiglu(h, w_gate, w_up, w_down):
    return jnp.einsum('btf,fd->btd', jax.nn.silu(h @ w_gate) * (h @ w_up), w_down)


def moe_swiglu(h, router, w_gate, w_up, w_down):
    b, t, d = h.shape
    n = b * t
    xf = h.reshape(n, d)
    logits = jnp.einsum('nd,de->ne', xf.astype(jnp.float32), router.astype(jnp.float32))
    top_logits, top_idx = lax.top_k(logits, TOP_K)
    gate = jax.nn.softmax(top_logits, axis=-1)
    n_assign = n * TOP_K
    expert = top_idx.reshape(n_assign)
    token = jnp.arange(n_assign, dtype=jnp.int32) // TOP_K
    order = jnp.argsort(expert)
    e_sorted = expert[order]
    counts = jnp.zeros((N_EXPERTS,), jnp.int32).at[expert].add(1)
    padded = (counts + MOE_BLOCK - 1) // MOE_BLOCK * MOE_BLOCK
    ends = jnp.cumsum(padded)
    dest = ((ends - padded)[e_sorted] + jnp.arange(n_assign, dtype=jnp.int32)
            - (jnp.cumsum(counts) - counts)[e_sorted])
    n_blocks = -(-n_assign // MOE_BLOCK) + N_EXPERTS
    cap = n_blocks * MOE_BLOCK
    row_token = jnp.zeros((cap,), jnp.int32).at[dest].set(token[order])
    row_gate = jnp.zeros((cap,), jnp.float32).at[dest].set(gate.reshape(n_assign)[order])
    block_start = jnp.arange(n_blocks, dtype=jnp.int32) * MOE_BLOCK
    block_expert = jnp.minimum(jnp.sum(block_start[:, None] >= ends[None, :], axis=1), N_EXPERTS - 1)
    x_rows = xf[row_token].reshape(n_blocks, MOE_BLOCK, d)

    def expert_block(args):
        xb, e = args
        return jnp.einsum('rf,fd->rd', jax.nn.silu(xb @ w_gate[e]) * (xb @ w_up[e]), w_down[e])

    y_rows = lax.map(expert_block, (x_rows, block_expert)).reshape(cap, d)
    out = jnp.zeros((n, d), jnp.float32).at[row_token].add(y_rows.astype(jnp.float32) * row_gate[:, None])
    return out.astype(h.dtype).reshape(b, t, d)


def setup_inputs(seed: int = 0) -> dict:
    key = jax.random.key(seed)
    ks = iter(jax.random.split(key, 48))
    f32 = jnp.float32

    def nrm(shape, scale):
        return jax.random.normal(next(ks), shape, f32) * scale

    def unif(shape, lo, hi):
        return jax.random.uniform(next(ks), shape, f32, lo, hi)

    def gain(shape):
        return 1.0 + nrm(shape, 0.02)

    L = DEPTH
    gdn_dt = jnp.exp(unif((L, GDN_HEADS), float(np.log(1e-3)), float(np.log(1e-1))))
    return {
        'x': nrm((BATCH, SEQ, D_MODEL), 1.0),
        'meta_tokens': nrm((N_META, D_MODEL), 1.0),
        'norm_mix': gain((L, D_MODEL)),
        'w_in': nrm((L, D_MODEL, IN_WIDTH), D_MODEL ** -0.5),
        'gla_gate_up': nrm((L, GLA_GATE_RANK, GLA_HEADS * GLA_DK), GLA_GATE_RANK ** -0.5),
        'gla_gate_bias': nrm((L, GLA_HEADS * GLA_DK), 0.1),
        'gla_norm': gain((L, GLA_DV)),
        'mla_q_norm': gain((L, MLA_Q_RANK)),
        'mla_w_uq': nrm((L, MLA_Q_RANK, MLA_HEADS * (MLA_NOPE_DIM + MLA_ROPE_DIM)), MLA_Q_RANK ** -0.5),
        'mla_kv_norm': gain((L, MLA_KV_RANK)),
        'mla_w_ukv': nrm((L, MLA_KV_RANK, MLA_HEADS * (MLA_NOPE_DIM + MLA_V_DIM)), MLA_KV_RANK ** -0.5),
        'gdn_conv': nrm((L, GDN_CONV, GDN_HEADS * (2 * GDN_DK + GDN_DV)), GDN_CONV ** -0.5),
        'gdn_a_log': jnp.log(unif((L, GDN_HEADS), 1.0, 16.0)),
        'gdn_dt_bias': jnp.log(jnp.expm1(gdn_dt)),
        'gdn_norm': gain((L, GDN_DV)),
        'rwkv_mu': unif((L, RWKV_COLS), 0.0, 1.0),
        'rwkv_w0': unif((L, RWKV_WIDTH), -5.0, 1.0),
        'rwkv_w2': nrm((L, RWKV_W_RANK, RWKV_WIDTH), 0.1 * RWKV_W_RANK ** -0.5),
        'rwkv_a0': nrm((L, RWKV_WIDTH), 0.1),
        'rwkv_a2': nrm((L, RWKV_A_RANK, RWKV_WIDTH), 0.1 * RWKV_A_RANK ** -0.5),
        'rwkv_g2': nrm((L, RWKV_G_RANK, RWKV_WIDTH), RWKV_G_RANK ** -0.5),
        'rwkv_k_k': 0.85 + nrm((L, RWKV_WIDTH), 0.05),
        'rwkv_k_a': 1.0 + nrm((L, RWKV_WIDTH), 0.05),
        'rwkv_r_k': nrm((L, RWKV_HEADS, RWKV_N), 0.1),
        'rwkv_ln_w': gain((L, RWKV_WIDTH)),
        'rwkv_ln_b': nrm((L, RWKV_WIDTH), 0.02),
        'w_branch': nrm((L, N_BRANCH, BRANCH_WIDTH, D_MODEL), BRANCH_WIDTH ** -0.5),
        'w_out': nrm((L, D_MODEL, D_MODEL), D_MODEL ** -0.5),
        'norm_ffn': gain((L, D_MODEL)),
        'ffn_w_gate': nrm((N_DENSE, D_MODEL, D_FF), D_MODEL ** -0.5),
        'ffn_w_up': nrm((N_DENSE, D_MODEL, D_FF), D_MODEL ** -0.5),
        'ffn_w_down': nrm((N_DENSE, D_FF, D_MODEL), D_FF ** -0.5),
        'moe_router': nrm((N_MOE, D_MODEL, N_EXPERTS), D_MODEL ** -0.5),
        'moe_w_gate': nrm((N_MOE, N_EXPERTS, D_MODEL, D_FF_EXPERT), D_MODEL ** -0.5),
        'moe_w_up': nrm((N_MOE, N_EXPERTS, D_MODEL, D_FF_EXPERT), D_MODEL ** -0.5),
        'moe_w_down': nrm((N_MOE, N_EXPERTS, D_FF_EXPERT, D_MODEL), D_FF_EXPERT ** -0.5),
        'norm_final': gain((D_MODEL,)),
    }


def reference(x, meta_tokens, norm_mix, w_in, gla_gate_up, gla_gate_bias, gla_norm, mla_q_norm,
              mla_w_uq, mla_kv_norm, mla_w_ukv, gdn_conv, gdn_a_log, gdn_dt_bias, gdn_norm, rwkv_mu,
              rwkv_w0, rwkv_w2, rwkv_a0, rwkv_a2, rwkv_g2, rwkv_k_k, rwkv_k_a, rwkv_r_k, rwkv_ln_w,
              rwkv_ln_b, w_branch, w_out, norm_ffn, ffn_w_gate, ffn_w_up, ffn_w_down, moe_router,
              moe_w_gate, moe_w_up, moe_w_down, norm_final):
    b = x.shape[0]
    meta = jnp.broadcast_to(meta_tokens[None].astype(x.dtype), (b, N_META, D_MODEL))
    h_res = jnp.concatenate([meta, x], axis=1)
    for l in range(DEPTH):
        h = rms_norm(h_res, norm_mix[l])
        h_res = h_res + token_mixing(
            h, w_in[l], gla_gate_up[l], gla_gate_bias[l], gla_norm[l], mla_q_norm[l], mla_w_uq[l],
            mla_kv_norm[l], mla_w_ukv[l], gdn_conv[l], gdn_a_log[l], gdn_dt_bias[l], gdn_norm[l],
            rwkv_mu[l], rwkv_w0[l], rwkv_w2[l], rwkv_a0[l], rwkv_a2[l], rwkv_g2[l], rwkv_k_k[l],
            rwkv_k_a[l], rwkv_r_k[l], rwkv_ln_w[l], rwkv_ln_b[l], w_branch[l], w_out[l]).astype(h_res.dtype)
        h = rms_norm(h_res, norm_ffn[l])
        if l % 2 == 0:
            f = swiglu(h, ffn_w_gate[l // 2], ffn_w_up[l // 2], ffn_w_down[l // 2])
        else:
            f = moe_swiglu(h, moe_router[l // 2], moe_w_gate[l // 2], moe_w_up[l // 2], moe_w_down[l // 2])
        h_res = h_res + f.astype(h_res.dtype)
    return rms_norm(h_res, norm_final)[:, N_META:]
```

```python
import functools
import math

import jax
import jax.numpy as jnp
import numpy as np
from jax import lax
from jax.experimental import pallas as pl
from jax.experimental.pallas import tpu as pltpu

F32 = jnp.float32
BF16 = jnp.bfloat16

D_MODEL = 1024
N_META = 16
N_HEADS = 4
GLA_DK = 32
GLA_DV = 64
GLA_TAU = 16.0
MLA_NOPE = 64
MLA_ROPE = 32
MLA_V = 64
MLA_SLOT = 128
ROPE_THETA = 10000.0
GDN_DK = 64
GDN_CONV = 4
RWKV_N = 64
RWKV_LN_EPS = RWKV_N * 1e-5
CHUNK = 64
SUB = 16
N_EXPERTS = 8
NORM_EPS = 1e-6
L2_EPS = 1e-6
NEG_INF = -1e30
EXP_CLAMP = 80.0

LANE = 128
VMEM_LIMIT = 56 * 1024 * 1024

_OFF = np.cumsum([0, 128, 128, 256, 16, 256, 256, 128, 32, 768, 256, 4, 4, 1056, 4096]).tolist()
W_GLA, W_MLA, W_GDN, W_RWKV = 896, 512, 1152, 1152


def _cparams(*sem):
    return pltpu.CompilerParams(dimension_semantics=sem, vmem_limit_bytes=VMEM_LIMIT)


def _pick(n, prefs):
    for p in prefs:
        if n % p == 0:
            return p
    raise ValueError(f"no tile for {n}")


def _dot(a, b):
    return jnp.dot(a.astype(BF16), b.astype(BF16), preferred_element_type=F32)


def _dot_nt(a, b):
    return lax.dot_general(a.astype(BF16), b.astype(BF16), (((1,), (1,)), ((), ())),
                           preferred_element_type=F32)


def _dot_tn(a, b):
    return lax.dot_general(a.astype(BF16), b.astype(BF16), (((0,), (0,)), ((), ())),
                           preferred_element_type=F32)


def _split3(x):
    hi = x.astype(BF16)
    r1 = x - hi.astype(F32)
    mid = r1.astype(BF16)
    lo = (r1 - mid.astype(F32)).astype(BF16)
    return hi, mid, lo


def _dot01_left(m01, x):
    return sum(jnp.dot(m01, p, preferred_element_type=F32) for p in _split3(x))


def _dot01_right(x, m01):
    return sum(jnp.dot(p, m01, preferred_element_type=F32) for p in _split3(x))


def _iota(shape, dim):
    return lax.broadcasted_iota(jnp.int32, shape, dim)


def _div(x, w):
    return x >> int(math.log2(w))


def _ltri(n):
    return (_iota((n, n), 0) >= _iota((n, n), 1)).astype(BF16)


def _head_ones(n, w):
    return (_div(_iota((n, n), 0), w) == _div(_iota((n, n), 1), w)).astype(BF16)


def _lane_masks(width, w):
    lane = _div(_iota((1, width), 1), w)
    return [(lane == h).astype(F32) for h in range(width // w)]


def _stack(x, masks):
    return jnp.concatenate([x * m for m in masks], axis=0)


def _unstack(y, n):
    out = y[0:n]
    for h in range(1, y.shape[0] // n):
        out = out + y[h * n:(h + 1) * n]
    return out


def _rms(x, w):
    return x * lax.rsqrt(jnp.mean(x * x, axis=-1, keepdims=True) + NORM_EPS) * w


def _sigmoid(x):
    return 1.0 / (1.0 + jnp.exp(-x))


def _silu(x):
    return x * _sigmoid(x)


def _softplus(x):
    return jnp.maximum(x, 0.0) + jnp.log(1.0 + jnp.exp(-jnp.abs(x)))


def _neumann_inverse(x, eye):
    t = eye + x
    p = x
    for _ in range(int(math.log2(CHUNK)) - 1):
        p = _dot(p, p)
        t = t + _dot(t, p)
    return t


def _inproj_kernel(h_ref, nw_ref, wg_ref, wm_ref, wd_ref, wr_ref, og_ref, om_ref, od_ref, or_ref):
    xb = _rms(h_ref[...], nw_ref[...]).astype(BF16)
    og_ref[...] = jnp.dot(xb, wg_ref[...], preferred_element_type=F32)
    om_ref[...] = jnp.dot(xb, wm_ref[...], preferred_element_type=F32)
    od_ref[...] = jnp.dot(xb, wd_ref[...], preferred_element_type=F32)
    or_ref[...] = jnp.dot(xb, wr_ref[...], preferred_element_type=F32)


def _inproj(h, nw, wg, wm, wd, wr):
    n = h.shape[0]
    tm = _pick(n, (512, 256, 128, 64))
    full = lambda a: pl.BlockSpec(a.shape, lambda i: (0, 0))
    row = lambda w: pl.BlockSpec((tm, w), lambda i: (i, 0))
    return pl.pallas_call(
        _inproj_kernel,
        grid=(n // tm,),
        in_specs=[row(D_MODEL), full(nw), full(wg), full(wm), full(wd), full(wr)],
        out_specs=[row(W_GLA), row(W_MLA), row(W_GDN), row(W_RWKV)],
        out_shape=[jax.ShapeDtypeStruct((n, w), F32) for w in (W_GLA, W_MLA, W_GDN, W_RWKV)],
        compiler_params=_cparams("parallel"),
        name="inproj",
    )(h, nw, wg, wm, wd, wr)


def _merge_kernel(h_ref, nw_ref, wgate_ref, yg_ref, ym_ref, yd_ref, yr_ref,
                  wbg_ref, wbm_ref, wbd_ref, wbr_ref, wout_ref, out_ref):
    x = h_ref[...]
    xb = _rms(x, nw_ref[...]).astype(BF16)
    acc = jnp.zeros(x.shape, F32)
    branches = ((yg_ref, wbg_ref), (ym_ref, wbm_ref), (yd_ref, wbd_ref), (yr_ref, wbr_ref))
    for i, (y_ref, wb_ref) in enumerate(branches):
        logits = jnp.dot(xb, wgate_ref[:, i * D_MODEL:(i + 1) * D_MODEL], preferred_element_type=F32)
        proj = jnp.dot(y_ref[...], wb_ref[...], preferred_element_type=F32)
        acc = acc + _sigmoid(logits) * proj
    out_ref[...] = x + jnp.dot(acc.astype(BF16), wout_ref[...], preferred_element_type=F32)


def _merge(h, nw, wgate, ys, wbs, wout):
    n = h.shape[0]
    tm = _pick(n, (512, 256, 128, 64))
    full = lambda a: pl.BlockSpec(a.shape, lambda i: (0, 0))
    row = lambda w: pl.BlockSpec((tm, w), lambda i: (i, 0))
    return pl.pallas_call(
        _merge_kernel,
        grid=(n // tm,),
        in_specs=[row(D_MODEL), full(nw), full(wgate)] + [row(y.shape[1]) for y in ys]
        + [full(w) for w in wbs] + [full(wout)],
        out_specs=row(D_MODEL),
        out_shape=jax.ShapeDtypeStruct((n, D_MODEL), F32),
        compiler_params=_cparams("parallel"),
        name="merge",
    )(h, nw, wgate, *ys, *wbs, wout)


def _ffn_kernel(h_ref, nw_ref, wg_ref, wu_ref, wd_ref, out_ref, xb_ref, acc_ref):
    f = pl.program_id(1)

    @pl.when(f == 0)
    def _():
        xb_ref[...] = _rms(h_ref[...], nw_ref[...]).astype(BF16)
        acc_ref[...] = jnp.zeros(acc_ref.shape, F32)

    xb = xb_ref[...]
    a = jnp.dot(xb, wg_ref[...], preferred_element_type=F32)
    b = jnp.dot(xb, wu_ref[...], preferred_element_type=F32)
    acc_ref[...] += jnp.dot((_silu(a) * b).astype(BF16), wd_ref[...], preferred_element_type=F32)

    @pl.when(f == pl.num_programs(1) - 1)
    def _():
        out_ref[...] = h_ref[...] + acc_ref[...]


def _ffn(h, nw, wg, wu, wd):
    n = h.shape[0]
    dff = wg.shape[1]
    tm = _pick(n, (512, 256, 128, 64))
    tf = _pick(dff, (1408, 512, 256, 128))
    return pl.pallas_call(
        _ffn_kernel,
        grid=(n // tm, dff // tf),
        in_specs=[pl.BlockSpec((tm, D_MODEL), lambda i, f: (i, 0)),
                  pl.BlockSpec(nw.shape, lambda i, f: (0, 0)),
                  pl.BlockSpec((D_MODEL, tf), lambda i, f: (0, f)),
                  pl.BlockSpec((D_MODEL, tf), lambda i, f: (0, f)),
                  pl.BlockSpec((tf, D_MODEL), lambda i, f: (f, 0))],
        out_specs=pl.BlockSpec((tm, D_MODEL), lambda i, f: (i, 0)),
        out_shape=jax.ShapeDtypeStruct((n, D_MODEL), F32),
        scratch_shapes=[pltpu.VMEM((tm, D_MODEL), BF16), pltpu.VMEM((tm, D_MODEL), F32)],
        compiler_params=_cparams("parallel", "arbitrary"),
        name="ffn",
    )(h, nw, wg, wu, wd)


def _final_norm_kernel(h_ref, nw_ref, out_ref):
    out_ref[...] = _rms(h_ref[...], nw_ref[...])


def _final_norm(h, nw):
    n = h.shape[0]
    tm = _pick(n, (1024, 512, 256, 128, 64))
    return pl.pallas_call(
        _final_norm_kernel,
        grid=(n // tm,),
        in_specs=[pl.BlockSpec((tm, D_MODEL), lambda i: (i, 0)), pl.BlockSpec(nw.shape, lambda i: (0, 0))],
        out_specs=pl.BlockSpec((tm, D_MODEL), lambda i: (i, 0)),
        out_shape=jax.ShapeDtypeStruct((n, D_MODEL), F32),
        compiler_params=_cparams("parallel"),
        name="final_norm",
    )(h, nw)


def _gla_kernel(x_ref, gup_ref, gb_ref, nw_ref, y_ref, st_ref, la_ref):
    tb = x_ref.shape[0]

    @pl.when(pl.program_id(1) == 0)
    def _():
        st_ref[...] = jnp.zeros(st_ref.shape, F32)

    z = _dot(x_ref[:, 768:896], gup_ref[...]) + gb_ref[...]
    la_ref[...] = -_softplus(-z) * (1.0 / GLA_TAU)

    ltri = _ltri(CHUNK)
    qmasks = _lane_masks(N_HEADS * GLA_DK, GLA_DK)
    vmasks = _lane_masks(N_HEADS * GLA_DV, GLA_DV)
    bd = (_div(_iota((256, 128), 0), GLA_DV) == _div(_iota((256, 128), 1), GLA_DK)).astype(F32)
    hsum = _head_ones(N_HEADS * GLA_DV, GLA_DV)
    nw = nw_ref[...]

    def chunk(c, carry):
        r0 = pl.multiple_of(c * CHUNK, CHUNK)
        rows = pl.ds(r0, CHUNK)
        g = _dot01_left(ltri, la_ref[rows, :])
        q = x_ref[rows, 0:128] * (GLA_DK ** -0.5)
        k = x_ref[rows, 128:256]
        v = x_ref[rows, 256:512]
        st = st_ref[...]
        o_inter = _dot_nt(q * jnp.exp(g), st)
        intra = []
        for s in range(CHUNK // SUB):
            lo, hi = s * SUB, (s + 1) * SUB
            gs = jnp.zeros((1, 128), F32) if s == 0 else g[lo - 1:lo]
            qs = q[lo:hi] * jnp.exp(g[lo:hi] - gs)
            kt = k[:hi] * jnp.exp(jnp.minimum(gs - g[:hi], EXP_CLAMP))
            sc = _dot_nt(_stack(qs, qmasks), kt)
            causal = _iota(sc.shape, 1) <= lo + (_iota(sc.shape, 0) & (SUB - 1))
            p = _dot(jnp.where(causal, sc, 0.0), v[:hi])
            intra.append(sum(p[h * SUB:(h + 1) * SUB] * vmasks[h] for h in range(N_HEADS)))
        o = o_inter + jnp.concatenate(intra, axis=0)
        g_last = g[CHUNK - 1:CHUNK]
        k_end = k * jnp.exp(g_last - g)
        st_ref[...] = st * jnp.exp(g_last) + bd * _dot_tn(v, k_end)
        ms = _dot01_right(o * o, hsum) * (1.0 / GLA_DV)
        y = o * lax.rsqrt(ms + NORM_EPS) * nw * _silu(x_ref[rows, 512:768])
        y_ref[rows, :] = y.astype(y_ref.dtype)
        return carry

    lax.fori_loop(0, tb // CHUNK, chunk, 0)


def _gla(x, gup, gb, nw):
    b, tp, _ = x.shape
    tb = _pick(tp, (640, 128, 64))
    full = lambda a: pl.BlockSpec(a.shape, lambda i, j: (0, 0))
    return pl.pallas_call(
        _gla_kernel,
        grid=(b, tp // tb),
        in_specs=[pl.BlockSpec((None, tb, W_GLA), lambda i, j: (i, j, 0)), full(gup), full(gb), full(nw)],
        out_specs=pl.BlockSpec((None, tb, 256), lambda i, j: (i, j, 0)),
        out_shape=jax.ShapeDtypeStruct((b, tp, 256), BF16),
        scratch_shapes=[pltpu.VMEM((256, 128), F32), pltpu.VMEM((tb, 128), F32)],
        compiler_params=_cparams("parallel", "arbitrary"),
        name="gla",
    )(x, gup, gb, nw)


def _gdn_kernel(x_ref, cw_ref, alog_ref, dtb_ref, nw_ref, y_ref,
                s_ref, xp_ref, q_ref, k_ref, v_ref, beta_ref, gd_ref):
    tb = x_ref.shape[0]
    first = pl.program_id(1) == 0

    @pl.when(first)
    def _():
        s_ref[...] = jnp.zeros(s_ref.shape, F32)
        xp_ref[0:8, :] = jnp.zeros((8, 768), F32)

    @pl.when(jnp.logical_not(first))
    def _():
        xp_ref[0:8, :] = xp_ref[tb:tb + 8, :]

    xp_ref[8:tb + 8, :] = x_ref[:, 0:768]
    conv = sum(cw_ref[j:j + 1, :] * xp_ref[8 - (GDN_CONV - 1) + j:8 - (GDN_CONV - 1) + j + tb, :]
               for j in range(GDN_CONV))
    c = _silu(conv)
    hsum = _head_ones(256, GDN_DK)
    q = c[:, 0:256]
    k = c[:, 256:512]
    q_ref[...] = q * lax.rsqrt(_dot01_right(q * q, hsum) + L2_EPS) * (GDN_DK ** -0.5)
    k_ref[...] = k * lax.rsqrt(_dot01_right(k * k, hsum) + L2_EPS)
    v_ref[...] = c[:, 512:768]
    gates = x_ref[:, 1024:1152]
    beta_ref[...] = _sigmoid(gates)
    gd_ref[...] = -jnp.exp(alog_ref[...]) * _softplus(gates + dtb_ref[...])

    ltri = _ltri(CHUNK)
    masks = _lane_masks(256, GDN_DK)
    expand = lambda off: (_iota((128, 256), 0) == _div(_iota((128, 256), 1), GDN_DK) + off).astype(BF16)
    exp_beta, exp_g = expand(0), expand(N_HEADS)
    r = _iota((256, 256), 0)
    cidx = _iota((256, 256), 1)
    same = _div(r, CHUNK) == _div(cidx, CHUNK)
    incl = jnp.logical_and(same, r >= cidx)
    strict = jnp.logical_and(same, r > cidx)
    eye = (r == cidx).astype(F32)
    bd = same.astype(F32)
    nw = nw_ref[...]

    def chunk(ci, carry):
        r0 = pl.multiple_of(ci * CHUNK, CHUNK)
        rows = pl.ds(r0, CHUNK)
        q = q_ref[rows, :]
        k = k_ref[rows, :]
        v = v_ref[rows, :]
        bexp = _dot01_right(beta_ref[rows, :], exp_beta)
        gexp = _dot01_right(_dot01_left(ltri, gd_ref[rows, :]), exp_g)
        gst = _stack(gexp, masks)
        gcol = jnp.sum(gst, axis=1, keepdims=True) * (1.0 / GDN_DK)
        grow = jnp.sum(eye * gcol, axis=0, keepdims=True)
        dec = jnp.exp(jnp.minimum(gcol - grow, 0.0))
        kst = _stack(k, masks)
        kb = k * bexp
        m = jnp.where(strict, _dot_nt(_stack(kb, masks), kst) * dec, 0.0)
        attn = jnp.where(incl, _dot_nt(_stack(q, masks), kst) * dec, 0.0)
        t = _neumann_inverse(-m, eye)
        u = _unstack(_dot(t, _stack(v * bexp, masks)), CHUNK)
        w = _unstack(_dot(t, _stack(kb * jnp.exp(gexp), masks)), CHUNK)
        s = s_ref[...]
        v_new = u - _dot(w, s)
        o = _dot(q * jnp.exp(gexp), s) + _unstack(_dot(attn, _stack(v_new, masks)), CHUNK)
        g_last = gexp[CHUNK - 1:CHUNK]
        k_end = k * jnp.exp(g_last - gexp)
        s_ref[...] = s * jnp.exp(g_last) + bd * _dot_tn(k_end, v_new)
        ms = _dot01_right(o * o, hsum) * (1.0 / GDN_DK)
        y = o * lax.rsqrt(ms + NORM_EPS) * nw * _silu(x_ref[rows, 768:1024])
        y_ref[rows, :] = y.astype(y_ref.dtype)
        return carry

    lax.fori_loop(0, tb // CHUNK, chunk, 0)


def _gdn(x, cw, alog, dtb, nw):
    b, tp, _ = x.shape
    tb = _pick(tp, (640, 128, 64))
    full = lambda a: pl.BlockSpec(a.shape, lambda i, j: (0, 0))
    return pl.pallas_call(
        _gdn_kernel,
        grid=(b, tp // tb),
        in_specs=[pl.BlockSpec((None, tb, W_GDN), lambda i, j: (i, j, 0)),
                  full(cw), full(alog), full(dtb), full(nw)],
        out_specs=pl.BlockSpec((None, tb, 256), lambda i, j: (i, j, 0)),
        out_shape=jax.ShapeDtypeStruct((b, tp, 256), BF16),
        scratch_shapes=[pltpu.VMEM((256, 256), F32), pltpu.VMEM((tb + 8, 768), F32),
                        pltpu.VMEM((tb, 256), F32), pltpu.VMEM((tb, 256), F32), pltpu.VMEM((tb, 256), F32),
                        pltpu.VMEM((tb, 128), F32), pltpu.VMEM((tb, 128), F32)],
        compiler_params=_cparams("parallel", "arbitrary"),
        name="gdn",
    )(x, cw, alog, dtb, nw)


def _rwkv_kernel(x_ref, mu_ref, w2a2_ref, w0_ref, a0_ref, g2_ref, kk_ref, ka_ref, rk_ref,
                 lnw_ref, lnb_ref, y_ref,
                 s_ref, xp_ref, r_s, k_s, v_s, kk_s, b_s, lw_s, g_s):
    tb = x_ref.shape[0]
    first = pl.program_id(1) == 0

    @pl.when(first)
    def _():
        s_ref[...] = jnp.zeros(s_ref.shape, F32)
        xp_ref[0:8, :] = jnp.zeros((8, W_RWKV), F32)

    @pl.when(jnp.logical_not(first))
    def _():
        xp_ref[0:8, :] = xp_ref[tb:tb + 8, :]

    x = x_ref[...]
    xp_ref[8:tb + 8, :] = x
    z = x + (xp_ref[7:tb + 7, :] - x) * mu_ref[...]
    r = z[:, 0:256]
    k = z[:, 256:512]
    wa = z[:, 768:896]
    wa = jnp.where(_iota(wa.shape, 1) < 64, jnp.tanh(wa), wa)
    pre = _dot(wa, w2a2_ref[...])
    w_log = -_softplus(-(w0_ref[...] + pre[:, 0:256])) - 0.5
    a = _sigmoid(a0_ref[...] + pre[:, 256:512])
    hsum = _head_ones(256, RWKV_N)
    kkv = k * kk_ref[...]
    kkn = kkv * lax.rsqrt(_dot01_right(kkv * kkv, hsum) + L2_EPS)
    r_s[...] = r
    k_s[...] = k * (1.0 + (a - 1.0) * ka_ref[...])
    v_s[...] = z[:, 512:768]
    kk_s[...] = kkn
    b_s[...] = kkn * a
    lw_s[...] = -jnp.exp(w_log)
    g_s[...] = _dot(_sigmoid(z[:, 896:1152]), g2_ref[...])

    ltri = _ltri(CHUNK)
    masks = _lane_masks(256, RWKV_N)
    rr = _iota((256, 256), 0)
    cc = _iota((256, 256), 1)
    same = _div(rr, CHUNK) == _div(cc, CHUNK)
    incl = jnp.logical_and(same, rr >= cc)
    strict = jnp.logical_and(same, rr > cc)
    eye = (rr == cc).astype(F32)
    bd = same.astype(F32)
    rk = rk_ref[...]
    lnw = lnw_ref[...]
    lnb = lnb_ref[...]

    def chunk(ci, carry):
        r0 = pl.multiple_of(ci * CHUNK, CHUNK)
        rows = pl.ds(r0, CHUNK)
        r = r_s[rows, :]
        k = k_s[rows, :]
        v = v_s[rows, :]
        kk = kk_s[rows, :]
        b = b_s[rows, :]
        lw = lw_s[rows, :]
        gl = _dot01_left(ltri, lw)
        g_last = gl[CHUNK - 1:CHUNK]
        e_neg = jnp.exp(-gl)
        a_t = -kk * jnp.exp(gl - lw)
        r_t = r * jnp.exp(gl)
        lhs = jnp.concatenate([_stack(a_t, masks), _stack(r_t, masks)], axis=0)
        rhs = jnp.concatenate([_stack(b * e_neg, masks), _stack(k * e_neg, masks)], axis=0)
        amat = _dot_nt(lhs, rhs)
        n4 = 4 * CHUNK
        a_ab = jnp.where(strict, amat[0:n4, 0:n4], 0.0)
        a_ak = jnp.where(strict, amat[0:n4, n4:], 0.0)
        a_rb = jnp.where(incl, amat[n4:, 0:n4], 0.0)
        a_rk = jnp.where(incl, amat[n4:, n4:], 0.0)
        t = _neumann_inverse(a_ab, eye)
        vst = _stack(v, masks)
        z_st = _dot(t, _dot(a_ak, vst))
        at_st = _dot(t, _stack(a_t, masks))
        s = s_ref[...]
        u_st = _dot_nt(at_st, s) + z_st
        y = _dot_nt(r_t, s) + _unstack(_dot(a_rb, u_st) + _dot(a_rk, vst), CHUNK)
        u = _unstack(u_st, CHUNK)
        e_end = jnp.exp(g_last - gl)
        s_ref[...] = s * jnp.exp(g_last) + bd * (_dot_tn(u, b * e_end) + _dot_tn(v, k * e_end))
        mean = _dot01_right(y, hsum) * (1.0 / RWKV_N)
        d = y - mean
        var = _dot01_right(d * d, hsum) * (1.0 / RWKV_N)
        yn = d * lax.rsqrt(var + RWKV_LN_EPS) * lnw + lnb
        bonus = _dot01_right(r * k * rk, hsum) * v
        y_ref[rows, :] = ((yn + bonus) * g_s[rows, :]).astype(y_ref.dtype)
        return carry

    lax.fori_loop(0, tb // CHUNK, chunk, 0)


def _rwkv(x, mu, w2a2, w0, a0, g2, kk, ka, rk, lnw, lnb):
    b, tp, _ = x.shape
    tb = _pick(tp, (640, 128, 64))
    full = lambda a: pl.BlockSpec(a.shape, lambda i, j: (0, 0))
    small = (mu, w2a2, w0, a0, g2, kk, ka, rk, lnw, lnb)
    return pl.pallas_call(
        _rwkv_kernel,
        grid=(b, tp // tb),
        in_specs=[pl.BlockSpec((None, tb, W_RWKV), lambda i, j: (i, j, 0))] + [full(a) for a in small],
        out_specs=pl.BlockSpec((None, tb, 256), lambda i, j: (i, j, 0)),
        out_shape=jax.ShapeDtypeStruct((b, tp, 256), BF16),
        scratch_shapes=[pltpu.VMEM((256, 256), F32), pltpu.VMEM((tb + 8, W_RWKV), F32)]
        + [pltpu.VMEM((tb, 256), F32) for _ in range(7)],
        compiler_params=_cparams("parallel", "arbitrary"),
        name="rwkv",
    )(x, *small)


def _mla_prep_kernel(x_ref, qnw_ref, kvnw_ref, wqa_ref, wqb_ref, wk_ref, wv_ref, ea_ref, eb_ref,
                     c1_ref, s1_ref, q_ref, k_ref, v_ref):
    x = x_ref[...]
    qn = _rms(x[:, 0:256], qnw_ref[...]).astype(BF16)
    kvn = _rms(x[:, 256:384], kvnw_ref[...]).astype(BF16)
    kpe = x[:, 384:512].astype(BF16)
    c1 = c1_ref[...]
    s1 = s1_ref[...]
    qa = jnp.dot(qn, wqa_ref[...], preferred_element_type=F32)
    qb = jnp.dot(qn, wqb_ref[...], preferred_element_type=F32)
    kn = jnp.dot(kvn, wk_ref[...], preferred_element_type=F32)
    kp = (jnp.dot(kpe, ea_ref[...], preferred_element_type=F32) * c1
          + jnp.dot(kpe, eb_ref[...], preferred_element_type=F32) * s1)
    v_ref[...] = jnp.dot(kvn, wv_ref[...], preferred_element_type=F32).astype(BF16)
    for h in range(N_HEADS):
        sl = slice(h * MLA_SLOT, (h + 1) * MLA_SLOT)
        q_ref[:, sl] = (qa[:, sl] * c1 + qb[:, sl] * s1).astype(BF16)
        k_ref[:, sl] = (kn[:, sl] + kp).astype(BF16)


def _mla_prep(x, qnw, kvnw, wqa, wqb, wk, wv, ea, eb, c1, s1):
    b, tp, _ = x.shape
    tm = _pick(tp, (640, 128, 64))
    full = lambda a: pl.BlockSpec(a.shape, lambda i, j: (0, 0))
    wide = N_HEADS * MLA_SLOT
    out = pl.BlockSpec((None, tm, wide), lambda i, j: (i, j, 0))
    tab = pl.BlockSpec((tm, MLA_SLOT), lambda i, j: (j, 0))
    return pl.pallas_call(
        _mla_prep_kernel,
        grid=(b, tp // tm),
        in_specs=[pl.BlockSpec((None, tm, W_MLA), lambda i, j: (i, j, 0))]
        + [full(a) for a in (qnw, kvnw, wqa, wqb, wk, wv, ea, eb)] + [tab, tab],
        out_specs=[out, out, out],
        out_shape=[jax.ShapeDtypeStruct((b, tp, wide), BF16)] * 3,
        compiler_params=_cparams("parallel", "parallel"),
        name="mla_prep",
    )(x, qnw, kvnw, wqa, wqb, wk, wv, ea, eb, c1, s1)


def _flash_kernel(q_ref, k_ref, v_ref, o_ref, m_ref, l_ref, acc_ref):
    qi = pl.program_id(2)
    ki = pl.program_id(3)
    tq = q_ref.shape[0]
    tk = k_ref.shape[0]

    @pl.when(ki == 0)
    def _():
        m_ref[...] = jnp.full(m_ref.shape, NEG_INF, F32)
        l_ref[...] = jnp.zeros(l_ref.shape, F32)
        acc_ref[...] = jnp.zeros(acc_ref.shape, F32)

    @pl.when(ki <= qi)
    def _():
        s = lax.dot_general(q_ref[...], k_ref[...], (((1,), (1,)), ((), ())), preferred_element_type=F32)
        visible = qi * tq + _iota(s.shape, 0) >= ki * tk + _iota(s.shape, 1)
        s = jnp.where(visible, s, NEG_INF)
        m_old = m_ref[...]
        m_new = jnp.maximum(m_old, jnp.max(s, axis=-1, keepdims=True))
        alpha = jnp.exp(m_old - m_new)
        p = jnp.exp(s - m_new)
        l_ref[...] = alpha * l_ref[...] + jnp.sum(p, axis=-1, keepdims=True)
        acc_ref[...] = alpha * acc_ref[...] + jnp.dot(p.astype(BF16), v_ref[...], preferred_element_type=F32)
        m_ref[...] = m_new

    @pl.when(ki == pl.num_programs(3) - 1)
    def _():
        o_ref[...] = (acc_ref[...] / l_ref[...]).astype(o_ref.dtype)


def _flash(q, k, v):
    b, tp, wide = q.shape
    t = _pick(tp, (640, 128, 64))
    nb = tp // t
    qspec = pl.BlockSpec((None, t, MLA_SLOT), lambda i, h, qi, ki: (i, qi, h))
    kspec = pl.BlockSpec((None, t, MLA_SLOT), lambda i, h, qi, ki: (i, jnp.minimum(ki, qi), h))
    return pl.pallas_call(
        _flash_kernel,
        grid=(b, N_HEADS, nb, nb),
        in_specs=[qspec, kspec, kspec],
        out_specs=qspec,
        out_shape=jax.ShapeDtypeStruct((b, tp, wide), BF16),
        scratch_shapes=[pltpu.VMEM((t, 1), F32), pltpu.VMEM((t, 1), F32), pltpu.VMEM((t, MLA_SLOT), F32)],
        compiler_params=_cparams("parallel", "parallel", "parallel", "arbitrary"),
        name="mla_flash",
    )(q, k, v)


def _router_kernel(h_ref, nw_ref, wr_ref, xn_ref, info_ref):
    xn = _rms(h_ref[...], nw_ref[...])
    xn_ref[...] = xn.astype(BF16)
    logits = jnp.dot(xn, wr_ref[...], preferred_element_type=F32, precision=lax.Precision.HIGHEST)
    lane = _iota(logits.shape, 1).astype(F32)
    valid = lane < N_EXPERTS
    l0 = jnp.where(valid, logits, NEG_INF)
    m1 = jnp.max(l0, axis=-1, keepdims=True)
    i1 = jnp.min(jnp.where(l0 == m1, lane, float(LANE)), axis=-1, keepdims=True)
    l1 = jnp.where(lane == i1, NEG_INF, l0)
    m2 = jnp.max(l1, axis=-1, keepdims=True)
    i2 = jnp.min(jnp.where(l1 == m2, lane, float(LANE)), axis=-1, keepdims=True)
    e2 = jnp.exp(m2 - m1)
    g1 = 1.0 / (1.0 + e2)
    g2 = e2 / (1.0 + e2)
    info = jnp.where(lane == 0, i1, 0.0)
    info = jnp.where(lane == 1, i2, info)
    info = jnp.where(lane == 2, g1, info)
    info = jnp.where(lane == 3, g2, info)
    info_ref[...] = info


def _router(h, nw, wr):
    n = h.shape[0]
    tm = _pick(n, (512, 256, 128, 64))
    return pl.pallas_call(
        _router_kernel,
        grid=(n // tm,),
        in_specs=[pl.BlockSpec((tm, D_MODEL), lambda i: (i, 0)), pl.BlockSpec(nw.shape, lambda i: (0, 0)),
                  pl.BlockSpec(wr.shape, lambda i: (0, 0))],
        out_specs=[pl.BlockSpec((tm, D_MODEL), lambda i: (i, 0)), pl.BlockSpec((tm, LANE), lambda i: (i, 0))],
        out_shape=[jax.ShapeDtypeStruct((n, D_MODEL), BF16), jax.ShapeDtypeStruct((n, LANE), F32)],
        compiler_params=_cparams("parallel"),
        name="moe_router",
    )(h, nw, wr)


def _expert_kernel(be_ref, x_ref, gate_ref, wg_ref, wu_ref, wd_ref, out_ref, acc_ref):
    f = pl.program_id(1)

    @pl.when(f == 0)
    def _():
        acc_ref[...] = jnp.zeros(acc_ref.shape, F32)

    xb = x_ref[...]
    a = jnp.dot(xb, wg_ref[...], preferred_element_type=F32)
    b = jnp.dot(xb, wu_ref[...], preferred_element_type=F32)
    acc_ref[...] += jnp.dot((_silu(a) * b).astype(BF16), wd_ref[...], preferred_element_type=F32)

    @pl.when(f == pl.num_programs(1) - 1)
    def _():
        out_ref[...] = acc_ref[...] * gate_ref[...]


def _experts(block_expert, x_rows, row_gate, wg, wu, wd, tm):
    cap = x_rows.shape[0]
    dff = wg.shape[2]
    tf = _pick(dff, (1792, 512, 256, 128))
    grid_spec = pltpu.PrefetchScalarGridSpec(
        num_scalar_prefetch=1,
        grid=(cap // tm, dff // tf),
        in_specs=[pl.BlockSpec((tm, D_MODEL), lambda i, f, be: (i, 0)),
                  pl.BlockSpec((tm, 1), lambda i, f, be: (i, 0)),
                  pl.BlockSpec((None, D_MODEL, tf), lambda i, f, be: (be[i], 0, f)),
                  pl.BlockSpec((None, D_MODEL, tf), lambda i, f, be: (be[i], 0, f)),
                  pl.BlockSpec((None, tf, D_MODEL), lambda i, f, be: (be[i], f, 0))],
        out_specs=pl.BlockSpec((tm, D_MODEL), lambda i, f, be: (i, 0)),
        scratch_shapes=[pltpu.VMEM((tm, D_MODEL), F32)],
    )
    return pl.pallas_call(
        _expert_kernel,
        grid_spec=grid_spec,
        out_shape=jax.ShapeDtypeStruct((cap, D_MODEL), F32),
        compiler_params=_cparams("parallel", "arbitrary"),
        name="moe_experts",
    )(block_expert, x_rows, row_gate, wg, wu, wd)


def _combine_kernel(h_ref, ya_ref, yb_ref, out_ref):
    out_ref[...] = h_ref[...] + (ya_ref[...] + yb_ref[...])


def _combine(h, ya, yb):
    n = h.shape[0]
    tm = _pick(n, (1024, 512, 256, 128, 64))
    spec = pl.BlockSpec((tm, D_MODEL), lambda i: (i, 0))
    return pl.pallas_call(
        _combine_kernel, grid=(n // tm,), in_specs=[spec, spec, spec], out_specs=spec,
        out_shape=jax.ShapeDtypeStruct((n, D_MODEL), F32),
        compiler_params=_cparams("parallel"), name="moe_combine",
    )(h, ya, yb)


def _moe(h, nw, router, wg, wu, wd):
    n = h.shape[0]
    tm = _pick(n, (512, 64))
    wr = jnp.pad(router.astype(F32), ((0, 0), (0, LANE - N_EXPERTS)))
    xn, info = _router(h, nw, wr)
    expert = info[:, 0:2].astype(jnp.int32).reshape(-1)
    gate = info[:, 2:4].reshape(-1)
    n_assign = 2 * n
    order = jnp.argsort(expert)
    counts = jnp.sum(expert[:, None] == jnp.arange(N_EXPERTS)[None, :], axis=0).astype(jnp.int32)
    padded = (counts + tm - 1) // tm * tm
    pad_end = jnp.cumsum(padded)
    pad_start = pad_end - padded
    start = jnp.cumsum(counts) - counts
    n_blocks = -(-n_assign // tm) + N_EXPERTS
    cap = n_blocks * tm
    rows = jnp.arange(cap, dtype=jnp.int32)
    row_expert = jnp.minimum(jnp.sum(rows[:, None] >= pad_end[None, :], axis=1), N_EXPERTS - 1)
    rank = rows - pad_start[row_expert]
    valid = rank < counts[row_expert]
    src = order[jnp.clip(start[row_expert] + rank, 0, n_assign - 1)]
    row_token = jnp.where(valid, src // 2, 0)
    row_gate = jnp.where(valid, gate[src], 0.0)
    block_expert = row_expert[::tm]
    sorted_pos = jnp.zeros((n_assign,), jnp.int32).at[order].set(jnp.arange(n_assign, dtype=jnp.int32))
    dest = pad_start[expert] + sorted_pos - start[expert]
    x_rows = jnp.take(xn, row_token, axis=0)
    y_rows = _experts(block_expert, x_rows, row_gate[:, None], wg, wu, wd, tm)
    dest = dest.reshape(n, 2)
    return _combine(h, jnp.take(y_rows, dest[:, 0], axis=0), jnp.take(y_rows, dest[:, 1], axis=0))


def _pad_cols(a, width):
    return jnp.pad(a, ((0, 0), (0, width - a.shape[1])))


def _row(a, width=None):
    a = a.reshape(1, -1).astype(F32)
    return a if width is None else _pad_cols(a, width)


def _rope_tables(tp):
    pos = jnp.arange(tp, dtype=F32)
    inv_freq = ROPE_THETA ** (-jnp.arange(0, MLA_ROPE, 2, dtype=F32) / MLA_ROPE)
    ang = pos[:, None] * inv_freq[None, :]
    cos, sin = jnp.cos(ang), jnp.sin(ang)
    ones = jnp.ones((tp, MLA_NOPE), F32)
    zeros = jnp.zeros((tp, MLA_SLOT - MLA_NOPE - MLA_ROPE), F32)
    c1 = jnp.concatenate([ones, cos, cos, zeros], axis=1)
    s1 = jnp.concatenate([0.0 * ones, -sin, sin, zeros], axis=1)
    return c1, s1


def _mla_weights(w_uq, w_ukv):
    half = MLA_ROPE // 2
    scale = (MLA_NOPE + MLA_ROPE) ** -0.5
    zq = jnp.zeros((w_uq.shape[0], MLA_SLOT - MLA_NOPE - MLA_ROPE), F32)
    zn = jnp.zeros((w_uq.shape[0], MLA_NOPE), F32)
    zk = jnp.zeros((w_ukv.shape[0], MLA_SLOT - MLA_NOPE), F32)
    wqa, wqb, wk, wv = [], [], [], []
    for h in range(N_HEADS):
        q = w_uq[:, h * 96:(h + 1) * 96] * scale
        nope, x1, x2 = q[:, :MLA_NOPE], q[:, MLA_NOPE:MLA_NOPE + half], q[:, MLA_NOPE + half:]
        wqa += [nope, x1, x2, zq]
        wqb += [zn, x2, x1, zq]
        kv = w_ukv[:, h * 128:(h + 1) * 128]
        wk += [kv[:, :MLA_NOPE], zk]
        wv += [kv[:, MLA_NOPE:], zk]
    cat = lambda parts: jnp.concatenate(parts, axis=1).astype(BF16)
    ea = np.zeros((MLA_SLOT, MLA_SLOT), np.float32)
    eb = np.zeros((MLA_SLOT, MLA_SLOT), np.float32)
    for i in range(MLA_ROPE):
        ea[i, MLA_NOPE + i] = 1.0
        eb[(i + half) % MLA_ROPE, MLA_NOPE + i] = 1.0
    return cat(wqa), cat(wqb), cat(wk), cat(wv), jnp.asarray(ea, BF16), jnp.asarray(eb, BF16)


def _branch_weights(w_branch):
    wb = w_branch.astype(BF16)
    z = jnp.zeros((MLA_SLOT - MLA_V, D_MODEL), BF16)
    parts = []
    for h in range(N_HEADS):
        parts += [wb[1, h * MLA_V:(h + 1) * MLA_V], z]
    return wb[0], jnp.concatenate(parts, axis=0), wb[2], wb[3]


def _token_mixing(h, b, tp, p, l, tables):
    w_in = p["w_in"][l]
    col = lambda i, j: w_in[:, _OFF[i]:_OFF[j]]
    wg = jnp.concatenate([col(0, 3), col(4, 5), _pad_cols(col(3, 4), LANE)], axis=1).astype(BF16)
    wm = jnp.concatenate([col(5, 7), _pad_cols(col(7, 8), LANE)], axis=1).astype(BF16)
    wd = jnp.concatenate([col(8, 10), _pad_cols(col(10, 12), LANE)], axis=1).astype(BF16)
    wr = _pad_cols(col(12, 13), W_RWKV).astype(BF16)
    nw = _row(p["norm_mix"][l])
    xg, xm, xd, xr = _inproj(h, nw, wg, wm, wd, wr)
    shape3 = lambda a: a.reshape(b, tp, a.shape[1])

    gup = jnp.pad(p["gla_gate_up"][l], ((0, LANE - 16), (0, 0))).astype(BF16)
    y_gla = _gla(shape3(xg), gup, _row(p["gla_gate_bias"][l]), _row(jnp.tile(p["gla_norm"][l], N_HEADS)))

    wqa, wqb, wk, wv, ea, eb = _mla_weights(p["mla_w_uq"][l], p["mla_w_ukv"][l])
    q, k, v = _mla_prep(shape3(xm), _row(p["mla_q_norm"][l]), _row(p["mla_kv_norm"][l]),
                        wqa, wqb, wk, wv, ea, eb, *tables)
    y_mla = _flash(q, k, v)

    lanes4 = lambda a: jnp.pad(a.reshape(1, N_HEADS).astype(F32), ((0, 0), (N_HEADS, LANE - 2 * N_HEADS)))
    y_gdn = _gdn(shape3(xd), p["gdn_conv"][l].astype(F32), lanes4(p["gdn_a_log"][l]),
                 lanes4(p["gdn_dt_bias"][l]), _row(jnp.tile(p["gdn_norm"][l], N_HEADS)))

    w2a2 = jnp.zeros((LANE, 512), F32)
    w2a2 = w2a2.at[0:64, 0:256].set(p["rwkv_w2"][l]).at[64:128, 256:512].set(p["rwkv_a2"][l]).astype(BF16)
    g2 = jnp.pad(p["rwkv_g2"][l], ((0, 256 - 160), (0, 0))).astype(BF16)
    y_rwkv = _rwkv(shape3(xr), _row(p["rwkv_mu"][l], W_RWKV), w2a2, _row(p["rwkv_w0"][l]),
                   _row(p["rwkv_a0"][l]), g2, _row(p["rwkv_k_k"][l]), _row(p["rwkv_k_a"][l]),
                   _row(p["rwkv_r_k"][l]), _row(p["rwkv_ln_w"][l]), _row(p["rwkv_ln_b"][l]))

    flat = lambda a: a.reshape(b * tp, a.shape[2])
    ys = [flat(y_gla), flat(y_mla), flat(y_gdn), flat(y_rwkv)]
    return _merge(h, nw, col(13, 14).astype(BF16), ys, _branch_weights(p["w_branch"][l]),
                  p["w_out"][l].astype(BF16))


def kernel(x, meta_tokens, norm_mix, w_in, gla_gate_up, gla_gate_bias, gla_norm, mla_q_norm, mla_w_uq, mla_kv_norm, mla_w_ukv, gdn_conv, gdn_a_log, gdn_dt_bias, gdn_norm, rwkv_mu, rwkv_w0, rwkv_w2, rwkv_a0, rwkv_a2, rwkv_g2, rwkv_k_k, rwkv_k_a, rwkv_r_k, rwkv_ln_w, rwkv_ln_b, w_branch, w_out, norm_ffn, ffn_w_gate, ffn_w_up, ffn_w_down, moe_router, moe_w_gate, moe_w_up, moe_w_down, norm_final):
    p = dict(norm_mix=norm_mix, w_in=w_in, gla_gate_up=gla_gate_up, gla_gate_bias=gla_gate_bias,
             gla_norm=gla_norm, mla_q_norm=mla_q_norm, mla_w_uq=mla_w_uq, mla_kv_norm=mla_kv_norm,
             mla_w_ukv=mla_w_ukv, gdn_conv=gdn_conv, gdn_a_log=gdn_a_log, gdn_dt_bias=gdn_dt_bias,
             gdn_norm=gdn_norm, rwkv_mu=rwkv_mu, rwkv_w0=rwkv_w0, rwkv_w2=rwkv_w2, rwkv_a0=rwkv_a0,
             rwkv_a2=rwkv_a2, rwkv_g2=rwkv_g2, rwkv_k_k=rwkv_k_k, rwkv_k_a=rwkv_k_a, rwkv_r_k=rwkv_r_k,
             rwkv_ln_w=rwkv_ln_w, rwkv_ln_b=rwkv_ln_b, w_branch=w_branch, w_out=w_out)
    b, seq, d = x.shape
    t_real = N_META + seq
    tp = -(-t_real // 128) * 128
    meta = jnp.broadcast_to(meta_tokens[None].astype(x.dtype), (b, N_META, d))
    h = jnp.concatenate([meta, x, jnp.zeros((b, tp - t_real, d), x.dtype)], axis=1).reshape(b * tp, d)
    tables = _rope_tables(tp)
    depth = norm_mix.shape[0]
    for l in range(depth):
        h = _token_mixing(h, b, tp, p, l, tables)
        nw = _row(norm_ffn[l])
        if l % 2 == 0:
            h = _ffn(h, nw, ffn_w_gate[l // 2].astype(BF16), ffn_w_up[l // 2].astype(BF16),
                     ffn_w_down[l // 2].astype(BF16))
        else:
            h = _moe(h, nw, moe_router[l // 2], moe_w_gate[l // 2].astype(BF16),
                     moe_w_up[l // 2].astype(BF16), moe_w_down[l // 2].astype(BF16))
    out = _final_norm(h, _row(norm_final))
    return out.reshape(b, tp, d)[:, N_META:t_real]
```

```python
import functools
import math

import jax
import jax.numpy as jnp
import numpy as np
from jax import lax
from jax.experimental import pallas as pl
from jax.experimental.pallas import tpu as pltpu

F32 = jnp.float32
BF16 = jnp.bfloat16

D_MODEL = 1024
N_META = 16
N_HEADS = 4
GLA_DK = 32
GLA_DV = 64
GLA_TAU = 16.0
MLA_NOPE = 64
MLA_ROPE = 32
MLA_V = 64
MLA_SLOT = 128
ROPE_THETA = 10000.0
GDN_DK = 64
GDN_CONV = 4
RWKV_N = 64
RWKV_LN_EPS = RWKV_N * 1e-5
CHUNK = 64
SUB = 16
PREP_GROUP = 5
N_EXPERTS = 8
NORM_EPS = 1e-6
L2_EPS = 1e-6
NEG_INF = -1e30
EXP_CLAMP = 80.0

LANE = 128
VMEM_LIMIT = 56 * 1024 * 1024

_OFF = np.cumsum([0, 128, 128, 256, 16, 256, 256, 128, 32, 768, 256, 4, 4, 1056, 4096]).tolist()
W_GLA, W_MLA, W_GDN, W_RWKV = 896, 512, 1152, 1152


def _cparams(*sem):
    return pltpu.CompilerParams(dimension_semantics=sem, vmem_limit_bytes=VMEM_LIMIT)


def _pick(n, prefs):
    for p in prefs:
        if n % p == 0:
            return p
    raise ValueError(f"no tile for {n}")


def _dot(a, b):
    return jnp.dot(a.astype(BF16), b.astype(BF16), preferred_element_type=F32)


def _dot_nt(a, b):
    return lax.dot_general(a.astype(BF16), b.astype(BF16), (((1,), (1,)), ((), ())),
                           preferred_element_type=F32)


def _dot_tn(a, b):
    return lax.dot_general(a.astype(BF16), b.astype(BF16), (((0,), (0,)), ((), ())),
                           preferred_element_type=F32)


def _split3(x):
    hi = x.astype(BF16)
    r1 = x - hi.astype(F32)
    mid = r1.astype(BF16)
    lo = (r1 - mid.astype(F32)).astype(BF16)
    return hi, mid, lo


def _dot01_left(m01, x):
    return sum(jnp.dot(m01, p, preferred_element_type=F32) for p in _split3(x))


def _dot01_right(x, m01):
    return sum(jnp.dot(p, m01, preferred_element_type=F32) for p in _split3(x))


def _iota(shape, dim):
    return lax.broadcasted_iota(jnp.int32, shape, dim)


def _div(x, w):
    return x >> int(math.log2(w))


def _ltri(n):
    return (_iota((n, n), 0) >= _iota((n, n), 1)).astype(BF16)


def _head_ones(n, w):
    return (_div(_iota((n, n), 0), w) == _div(_iota((n, n), 1), w)).astype(BF16)


def _lane_masks(width, w):
    lane = _div(_iota((1, width), 1), w)
    return [(lane == h).astype(F32) for h in range(width // w)]


def _stack(x, masks):
    return jnp.concatenate([x * m for m in masks], axis=0)


def _unstack(y, n):
    out = y[0:n]
    for h in range(1, y.shape[0] // n):
        out = out + y[h * n:(h + 1) * n]
    return out


def _rms(x, w):
    return x * lax.rsqrt(jnp.mean(x * x, axis=-1, keepdims=True) + NORM_EPS) * w


def _sigmoid(x):
    return 1.0 / (1.0 + jnp.exp(-x))


def _silu(x):
    return x * _sigmoid(x)


def _softplus(x):
    return jnp.maximum(x, 0.0) + jnp.log(1.0 + jnp.exp(-jnp.abs(x)))


def _neumann_inverse(xs, eye):
    ts = [eye + x for x in xs]
    ps = list(xs)
    for _ in range(int(math.log2(CHUNK)) - 1):
        ps = [_dot(p, p) for p in ps]
        ts = [t + _dot(t, p) for t, p in zip(ts, ps)]
    return ts


def _for_chunks(n, body, group):
    def trip(i, carry):
        body([i * group + g for g in range(group)])
        return carry

    if n >= group:
        lax.fori_loop(0, n // group, trip, 0)
    if n % group:
        body(list(range(n - n % group, n)))


def _inproj_kernel(h_ref, nw_ref, wg_ref, wm_ref, wd_ref, wr_ref, og_ref, om_ref, od_ref, or_ref):
    xb = _rms(h_ref[...], nw_ref[...]).astype(BF16)
    og_ref[...] = jnp.dot(xb, wg_ref[...], preferred_element_type=F32)
    om_ref[...] = jnp.dot(xb, wm_ref[...], preferred_element_type=F32)
    od_ref[...] = jnp.dot(xb, wd_ref[...], preferred_element_type=F32)
    or_ref[...] = jnp.dot(xb, wr_ref[...], preferred_element_type=F32)


def _inproj(h, nw, wg, wm, wd, wr):
    n = h.shape[0]
    tm = _pick(n, (512, 256, 128, 64))
    full = lambda a: pl.BlockSpec(a.shape, lambda i: (0, 0))
    row = lambda w: pl.BlockSpec((tm, w), lambda i: (i, 0))
    return pl.pallas_call(
        _inproj_kernel,
        grid=(n // tm,),
        in_specs=[row(D_MODEL), full(nw), full(wg), full(wm), full(wd), full(wr)],
        out_specs=[row(W_GLA), row(W_MLA), row(W_GDN), row(W_RWKV)],
        out_shape=[jax.ShapeDtypeStruct((n, w), F32) for w in (W_GLA, W_MLA, W_GDN, W_RWKV)],
        compiler_params=_cparams("parallel"),
        name="inproj",
    )(h, nw, wg, wm, wd, wr)


def _merge_kernel(h_ref, nw_ref, wgate_ref, yg_ref, ym_ref, yd_ref, yr_ref,
                  wbg_ref, wbm_ref, wbd_ref, wbr_ref, wout_ref, out_ref):
    x = h_ref[...]
    xb = _rms(x, nw_ref[...]).astype(BF16)
    acc = jnp.zeros(x.shape, F32)
    branches = ((yg_ref, wbg_ref), (ym_ref, wbm_ref), (yd_ref, wbd_ref), (yr_ref, wbr_ref))
    for i, (y_ref, wb_ref) in enumerate(branches):
        logits = jnp.dot(xb, wgate_ref[:, i * D_MODEL:(i + 1) * D_MODEL], preferred_element_type=F32)
        proj = jnp.dot(y_ref[...], wb_ref[...], preferred_element_type=F32)
        acc = acc + _sigmoid(logits) * proj
    out_ref[...] = x + jnp.dot(acc.astype(BF16), wout_ref[...], preferred_element_type=F32)


def _merge(h, nw, wgate, ys, wbs, wout):
    n = h.shape[0]
    tm = _pick(n, (512, 256, 128, 64))
    full = lambda a: pl.BlockSpec(a.shape, lambda i: (0, 0))
    row = lambda w: pl.BlockSpec((tm, w), lambda i: (i, 0))
    return pl.pallas_call(
        _merge_kernel,
        grid=(n // tm,),
        in_specs=[row(D_MODEL), full(nw), full(wgate)] + [row(y.shape[1]) for y in ys]
        + [full(w) for w in wbs] + [full(wout)],
        out_specs=row(D_MODEL),
        out_shape=jax.ShapeDtypeStruct((n, D_MODEL), F32),
        compiler_params=_cparams("parallel"),
        name="merge",
    )(h, nw, wgate, *ys, *wbs, wout)


def _ffn_kernel(h_ref, nw_ref, wg_ref, wu_ref, wd_ref, out_ref, xb_ref, acc_ref):
    f = pl.program_id(1)

    @pl.when(f == 0)
    def _():
        xb_ref[...] = _rms(h_ref[...], nw_ref[...]).astype(BF16)
        acc_ref[...] = jnp.zeros(acc_ref.shape, F32)

    xb = xb_ref[...]
    a = jnp.dot(xb, wg_ref[...], preferred_element_type=F32)
    b = jnp.dot(xb, wu_ref[...], preferred_element_type=F32)
    acc_ref[...] += jnp.dot((_silu(a) * b).astype(BF16), wd_ref[...], preferred_element_type=F32)

    @pl.when(f == pl.num_programs(1) - 1)
    def _():
        out_ref[...] = h_ref[...] + acc_ref[...]


def _ffn(h, nw, wg, wu, wd):
    n = h.shape[0]
    dff = wg.shape[1]
    tm = _pick(n, (512, 256, 128, 64))
    tf = _pick(dff, (1408, 512, 256, 128))
    return pl.pallas_call(
        _ffn_kernel,
        grid=(n // tm, dff // tf),
        in_specs=[pl.BlockSpec((tm, D_MODEL), lambda i, f: (i, 0)),
                  pl.BlockSpec(nw.shape, lambda i, f: (0, 0)),
                  pl.BlockSpec((D_MODEL, tf), lambda i, f: (0, f)),
                  pl.BlockSpec((D_MODEL, tf), lambda i, f: (0, f)),
                  pl.BlockSpec((tf, D_MODEL), lambda i, f: (f, 0))],
        out_specs=pl.BlockSpec((tm, D_MODEL), lambda i, f: (i, 0)),
        out_shape=jax.ShapeDtypeStruct((n, D_MODEL), F32),
        scratch_shapes=[pltpu.VMEM((tm, D_MODEL), BF16), pltpu.VMEM((tm, D_MODEL), F32)],
        compiler_params=_cparams("parallel", "arbitrary"),
        name="ffn",
    )(h, nw, wg, wu, wd)


def _final_norm_kernel(h_ref, nw_ref, out_ref):
    out_ref[...] = _rms(h_ref[...], nw_ref[...])


def _final_norm(h, nw):
    n = h.shape[0]
    tm = _pick(n, (1024, 512, 256, 128, 64))
    return pl.pallas_call(
        _final_norm_kernel,
        grid=(n // tm,),
        in_specs=[pl.BlockSpec((tm, D_MODEL), lambda i: (i, 0)), pl.BlockSpec(nw.shape, lambda i: (0, 0))],
        out_specs=pl.BlockSpec((tm, D_MODEL), lambda i: (i, 0)),
        out_shape=jax.ShapeDtypeStruct((n, D_MODEL), F32),
        compiler_params=_cparams("parallel"),
        name="final_norm",
    )(h, nw)


def _gla_kernel(x_ref, gup_ref, gb_ref, nw_ref, y_ref, st_ref, la_ref, o_s, qg_s, kv_s, gt_s):
    tb = x_ref.shape[0]

    @pl.when(pl.program_id(1) == 0)
    def _():
        st_ref[...] = jnp.zeros(st_ref.shape, F32)

    z = _dot(x_ref[:, 768:896], gup_ref[...]) + gb_ref[...]
    la_ref[...] = -_softplus(-z) * (1.0 / GLA_TAU)

    ltri = _ltri(CHUNK)
    qmasks = _lane_masks(N_HEADS * GLA_DK, GLA_DK)
    vmasks = _lane_masks(N_HEADS * GLA_DV, GLA_DV)
    bd = (_div(_iota((256, 128), 0), GLA_DV) == _div(_iota((256, 128), 1), GLA_DK)).astype(F32)
    hsum = _head_ones(N_HEADS * GLA_DV, GLA_DV)
    nw = nw_ref[...]

    n4 = N_HEADS * GLA_DV

    def prepare(cis):
        n = range(len(cis))
        rows = [pl.ds(pl.multiple_of(ci * CHUNK, CHUNK), CHUNK) for ci in cis]
        g = [_dot01_left(ltri, la_ref[r, :]) for r in rows]
        q = [x_ref[r, 0:128] * (GLA_DK ** -0.5) for r in rows]
        k = [x_ref[r, 128:256] for r in rows]
        v = [x_ref[r, 256:512] for r in rows]
        intra = [[] for _ in n]
        for s in range(CHUNK // SUB):
            lo, hi = s * SUB, (s + 1) * SUB
            sc = []
            for i in n:
                gs = jnp.zeros((1, 128), F32) if s == 0 else g[i][lo - 1:lo]
                qs = q[i][lo:hi] * jnp.exp(g[i][lo:hi] - gs)
                kt = k[i][:hi] * jnp.exp(jnp.minimum(gs - g[i][:hi], EXP_CLAMP))
                sc.append(_dot_nt(_stack(qs, qmasks), kt))
            causal = _iota(sc[0].shape, 1) <= lo + (_iota(sc[0].shape, 0) & (SUB - 1))
            p = [_dot(jnp.where(causal, sc[i], 0.0), v[i][:hi]) for i in n]
            for i in n:
                intra[i].append(sum(p[i][h * SUB:(h + 1) * SUB] * vmasks[h] for h in range(N_HEADS)))
        for i in n:
            g_last = g[i][CHUNK - 1:CHUNK]
            o_s[rows[i], :] = jnp.concatenate(intra[i], axis=0)
            qg_s[rows[i], :] = (q[i] * jnp.exp(g[i])).astype(BF16)
            kv_s[pl.ds(pl.multiple_of(cis[i] * n4, n4), n4), :] = bd * _dot_tn(v[i], k[i] * jnp.exp(g_last - g[i]))
            gt_s[pl.ds(pl.multiple_of(cis[i] * 8, 8), 8), :] = jnp.broadcast_to(jnp.exp(g_last), (8, 128))

    def advance(cis):
        for ci in cis:
            rows = pl.ds(pl.multiple_of(ci * CHUNK, CHUNK), CHUNK)
            st = st_ref[...]
            o_s[rows, :] += _dot_nt(qg_s[rows, :], st)
            st_ref[...] = (st * gt_s[pl.ds(pl.multiple_of(ci * 8, 8), 1), :]
                           + kv_s[pl.ds(pl.multiple_of(ci * n4, n4), n4), :])

    _for_chunks(tb // CHUNK, prepare, PREP_GROUP)
    _for_chunks(tb // CHUNK, advance, 2)
    o = o_s[...]
    ms = _dot01_right(o * o, hsum) * (1.0 / GLA_DV)
    y_ref[...] = (o * lax.rsqrt(ms + NORM_EPS) * nw * _silu(x_ref[:, 512:768])).astype(y_ref.dtype)


def _gla(x, gup, gb, nw):
    b, tp, _ = x.shape
    tb = _pick(tp, (640, 128, 64))
    nc = tb // CHUNK
    full = lambda a: pl.BlockSpec(a.shape, lambda i, j: (0, 0))
    return pl.pallas_call(
        _gla_kernel,
        grid=(b, tp // tb),
        in_specs=[pl.BlockSpec((None, tb, W_GLA), lambda i, j: (i, j, 0)), full(gup), full(gb), full(nw)],
        out_specs=pl.BlockSpec((None, tb, 256), lambda i, j: (i, j, 0)),
        out_shape=jax.ShapeDtypeStruct((b, tp, 256), BF16),
        scratch_shapes=[pltpu.VMEM((256, 128), F32), pltpu.VMEM((tb, 128), F32),
                        pltpu.VMEM((tb, 256), F32), pltpu.VMEM((tb, 128), BF16),
                        pltpu.VMEM((nc * N_HEADS * GLA_DV, 128), F32), pltpu.VMEM((nc * 8, 128), F32)],
        compiler_params=_cparams("parallel", "arbitrary"),
        name="gla",
    )(x, gup, gb, nw)


def _gdn_kernel(x_ref, cw_ref, alog_ref, dtb_ref, nw_ref, y_ref,
                s_ref, xp_ref, q_ref, k_ref, v_ref, beta_ref, gd_ref,
                u_s, w_s, attn_s, qd_s, ke_s, gt_s, o_s):
    tb = x_ref.shape[0]
    first = pl.program_id(1) == 0

    @pl.when(first)
    def _():
        s_ref[...] = jnp.zeros(s_ref.shape, F32)
        xp_ref[0:8, :] = jnp.zeros((8, 768), F32)

    @pl.when(jnp.logical_not(first))
    def _():
        xp_ref[0:8, :] = xp_ref[tb:tb + 8, :]

    xp_ref[8:tb + 8, :] = x_ref[:, 0:768]
    conv = sum(cw_ref[j:j + 1, :] * xp_ref[8 - (GDN_CONV - 1) + j:8 - (GDN_CONV - 1) + j + tb, :]
               for j in range(GDN_CONV))
    c = _silu(conv)
    hsum = _head_ones(256, GDN_DK)
    q = c[:, 0:256]
    k = c[:, 256:512]
    q_ref[...] = q * lax.rsqrt(_dot01_right(q * q, hsum) + L2_EPS) * (GDN_DK ** -0.5)
    k_ref[...] = k * lax.rsqrt(_dot01_right(k * k, hsum) + L2_EPS)
    v_ref[...] = c[:, 512:768]
    gates = x_ref[:, 1024:1152]
    beta_ref[...] = _sigmoid(gates)
    gd_ref[...] = -jnp.exp(alog_ref[...]) * _softplus(gates + dtb_ref[...])

    ltri = _ltri(CHUNK)
    masks = _lane_masks(256, GDN_DK)
    expand = lambda off: (_iota((128, 256), 0) == _div(_iota((128, 256), 1), GDN_DK) + off).astype(BF16)
    exp_beta, exp_g = expand(0), expand(N_HEADS)
    r = _iota((256, 256), 0)
    cidx = _iota((256, 256), 1)
    same = _div(r, CHUNK) == _div(cidx, CHUNK)
    incl = jnp.logical_and(same, r >= cidx)
    strict = jnp.logical_and(same, r > cidx)
    eye = (r == cidx).astype(F32)
    bd = same.astype(F32)
    nw = nw_ref[...]

    def prepare(cis):
        n = range(len(cis))
        rows = [pl.ds(pl.multiple_of(ci * CHUNK, CHUNK), CHUNK) for ci in cis]
        srows = [pl.ds(pl.multiple_of(ci * (N_HEADS * CHUNK), N_HEADS * CHUNK), N_HEADS * CHUNK) for ci in cis]
        q = [q_ref[r, :] for r in rows]
        k = [k_ref[r, :] for r in rows]
        v = [v_ref[r, :] for r in rows]
        bexp = [_dot01_right(beta_ref[r, :], exp_beta) for r in rows]
        gcum = [_dot01_left(ltri, gd_ref[r, :]) for r in rows]
        gexp = [_dot01_right(g, exp_g) for g in gcum]
        kst = [_stack(x, masks) for x in k]
        kb = [k[i] * bexp[i] for i in n]
        kk = [_dot_nt(_stack(kb[i], masks), kst[i]) for i in n]
        qk = [_dot_nt(_stack(q[i], masks), kst[i]) for i in n]
        dec = []
        for g in gexp:
            gcol = jnp.sum(_stack(g, masks), axis=1, keepdims=True) * (1.0 / GDN_DK)
            grow = jnp.sum(eye * gcol, axis=0, keepdims=True)
            dec.append(jnp.exp(jnp.minimum(gcol - grow, 0.0)))
        for i in n:
            attn_s[srows[i], :] = jnp.where(incl, qk[i] * dec[i], 0.0).astype(BF16)
        t = _neumann_inverse([-jnp.where(strict, kk[i] * dec[i], 0.0) for i in n], eye)
        u = [_dot(t[i], _stack(v[i] * bexp[i], masks)) for i in n]
        w = [_dot(t[i], _stack(kb[i] * jnp.exp(gexp[i]), masks)) for i in n]
        for i in n:
            u_s[rows[i], :] = _unstack(u[i], CHUNK)
            w_s[rows[i], :] = _unstack(w[i], CHUNK).astype(BF16)
            g_last = gexp[i][CHUNK - 1:CHUNK]
            qd_s[rows[i], :] = (q[i] * jnp.exp(gexp[i])).astype(BF16)
            ke_s[rows[i], :] = (k[i] * jnp.exp(g_last - gexp[i])).astype(BF16)
            gt_s[pl.ds(pl.multiple_of(cis[i] * 8, 8), 8), :] = jnp.broadcast_to(jnp.exp(g_last), (8, 256))

    def advance(cis):
        for ci in cis:
            advance_one(ci)

    def advance_one(ci):
        rows = pl.ds(pl.multiple_of(ci * CHUNK, CHUNK), CHUNK)
        srows = pl.ds(pl.multiple_of(ci * (N_HEADS * CHUNK), N_HEADS * CHUNK), N_HEADS * CHUNK)
        s = s_ref[...]
        sb = s.astype(BF16)
        v_new = u_s[rows, :] - jnp.dot(w_s[rows, :], sb, preferred_element_type=F32)
        o_s[rows, :] = (jnp.dot(qd_s[rows, :], sb, preferred_element_type=F32)
                        + _unstack(_dot(attn_s[srows, :], _stack(v_new, masks)), CHUNK))
        g_tot = gt_s[pl.ds(pl.multiple_of(ci * 8, 8), 1), :]
        s_ref[...] = s * g_tot + bd * _dot_tn(ke_s[rows, :], v_new)

    _for_chunks(tb // CHUNK, prepare, PREP_GROUP)
    _for_chunks(tb // CHUNK, advance, 2)
    o = o_s[...]
    ms = _dot01_right(o * o, hsum) * (1.0 / GDN_DK)
    y_ref[...] = (o * lax.rsqrt(ms + NORM_EPS) * nw * _silu(x_ref[:, 768:1024])).astype(y_ref.dtype)


def _gdn(x, cw, alog, dtb, nw):
    b, tp, _ = x.shape
    tb = _pick(tp, (640, 128, 64))
    nc = tb // CHUNK
    full = lambda a: pl.BlockSpec(a.shape, lambda i, j: (0, 0))
    return pl.pallas_call(
        _gdn_kernel,
        grid=(b, tp // tb),
        in_specs=[pl.BlockSpec((None, tb, W_GDN), lambda i, j: (i, j, 0)),
                  full(cw), full(alog), full(dtb), full(nw)],
        out_specs=pl.BlockSpec((None, tb, 256), lambda i, j: (i, j, 0)),
        out_shape=jax.ShapeDtypeStruct((b, tp, 256), BF16),
        scratch_shapes=[pltpu.VMEM((256, 256), F32), pltpu.VMEM((tb + 8, 768), F32),
                        pltpu.VMEM((tb, 256), F32), pltpu.VMEM((tb, 256), F32), pltpu.VMEM((tb, 256), F32),
                        pltpu.VMEM((tb, 128), F32), pltpu.VMEM((tb, 128), F32),
                        pltpu.VMEM((tb, 256), F32), pltpu.VMEM((tb, 256), BF16),
                        pltpu.VMEM((nc * N_HEADS * CHUNK, 256), BF16),
                        pltpu.VMEM((tb, 256), BF16), pltpu.VMEM((tb, 256), BF16),
                        pltpu.VMEM((nc * 8, 256), F32), pltpu.VMEM((tb, 256), F32)],
        compiler_params=_cparams("parallel", "arbitrary"),
        name="gdn",
    )(x, cw, alog, dtb, nw)


def _rwkv_kernel(x_ref, mu_ref, w2a2_ref, w0_ref, a0_ref, g2_ref, kk_ref, ka_ref, rk_ref,
                 lnw_ref, lnb_ref, y_ref,
                 s_ref, xp_ref, r_s, k_s, v_s, kk_s, b_s, lw_s, g_s,
                 at_s, z_s, arb_s, yv_s, rt_s, be_s, vk_s, gt_s, y_s):
    tb = x_ref.shape[0]
    first = pl.program_id(1) == 0

    @pl.when(first)
    def _():
        s_ref[...] = jnp.zeros(s_ref.shape, F32)
        xp_ref[0:8, :] = jnp.zeros((8, W_RWKV), F32)

    @pl.when(jnp.logical_not(first))
    def _():
        xp_ref[0:8, :] = xp_ref[tb:tb + 8, :]

    x = x_ref[...]
    xp_ref[8:tb + 8, :] = x
    z = x + (xp_ref[7:tb + 7, :] - x) * mu_ref[...]
    r = z[:, 0:256]
    k = z[:, 256:512]
    wa = z[:, 768:896]
    wa = jnp.where(_iota(wa.shape, 1) < 64, jnp.tanh(wa), wa)
    pre = _dot(wa, w2a2_ref[...])
    w_log = -_softplus(-(w0_ref[...] + pre[:, 0:256])) - 0.5
    a = _sigmoid(a0_ref[...] + pre[:, 256:512])
    hsum = _head_ones(256, RWKV_N)
    kkv = k * kk_ref[...]
    kkn = kkv * lax.rsqrt(_dot01_right(kkv * kkv, hsum) + L2_EPS)
    r_s[...] = r
    k_s[...] = k * (1.0 + (a - 1.0) * ka_ref[...])
    v_s[...] = z[:, 512:768]
    kk_s[...] = kkn
    b_s[...] = kkn * a
    lw_s[...] = -jnp.exp(w_log)
    g_s[...] = _dot(_sigmoid(z[:, 896:1152]), g2_ref[...])

    ltri = _ltri(CHUNK)
    masks = _lane_masks(256, RWKV_N)
    rr = _iota((256, 256), 0)
    cc = _iota((256, 256), 1)
    same = _div(rr, CHUNK) == _div(cc, CHUNK)
    incl = jnp.logical_and(same, rr >= cc)
    strict = jnp.logical_and(same, rr > cc)
    eye = (rr == cc).astype(F32)
    bd = same.astype(F32)
    rk = rk_ref[...]
    lnw = lnw_ref[...]
    lnb = lnb_ref[...]

    n4 = N_HEADS * CHUNK

    def prepare(cis):
        n = range(len(cis))
        rows = [pl.ds(pl.multiple_of(ci * CHUNK, CHUNK), CHUNK) for ci in cis]
        srows = [pl.ds(pl.multiple_of(ci * n4, n4), n4) for ci in cis]
        lw = [lw_s[r, :] for r in rows]
        gl = [_dot01_left(ltri, x) for x in lw]
        v = [v_s[r, :] for r in rows]
        k = [k_s[r, :] for r in rows]
        b = [b_s[r, :] for r in rows]
        e_neg = [jnp.exp(-g) for g in gl]
        a_st = [_stack(-kk_s[rows[i], :] * jnp.exp(gl[i] - lw[i]), masks) for i in n]
        r_t = [r_s[rows[i], :] * jnp.exp(gl[i]) for i in n]
        amat = []
        for i in n:
            lhs = jnp.concatenate([a_st[i], _stack(r_t[i], masks)], axis=0)
            rhs = jnp.concatenate([_stack(b[i] * e_neg[i], masks), _stack(k[i] * e_neg[i], masks)], axis=0)
            amat.append(_dot_nt(lhs, rhs))
        vst = [_stack(x, masks) for x in v]
        av = [_dot(jnp.where(strict, amat[i][0:n4, n4:], 0.0), vst[i]) for i in n]
        yv = [_dot(jnp.where(incl, amat[i][n4:, n4:], 0.0), vst[i]) for i in n]
        t = _neumann_inverse([jnp.where(strict, m[0:n4, 0:n4], 0.0) for m in amat], eye)
        z = [_dot(t[i], av[i]) for i in n]
        at = [_dot(t[i], a_st[i]) for i in n]
        for i in n:
            g_last = gl[i][CHUNK - 1:CHUNK]
            e_end = jnp.exp(g_last - gl[i])
            z_s[srows[i], :] = z[i]
            at_s[srows[i], :] = at[i].astype(BF16)
            arb_s[srows[i], :] = jnp.where(incl, amat[i][n4:, 0:n4], 0.0).astype(BF16)
            yv_s[rows[i], :] = _unstack(yv[i], CHUNK)
            rt_s[rows[i], :] = r_t[i].astype(BF16)
            be_s[rows[i], :] = (b[i] * e_end).astype(BF16)
            vk_s[srows[i], :] = bd * _dot_tn(v[i], k[i] * e_end)
            gt_s[pl.ds(pl.multiple_of(cis[i] * 8, 8), 8), :] = jnp.broadcast_to(jnp.exp(g_last), (8, 256))

    def advance(cis):
        for ci in cis:
            rows = pl.ds(pl.multiple_of(ci * CHUNK, CHUNK), CHUNK)
            srows = pl.ds(pl.multiple_of(ci * n4, n4), n4)
            s = s_ref[...]
            sb = s.astype(BF16)
            u_st = _dot_nt(at_s[srows, :], sb) + z_s[srows, :]
            y_s[rows, :] = (_dot_nt(rt_s[rows, :], sb) + _unstack(_dot(arb_s[srows, :], u_st), CHUNK)
                            + yv_s[rows, :])
            g_tot = gt_s[pl.ds(pl.multiple_of(ci * 8, 8), 1), :]
            s_ref[...] = s * g_tot + bd * _dot_tn(_unstack(u_st, CHUNK), be_s[rows, :]) + vk_s[srows, :]

    _for_chunks(tb // CHUNK, prepare, PREP_GROUP)
    _for_chunks(tb // CHUNK, advance, 2)
    y = y_s[...]
    v = v_s[...]
    mean = _dot01_right(y, hsum) * (1.0 / RWKV_N)
    d = y - mean
    var = _dot01_right(d * d, hsum) * (1.0 / RWKV_N)
    yn = d * lax.rsqrt(var + RWKV_LN_EPS) * lnw + lnb
    bonus = _dot01_right(r_s[...] * k_s[...] * rk, hsum) * v
    y_ref[...] = ((yn + bonus) * g_s[...]).astype(y_ref.dtype)


def _rwkv(x, mu, w2a2, w0, a0, g2, kk, ka, rk, lnw, lnb):
    b, tp, _ = x.shape
    tb = _pick(tp, (640, 128, 64))
    nc = tb // CHUNK
    full = lambda a: pl.BlockSpec(a.shape, lambda i, j: (0, 0))
    small = (mu, w2a2, w0, a0, g2, kk, ka, rk, lnw, lnb)
    stacked = lambda dt: pltpu.VMEM((nc * N_HEADS * CHUNK, 256), dt)
    return pl.pallas_call(
        _rwkv_kernel,
        grid=(b, tp // tb),
        in_specs=[pl.BlockSpec((None, tb, W_RWKV), lambda i, j: (i, j, 0))] + [full(a) for a in small],
        out_specs=pl.BlockSpec((None, tb, 256), lambda i, j: (i, j, 0)),
        out_shape=jax.ShapeDtypeStruct((b, tp, 256), BF16),
        scratch_shapes=[pltpu.VMEM((256, 256), F32), pltpu.VMEM((tb + 8, W_RWKV), F32)]
        + [pltpu.VMEM((tb, 256), F32) for _ in range(7)]
        + [stacked(BF16), stacked(F32), stacked(BF16), pltpu.VMEM((tb, 256), F32),
           pltpu.VMEM((tb, 256), BF16), pltpu.VMEM((tb, 256), BF16), stacked(F32),
           pltpu.VMEM((nc * 8, 256), F32), pltpu.VMEM((tb, 256), F32)],
        compiler_params=_cparams("parallel", "arbitrary"),
        name="rwkv",
    )(x, *small)


def _mla_prep_kernel(x_ref, qnw_ref, kvnw_ref, wqa_ref, wqb_ref, wk_ref, wv_ref, ea_ref, eb_ref,
                     c1_ref, s1_ref, q_ref, k_ref, v_ref):
    x = x_ref[...]
    qn = _rms(x[:, 0:256], qnw_ref[...]).astype(BF16)
    kvn = _rms(x[:, 256:384], kvnw_ref[...]).astype(BF16)
    kpe = x[:, 384:512].astype(BF16)
    c1 = c1_ref[...]
    s1 = s1_ref[...]
    qa = jnp.dot(qn, wqa_ref[...], preferred_element_type=F32)
    qb = jnp.dot(qn, wqb_ref[...], preferred_element_type=F32)
    kn = jnp.dot(kvn, wk_ref[...], preferred_element_type=F32)
    kp = (jnp.dot(kpe, ea_ref[...], preferred_element_type=F32) * c1
          + jnp.dot(kpe, eb_ref[...], preferred_element_type=F32) * s1)
    ones_lane = ((_iota((1, N_HEADS * MLA_SLOT), 1) & (MLA_SLOT - 1)) == MLA_V).astype(F32)
    v_ref[...] = (jnp.dot(kvn, wv_ref[...], preferred_element_type=F32) + ones_lane).astype(BF16)
    for h in range(N_HEADS):
        sl = slice(h * MLA_SLOT, (h + 1) * MLA_SLOT)
        q_ref[:, sl] = (qa[:, sl] * c1 + qb[:, sl] * s1).astype(BF16)
        k_ref[:, sl] = (kn[:, sl] + kp).astype(BF16)


def _mla_prep(x, qnw, kvnw, wqa, wqb, wk, wv, ea, eb, c1, s1):
    b, tp, _ = x.shape
    tm = _pick(tp, (640, 128, 64))
    full = lambda a: pl.BlockSpec(a.shape, lambda i, j: (0, 0))
    wide = N_HEADS * MLA_SLOT
    out = pl.BlockSpec((None, tm, wide), lambda i, j: (i, j, 0))
    tab = pl.BlockSpec((tm, MLA_SLOT), lambda i, j: (j, 0))
    return pl.pallas_call(
        _mla_prep_kernel,
        grid=(b, tp // tm),
        in_specs=[pl.BlockSpec((None, tm, W_MLA), lambda i, j: (i, j, 0))]
        + [full(a) for a in (qnw, kvnw, wqa, wqb, wk, wv, ea, eb)] + [tab, tab],
        out_specs=[out, out, out],
        out_shape=[jax.ShapeDtypeStruct((b, tp, wide), BF16)] * 3,
        compiler_params=_cparams("parallel", "parallel"),
        name="mla_prep",
    )(x, qnw, kvnw, wqa, wqb, wk, wv, ea, eb, c1, s1)


FLASH_HEADS = 4


def _flash_kernel(q_ref, k_ref, v_ref, o_ref, m_ref, acc_ref):
    qi = pl.program_id(2)
    t = q_ref.shape[0]
    m_ref[...] = jnp.full(m_ref.shape, NEG_INF, F32)
    acc_ref[...] = jnp.zeros(acc_ref.shape, F32)

    def block(j, diagonal):
        rows = pl.ds(pl.multiple_of(j * t, t), t)
        for h in range(FLASH_HEADS):
            sl = slice(h * MLA_SLOT, (h + 1) * MLA_SLOT)
            s = lax.dot_general(q_ref[:, sl], k_ref[rows, sl], (((1,), (1,)), ((), ())),
                                preferred_element_type=F32)
            if diagonal:
                s = jnp.where(_iota(s.shape, 0) >= _iota(s.shape, 1), s, NEG_INF)
            m_old = m_ref[h]
            m_new = jnp.maximum(m_old, jnp.max(s, axis=-1, keepdims=True))
            p = jnp.exp2(s - m_new).astype(BF16)
            acc_ref[h] = (jnp.exp2(m_old - m_new) * acc_ref[h]
                          + jnp.dot(p, v_ref[rows, sl], preferred_element_type=F32))
            m_ref[h] = m_new

    def full_block(j, carry):
        block(j, False)
        return carry

    lax.fori_loop(0, qi, full_block, 0)
    block(qi, True)
    for h in range(FLASH_HEADS):
        acc = acc_ref[h]
        o_ref[:, h * MLA_SLOT:(h + 1) * MLA_SLOT] = (acc / acc[:, MLA_V:MLA_V + 1]).astype(o_ref.dtype)


def _flash(q, k, v):
    b, tp, wide = q.shape
    t = _pick(tp, (640, 128, 64))
    w = FLASH_HEADS * MLA_SLOT
    qspec = pl.BlockSpec((None, t, w), lambda i, h, qi: (i, qi, h))
    kspec = pl.BlockSpec((None, tp, w), lambda i, h, qi: (i, 0, h))
    return pl.pallas_call(
        _flash_kernel,
        grid=(b, wide // w, tp // t),
        in_specs=[qspec, kspec, kspec],
        out_specs=qspec,
        out_shape=jax.ShapeDtypeStruct((b, tp, wide), BF16),
        scratch_shapes=[pltpu.VMEM((FLASH_HEADS, t, 1), F32), pltpu.VMEM((FLASH_HEADS, t, MLA_SLOT), F32)],
        compiler_params=_cparams("parallel", "parallel", "arbitrary"),
        name="mla_flash",
    )(q, k, v)


def _router_kernel(h_ref, nw_ref, wr_ref, xn_ref, info_ref):
    xn = _rms(h_ref[...], nw_ref[...])
    xn_ref[...] = xn.astype(BF16)
    logits = jnp.dot(xn, wr_ref[...], preferred_element_type=F32, precision=lax.Precision.HIGHEST)
    lane = _iota(logits.shape, 1).astype(F32)
    valid = lane < N_EXPERTS
    l0 = jnp.where(valid, logits, NEG_INF)
    m1 = jnp.max(l0, axis=-1, keepdims=True)
    i1 = jnp.min(jnp.where(l0 == m1, lane, float(LANE)), axis=-1, keepdims=True)
    l1 = jnp.where(lane == i1, NEG_INF, l0)
    m2 = jnp.max(l1, axis=-1, keepdims=True)
    i2 = jnp.min(jnp.where(l1 == m2, lane, float(LANE)), axis=-1, keepdims=True)
    e2 = jnp.exp(m2 - m1)
    g1 = 1.0 / (1.0 + e2)
    g2 = e2 / (1.0 + e2)
    info = jnp.where(lane == 0, i1, 0.0)
    info = jnp.where(lane == 1, i2, info)
    info = jnp.where(lane == 2, g1, info)
    info = jnp.where(lane == 3, g2, info)
    info_ref[...] = info


def _router(h, nw, wr):
    n = h.shape[0]
    tm = _pick(n, (512, 256, 128, 64))
    return pl.pallas_call(
        _router_kernel,
        grid=(n // tm,),
        in_specs=[pl.BlockSpec((tm, D_MODEL), lambda i: (i, 0)), pl.BlockSpec(nw.shape, lambda i: (0, 0)),
                  pl.BlockSpec(wr.shape, lambda i: (0, 0))],
        out_specs=[pl.BlockSpec((tm, D_MODEL), lambda i: (i, 0)), pl.BlockSpec((tm, LANE), lambda i: (i, 0))],
        out_shape=[jax.ShapeDtypeStruct((n, D_MODEL), BF16), jax.ShapeDtypeStruct((n, LANE), F32)],
        compiler_params=_cparams("parallel"),
        name="moe_router",
    )(h, nw, wr)


def _expert_kernel(be_ref, x_ref, gate_ref, wg_ref, wu_ref, wd_ref, out_ref, acc_ref):
    f = pl.program_id(1)

    @pl.when(f == 0)
    def _():
        acc_ref[...] = jnp.zeros(acc_ref.shape, F32)

    xb = x_ref[...]
    a = jnp.dot(xb, wg_ref[...], preferred_element_type=F32)
    b = jnp.dot(xb, wu_ref[...], preferred_element_type=F32)
    acc_ref[...] += jnp.dot((_silu(a) * b).astype(BF16), wd_ref[...], preferred_element_type=F32)

    @pl.when(f == pl.num_programs(1) - 1)
    def _():
        out_ref[...] = acc_ref[...] * gate_ref[...]


def _experts(block_expert, x_rows, row_gate, wg, wu, wd, tm):
    cap = x_rows.shape[0]
    dff = wg.shape[2]
    tf = _pick(dff, (1792, 512, 256, 128))
    grid_spec = pltpu.PrefetchScalarGridSpec(
        num_scalar_prefetch=1,
        grid=(cap // tm, dff // tf),
        in_specs=[pl.BlockSpec((tm, D_MODEL), lambda i, f, be: (i, 0)),
                  pl.BlockSpec((tm, 1), lambda i, f, be: (i, 0)),
                  pl.BlockSpec((None, D_MODEL, tf), lambda i, f, be: (be[i], 0, f)),
                  pl.BlockSpec((None, D_MODEL, tf), lambda i, f, be: (be[i], 0, f)),
                  pl.BlockSpec((None, tf, D_MODEL), lambda i, f, be: (be[i], f, 0))],
        out_specs=pl.BlockSpec((tm, D_MODEL), lambda i, f, be: (i, 0)),
        scratch_shapes=[pltpu.VMEM((tm, D_MODEL), F32)],
    )
    return pl.pallas_call(
        _expert_kernel,
        grid_spec=grid_spec,
        out_shape=jax.ShapeDtypeStruct((cap, D_MODEL), F32),
        compiler_params=_cparams("parallel", "arbitrary"),
        name="moe_experts",
    )(block_expert, x_rows, row_gate, wg, wu, wd)


def _combine_kernel(h_ref, ya_ref, yb_ref, out_ref):
    out_ref[...] = h_ref[...] + (ya_ref[...] + yb_ref[...])


def _combine(h, ya, yb):
    n = h.shape[0]
    tm = _pick(n, (1024, 512, 256, 128, 64))
    spec = pl.BlockSpec((tm, D_MODEL), lambda i: (i, 0))
    return pl.pallas_call(
        _combine_kernel, grid=(n // tm,), in_specs=[spec, spec, spec], out_specs=spec,
        out_shape=jax.ShapeDtypeStruct((n, D_MODEL), F32),
        compiler_params=_cparams("parallel"), name="moe_combine",
    )(h, ya, yb)


def _moe(h, nw, router, wg, wu, wd):
    n = h.shape[0]
    tm = _pick(n, (512, 64))
    wr = jnp.pad(router.astype(F32), ((0, 0), (0, LANE - N_EXPERTS)))
    xn, info = _router(h, nw, wr)
    expert = info[:, 0:2].astype(jnp.int32).reshape(-1)
    gate = info[:, 2:4].reshape(-1)
    n_assign = 2 * n
    order = jnp.argsort(expert)
    onehot = (expert[:, None] == jnp.arange(N_EXPERTS, dtype=jnp.int32)[None, :]).astype(jnp.int32)
    running = jnp.cumsum(onehot, axis=0)
    counts = running[-1]
    padded = (counts + tm - 1) // tm * tm
    pad_end = jnp.cumsum(padded)
    pad_start = pad_end - padded
    start = jnp.cumsum(counts) - counts
    n_blocks = -(-n_assign // tm) + N_EXPERTS
    cap = n_blocks * tm
    block_start = jnp.arange(n_blocks, dtype=jnp.int32) * tm
    block_expert = jnp.minimum(jnp.sum(block_start[:, None] >= pad_end[None, :], axis=1), N_EXPERTS - 1)
    block_expert = block_expert.astype(jnp.int32)
    rank = (block_start - pad_start[block_expert])[:, None] + jnp.arange(tm, dtype=jnp.int32)[None, :]
    valid = (rank < counts[block_expert][:, None]).reshape(cap)
    src = order[jnp.clip(start[block_expert][:, None] + rank, 0, n_assign - 1).reshape(cap)]
    row_token = jnp.where(valid, src // 2, 0)
    row_gate = jnp.where(valid, gate[src], 0.0)
    dest = jnp.sum(onehot * (pad_start[None, :] + running - 1), axis=1)
    x_rows = jnp.take(xn, row_token, axis=0)
    y_rows = _experts(block_expert, x_rows, row_gate[:, None], wg, wu, wd, tm)
    dest = dest.reshape(n, 2)
    return _combine(h, jnp.take(y_rows, dest[:, 0], axis=0), jnp.take(y_rows, dest[:, 1], axis=0))


def _pad_cols(a, width):
    return jnp.pad(a, ((0, 0), (0, width - a.shape[1])))


def _row(a, width=None):
    a = a.reshape(1, -1).astype(F32)
    return a if width is None else _pad_cols(a, width)


def _rope_tables(tp):
    pos = jnp.arange(tp, dtype=F32)
    inv_freq = ROPE_THETA ** (-jnp.arange(0, MLA_ROPE, 2, dtype=F32) / MLA_ROPE)
    ang = pos[:, None] * inv_freq[None, :]
    cos, sin = jnp.cos(ang), jnp.sin(ang)
    ones = jnp.ones((tp, MLA_NOPE), F32)
    zeros = jnp.zeros((tp, MLA_SLOT - MLA_NOPE - MLA_ROPE), F32)
    c1 = jnp.concatenate([ones, cos, cos, zeros], axis=1)
    s1 = jnp.concatenate([0.0 * ones, -sin, sin, zeros], axis=1)
    return c1, s1


def _mla_weights(w_uq, w_ukv):
    half = MLA_ROPE // 2
    scale = (MLA_NOPE + MLA_ROPE) ** -0.5 * math.log2(math.e)
    zq =jnp.zeros((w_uq.shape[0], MLA_SLOT - MLA_NOPE - MLA_ROPE), F32)
    zn = jnp.zeros((w_uq.shape[0], MLA_NOPE), F32)
    zk = jnp.zeros((w_ukv.shape[0], MLA_SLOT - MLA_NOPE), F32)
    wqa, wqb, wk, wv = [], [], [], []
    for h in range(N_HEADS):
        q = w_uq[:, h * 96:(h + 1) * 96] * scale
        nope, x1, x2 = q[:, :MLA_NOPE], q[:, MLA_NOPE:MLA_NOPE + half], q[:, MLA_NOPE + half:]
        wqa += [nope, x1, x2, zq]
        wqb += [zn, x2, x1, zq]
        kv = w_ukv[:, h * 128:(h + 1) * 128]
        wk += [kv[:, :MLA_NOPE], zk]
        wv += [kv[:, MLA_NOPE:], zk]
    cat = lambda parts: jnp.concatenate(parts, axis=1).astype(BF16)
    ea = np.zeros((MLA_SLOT, MLA_SLOT), np.float32)
    eb = np.zeros((MLA_SLOT, MLA_SLOT), np.float32)
    for i in range(MLA_ROPE):
        ea[i, MLA_NOPE + i] = 1.0
        eb[(i + half) % MLA_ROPE, MLA_NOPE + i] = 1.0
    return cat(wqa), cat(wqb), cat(wk), cat(wv), jnp.asarray(ea, BF16), jnp.asarray(eb, BF16)


def _branch_weights(w_branch):
    wb = w_branch.astype(BF16)
    z = jnp.zeros((MLA_SLOT - MLA_V, D_MODEL), BF16)
    parts = []
    for h in range(N_HEADS):
        parts += [wb[1, h * MLA_V:(h + 1) * MLA_V], z]
    return wb[0], jnp.concatenate(parts, axis=0), wb[2], wb[3]


def _token_mixing(h, b, tp, p, l, tables):
    w_in = p["w_in"][l]
    col = lambda i, j: w_in[:, _OFF[i]:_OFF[j]]
    wg = jnp.concatenate([col(0, 3), col(4, 5), _pad_cols(col(3, 4), LANE)], axis=1).astype(BF16)
    wm = jnp.concatenate([col(5, 7), _pad_cols(col(7, 8), LANE)], axis=1).astype(BF16)
    wd = jnp.concatenate([col(8, 10), _pad_cols(col(10, 12), LANE)], axis=1).astype(BF16)
    wr = _pad_cols(col(12, 13), W_RWKV).astype(BF16)
    nw = _row(p["norm_mix"][l])
    xg, xm, xd, xr = _inproj(h, nw, wg, wm, wd, wr)
    shape3 = lambda a: a.reshape(b, tp, a.shape[1])

    gup = jnp.pad(p["gla_gate_up"][l], ((0, LANE - 16), (0, 0))).astype(BF16)
    y_gla = _gla(shape3(xg), gup, _row(p["gla_gate_bias"][l]), _row(jnp.tile(p["gla_norm"][l], N_HEADS)))

    wqa, wqb, wk, wv, ea, eb = _mla_weights(p["mla_w_uq"][l], p["mla_w_ukv"][l])
    q, k, v = _mla_prep(shape3(xm), _row(p["mla_q_norm"][l]), _row(p["mla_kv_norm"][l]),
                        wqa, wqb, wk, wv, ea, eb, *tables)
    y_mla = _flash(q, k, v)

    lanes4 = lambda a: jnp.pad(a.reshape(1, N_HEADS).astype(F32), ((0, 0), (N_HEADS, LANE - 2 * N_HEADS)))
    y_gdn = _gdn(shape3(xd), p["gdn_conv"][l].astype(F32), lanes4(p["gdn_a_log"][l]),
                 lanes4(p["gdn_dt_bias"][l]), _row(jnp.tile(p["gdn_norm"][l], N_HEADS)))

    w2a2 = jnp.zeros((LANE, 512), F32)
    w2a2 = w2a2.at[0:64, 0:256].set(p["rwkv_w2"][l]).at[64:128, 256:512].set(p["rwkv_a2"][l]).astype(BF16)
    g2 = jnp.pad(p["rwkv_g2"][l], ((0, 256 - 160), (0, 0))).astype(BF16)
    y_rwkv = _rwkv(shape3(xr), _row(p["rwkv_mu"][l], W_RWKV), w2a2, _row(p["rwkv_w0"][l]),
                   _row(p["rwkv_a0"][l]), g2, _row(p["rwkv_k_k"][l]), _row(p["rwkv_k_a"][l]),
                   _row(p["rwkv_r_k"][l]), _row(p["rwkv_ln_w"][l]), _row(p["rwkv_ln_b"][l]))

    flat = lambda a: a.reshape(b * tp, a.shape[2])
    ys = [flat(y_gla), flat(y_mla), flat(y_gdn), flat(y_rwkv)]
    return _merge(h, nw, col(13, 14).astype(BF16), ys, _branch_weights(p["w_branch"][l]),
                  p["w_out"][l].astype(BF16))


def kernel(x, meta_tokens, norm_mix, w_in, gla_gate_up, gla_gate_bias, gla_norm, mla_q_norm, mla_w_uq, mla_kv_norm, mla_w_ukv, gdn_conv, gdn_a_log, gdn_dt_bias, gdn_norm, rwkv_mu, rwkv_w0, rwkv_w2, rwkv_a0, rwkv_a2, rwkv_g2, rwkv_k_k, rwkv_k_a, rwkv_r_k, rwkv_ln_w, rwkv_ln_b, w_branch, w_out, norm_ffn, ffn_w_gate, ffn_w_up, ffn_w_down, moe_router, moe_w_gate, moe_w_up, moe_w_down, norm_final):
    p = dict(norm_mix=norm_mix, w_in=w_in, gla_gate_up=gla_gate_up, gla_gate_bias=gla_gate_bias,
             gla_norm=gla_norm, mla_q_norm=mla_q_norm, mla_w_uq=mla_w_uq, mla_kv_norm=mla_kv_norm,
             mla_w_ukv=mla_w_ukv, gdn_conv=gdn_conv, gdn_a_log=gdn_a_log, gdn_dt_bias=gdn_dt_bias,
             gdn_norm=gdn_norm, rwkv_mu=rwkv_mu, rwkv_w0=rwkv_w0, rwkv_w2=rwkv_w2, rwkv_a0=rwkv_a0,
             rwkv_a2=rwkv_a2, rwkv_g2=rwkv_g2, rwkv_k_k=rwkv_k_k, rwkv_k_a=rwkv_k_a, rwkv_r_k=rwkv_r_k,
             rwkv_ln_w=rwkv_ln_w, rwkv_ln_b=rwkv_ln_b, w_branch=w_branch, w_out=w_out)
    b, seq, d = x.shape
    t_real = N_META + seq
    tp = -(-t_real // 128) * 128
    meta = jnp.broadcast_to(meta_tokens[None].astype(x.dtype), (b, N_META, d))
    h = jnp.concatenate([meta, x, jnp.zeros((b, tp - t_real, d), x.dtype)], axis=1).reshape(b * tp, d)
    tables = _rope_tables(tp)
    depth = norm_mix.shape[0]
    for l in range(depth):
        h = _token_mixing(h, b, tp, p, l, tables)
        nw = _row(norm_ffn[l])
        if l % 2 == 0:
            h = _ffn(h, nw, ffn_w_gate[l // 2].astype(BF16), ffn_w_up[l // 2].astype(BF16),
                     ffn_w_down[l // 2].astype(BF16))
        else:
            h = _moe(h, nw, moe_router[l // 2], moe_w_gate[l // 2].astype(BF16),
                     moe_w_up[l // 2].astype(BF16), moe_w_down[l // 2].astype(BF16))
    out = _final_norm(h, _row(norm_final))
    return out.reshape(b, tp, d)[:, N_META:t_real]
```

```python
import functools
import math

import jax
import jax.numpy as jnp
import numpy as np
from jax import lax
from jax.experimental import pallas as pl
from jax.experimental.pallas import tpu as pltpu
from jax.experimental.pallas import tpu_sc as plsc

F32 = jnp.float32
BF16 = jnp.bfloat16

D_MODEL = 1024
N_META = 16
N_HEADS = 4
GLA_DK = 32
GLA_DV = 64
GLA_TAU = 16.0
MLA_NOPE = 64
MLA_ROPE = 32
MLA_V = 64
MLA_SLOT = 128
ROPE_THETA = 10000.0
GDN_DK = 64
GDN_CONV = 4
RWKV_N = 64
RWKV_LN_EPS = RWKV_N * 1e-5
CHUNK = 64
SUB = 16
PREP_GROUP = 5
N_EXPERTS = 8
SC_CHUNK = 64
NORM_EPS = 1e-6
L2_EPS = 1e-6
NEG_INF = -1e30
EXP_CLAMP = 80.0

LANE = 128
VMEM_LIMIT = 56 * 1024 * 1024

_OFF = np.cumsum([0, 128, 128, 256, 16, 256, 256, 128, 32, 768, 256, 4, 4, 1056, 4096]).tolist()
W_GLA, W_MLA, W_GDN, W_RWKV = 896, 512, 1152, 1152


def _cparams(*sem):
    return pltpu.CompilerParams(dimension_semantics=sem, vmem_limit_bytes=VMEM_LIMIT)


def _pick(n, prefs):
    for p in prefs:
        if n % p == 0:
            return p
    raise ValueError(f"no tile for {n}")


def _dot(a, b):
    return jnp.dot(a.astype(BF16), b.astype(BF16), preferred_element_type=F32)


def _dot_nt(a, b):
    return lax.dot_general(a.astype(BF16), b.astype(BF16), (((1,), (1,)), ((), ())),
                           preferred_element_type=F32)


def _dot_tn(a, b):
    return lax.dot_general(a.astype(BF16), b.astype(BF16), (((0,), (0,)), ((), ())),
                           preferred_element_type=F32)


def _split3(x):
    hi = x.astype(BF16)
    r1 = x - hi.astype(F32)
    mid = r1.astype(BF16)
    lo = (r1 - mid.astype(F32)).astype(BF16)
    return hi, mid, lo


def _dot01_left(m01, x):
    return sum(jnp.dot(m01, p, preferred_element_type=F32) for p in _split3(x))


def _dot01_right(x, m01):
    return sum(jnp.dot(p, m01, preferred_element_type=F32) for p in _split3(x))


def _iota(shape, dim):
    return lax.broadcasted_iota(jnp.int32, shape, dim)


def _div(x, w):
    return x >> int(math.log2(w))


def _ltri(n):
    return (_iota((n, n), 0) >= _iota((n, n), 1)).astype(BF16)


def _head_ones(n, w):
    return (_div(_iota((n, n), 0), w) == _div(_iota((n, n), 1), w)).astype(BF16)


def _lane_masks(width, w):
    lane = _div(_iota((1, width), 1), w)
    return [(lane == h).astype(F32) for h in range(width // w)]


def _stack(x, masks):
    return jnp.concatenate([x * m for m in masks], axis=0)


def _unstack(y, n):
    out = y[0:n]
    for h in range(1, y.shape[0] // n):
        out = out + y[h * n:(h + 1) * n]
    return out


def _rms(x, w):
    return x * lax.rsqrt(jnp.mean(x * x, axis=-1, keepdims=True) + NORM_EPS) * w


def _sigmoid(x):
    return 1.0 / (1.0 + jnp.exp(-x))


def _silu(x):
    return x * _sigmoid(x)


def _softplus(x):
    return jnp.maximum(x, 0.0) + jnp.log(1.0 + jnp.exp(-jnp.abs(x)))


def _neumann_inverse(xs, eye):
    ts = [eye + x for x in xs]
    ps = list(xs)
    for _ in range(int(math.log2(CHUNK)) - 1):
        ps = [_dot(p, p) for p in ps]
        ts = [t + _dot(t, p) for t, p in zip(ts, ps)]
    return ts


def _for_chunks(n, body, group):
    def trip(i, carry):
        body([i * group + g for g in range(group)])
        return carry

    if n >= group:
        lax.fori_loop(0, n // group, trip, 0)
    if n % group:
        body(list(range(n - n % group, n)))


def _inproj_kernel(h_ref, nw_ref, wg_ref, wm_ref, wd_ref, wr_ref, og_ref, om_ref, od_ref, or_ref):
    xb = _rms(h_ref[...], nw_ref[...]).astype(BF16)
    og_ref[...] = jnp.dot(xb, wg_ref[...], preferred_element_type=F32)
    om_ref[...] = jnp.dot(xb, wm_ref[...], preferred_element_type=F32)
    od_ref[...] = jnp.dot(xb, wd_ref[...], preferred_element_type=F32)
    or_ref[...] = jnp.dot(xb, wr_ref[...], preferred_element_type=F32)


def _inproj(h, nw, wg, wm, wd, wr):
    n = h.shape[0]
    tm = _pick(n, (512, 256, 128, 64))
    full = lambda a: pl.BlockSpec(a.shape, lambda i: (0, 0))
    row = lambda w: pl.BlockSpec((tm, w), lambda i: (i, 0))
    return pl.pallas_call(
        _inproj_kernel,
        grid=(n // tm,),
        in_specs=[row(D_MODEL), full(nw), full(wg), full(wm), full(wd), full(wr)],
        out_specs=[row(W_GLA), row(W_MLA), row(W_GDN), row(W_RWKV)],
        out_shape=[jax.ShapeDtypeStruct((n, w), F32) for w in (W_GLA, W_MLA, W_GDN, W_RWKV)],
        compiler_params=_cparams("parallel"),
        name="inproj",
    )(h, nw, wg, wm, wd, wr)


def _merge_kernel(h_ref, nw_ref, wgate_ref, yg_ref, ym_ref, yd_ref, yr_ref,
                  wbg_ref, wbm_ref, wbd_ref, wbr_ref, wout_ref, out_ref):
    x = h_ref[...]
    xb = _rms(x, nw_ref[...]).astype(BF16)
    acc = jnp.zeros(x.shape, F32)
    branches = ((yg_ref, wbg_ref), (ym_ref, wbm_ref), (yd_ref, wbd_ref), (yr_ref, wbr_ref))
    for i, (y_ref, wb_ref) in enumerate(branches):
        logits = jnp.dot(xb, wgate_ref[:, i * D_MODEL:(i + 1) * D_MODEL], preferred_element_type=F32)
        proj = jnp.dot(y_ref[...], wb_ref[...], preferred_element_type=F32)
        acc = acc + _sigmoid(logits) * proj
    out_ref[...] = x + jnp.dot(acc.astype(BF16), wout_ref[...], preferred_element_type=F32)


def _merge(h, nw, wgate, ys, wbs, wout):
    n = h.shape[0]
    tm = _pick(n, (512, 256, 128, 64))
    full = lambda a: pl.BlockSpec(a.shape, lambda i: (0, 0))
    row = lambda w: pl.BlockSpec((tm, w), lambda i: (i, 0))
    return pl.pallas_call(
        _merge_kernel,
        grid=(n // tm,),
        in_specs=[row(D_MODEL), full(nw), full(wgate)] + [row(y.shape[1]) for y in ys]
        + [full(w) for w in wbs] + [full(wout)],
        out_specs=row(D_MODEL),
        out_shape=jax.ShapeDtypeStruct((n, D_MODEL), F32),
        compiler_params=_cparams("parallel"),
        name="merge",
    )(h, nw, wgate, *ys, *wbs, wout)


def _ffn_kernel(h_ref, nw_ref, wg_ref, wu_ref, wd_ref, out_ref, xb_ref, acc_ref):
    f = pl.program_id(1)

    @pl.when(f == 0)
    def _():
        xb_ref[...] = _rms(h_ref[...], nw_ref[...]).astype(BF16)
        acc_ref[...] = jnp.zeros(acc_ref.shape, F32)

    xb = xb_ref[...]
    a = jnp.dot(xb, wg_ref[...], preferred_element_type=F32)
    b = jnp.dot(xb, wu_ref[...], preferred_element_type=F32)
    acc_ref[...] += jnp.dot((_silu(a) * b).astype(BF16), wd_ref[...], preferred_element_type=F32)

    @pl.when(f == pl.num_programs(1) - 1)
    def _():
        out_ref[...] = h_ref[...] + acc_ref[...]


def _ffn(h, nw, wg, wu, wd):
    n = h.shape[0]
    dff = wg.shape[1]
    tm = _pick(n, (512, 256, 128, 64))
    tf = _pick(dff, (1408, 512, 256, 128))
    return pl.pallas_call(
        _ffn_kernel,
        grid=(n // tm, dff // tf),
        in_specs=[pl.BlockSpec((tm, D_MODEL), lambda i, f: (i, 0)),
                  pl.BlockSpec(nw.shape, lambda i, f: (0, 0)),
                  pl.BlockSpec((D_MODEL, tf), lambda i, f: (0, f)),
                  pl.BlockSpec((D_MODEL, tf), lambda i, f: (0, f)),
                  pl.BlockSpec((tf, D_MODEL), lambda i, f: (f, 0))],
        out_specs=pl.BlockSpec((tm, D_MODEL), lambda i, f: (i, 0)),
        out_shape=jax.ShapeDtypeStruct((n, D_MODEL), F32),
        scratch_shapes=[pltpu.VMEM((tm, D_MODEL), BF16), pltpu.VMEM((tm, D_MODEL), F32)],
        compiler_params=_cparams("parallel", "arbitrary"),
        name="ffn",
    )(h, nw, wg, wu, wd)


def _final_norm_kernel(h_ref, nw_ref, out_ref):
    out_ref[...] = _rms(h_ref[...], nw_ref[...])


def _final_norm(h, nw):
    n = h.shape[0]
    tm = _pick(n, (1024, 512, 256, 128, 64))
    return pl.pallas_call(
        _final_norm_kernel,
        grid=(n // tm,),
        in_specs=[pl.BlockSpec((tm, D_MODEL), lambda i: (i, 0)), pl.BlockSpec(nw.shape, lambda i: (0, 0))],
        out_specs=pl.BlockSpec((tm, D_MODEL), lambda i: (i, 0)),
        out_shape=jax.ShapeDtypeStruct((n, D_MODEL), F32),
        compiler_params=_cparams("parallel"),
        name="final_norm",
    )(h, nw)


def _gla_kernel(x_ref, gup_ref, gb_ref, nw_ref, y_ref, st_ref, la_ref, o_s, qg_s, kv_s, gt_s):
    tb = x_ref.shape[0]

    @pl.when(pl.program_id(1) == 0)
    def _():
        st_ref[...] = jnp.zeros(st_ref.shape, F32)

    z = _dot(x_ref[:, 768:896], gup_ref[...]) + gb_ref[...]
    la_ref[...] = -_softplus(-z) * (1.0 / GLA_TAU)

    ltri = _ltri(CHUNK)
    qmasks = _lane_masks(N_HEADS * GLA_DK, GLA_DK)
    vmasks = _lane_masks(N_HEADS * GLA_DV, GLA_DV)
    bd = (_div(_iota((256, 128), 0), GLA_DV) == _div(_iota((256, 128), 1), GLA_DK)).astype(F32)
    hsum = _head_ones(N_HEADS * GLA_DV, GLA_DV)
    nw = nw_ref[...]

    n4 = N_HEADS * GLA_DV

    def prepare(cis):
        n = range(len(cis))
        rows = [pl.ds(pl.multiple_of(ci * CHUNK, CHUNK), CHUNK) for ci in cis]
        g = [_dot01_left(ltri, la_ref[r, :]) for r in rows]
        q = [x_ref[r, 0:128] * (GLA_DK ** -0.5) for r in rows]
        k = [x_ref[r, 128:256] for r in rows]
        v = [x_ref[r, 256:512] for r in rows]
        intra = [[] for _ in n]
        for s in range(CHUNK // SUB):
            lo, hi = s * SUB, (s + 1) * SUB
            sc = []
            for i in n:
                gs = jnp.zeros((1, 128), F32) if s == 0 else g[i][lo - 1:lo]
                qs = q[i][lo:hi] * jnp.exp(g[i][lo:hi] - gs)
                kt = k[i][:hi] * jnp.exp(jnp.minimum(gs - g[i][:hi], EXP_CLAMP))
                sc.append(_dot_nt(_stack(qs, qmasks), kt))
            causal = _iota(sc[0].shape, 1) <= lo + (_iota(sc[0].shape, 0) & (SUB - 1))
            p = [_dot(jnp.where(causal, sc[i], 0.0), v[i][:hi]) for i in n]
            for i in n:
                intra[i].append(sum(p[i][h * SUB:(h + 1) * SUB] * vmasks[h] for h in range(N_HEADS)))
        for i in n:
            g_last = g[i][CHUNK - 1:CHUNK]
            o_s[rows[i], :] = jnp.concatenate(intra[i], axis=0)
            qg_s[rows[i], :] = (q[i] * jnp.exp(g[i])).astype(BF16)
            kv_s[pl.ds(pl.multiple_of(cis[i] * n4, n4), n4), :] = bd * _dot_tn(v[i], k[i] * jnp.exp(g_last - g[i]))
            gt_s[pl.ds(pl.multiple_of(cis[i] * 8, 8), 8), :] = jnp.broadcast_to(jnp.exp(g_last), (8, 128))

    def advance(cis):
        for ci in cis:
            rows = pl.ds(pl.multiple_of(ci * CHUNK, CHUNK), CHUNK)
            st = st_ref[...]
            o_s[rows, :] += _dot_nt(qg_s[rows, :], st)
            st_ref[...] = (st * gt_s[pl.ds(pl.multiple_of(ci * 8, 8), 1), :]
                           + kv_s[pl.ds(pl.multiple_of(ci * n4, n4), n4), :])

    _for_chunks(tb // CHUNK, prepare, PREP_GROUP)
    _for_chunks(tb // CHUNK, advance, 2)
    o = o_s[...]
    ms = _dot01_right(o * o, hsum) * (1.0 / GLA_DV)
    y_ref[...] = (o * lax.rsqrt(ms + NORM_EPS) * nw * _silu(x_ref[:, 512:768])).astype(y_ref.dtype)


def _gla(x, gup, gb, nw):
    b, tp, _ = x.shape
    tb = _pick(tp, (640, 128, 64))
    nc = tb // CHUNK
    full = lambda a: pl.BlockSpec(a.shape, lambda i, j: (0, 0))
    return pl.pallas_call(
        _gla_kernel,
        grid=(b, tp // tb),
        in_specs=[pl.BlockSpec((None, tb, W_GLA), lambda i, j: (i, j, 0)), full(gup), full(gb), full(nw)],
        out_specs=pl.BlockSpec((None, tb, 256), lambda i, j: (i, j, 0)),
        out_shape=jax.ShapeDtypeStruct((b, tp, 256), BF16),
        scratch_shapes=[pltpu.VMEM((256, 128), F32), pltpu.VMEM((tb, 128), F32),
                        pltpu.VMEM((tb, 256), F32), pltpu.VMEM((tb, 128), BF16),
                        pltpu.VMEM((nc * N_HEADS * GLA_DV, 128), F32), pltpu.VMEM((nc * 8, 128), F32)],
        compiler_params=_cparams("parallel", "arbitrary"),
        name="gla",
    )(x, gup, gb, nw)


def _gdn_kernel(x_ref, cw_ref, alog_ref, dtb_ref, nw_ref, y_ref,
                s_ref, xp_ref, q_ref, k_ref, v_ref, beta_ref, gd_ref,
                u_s, w_s, attn_s, qd_s, ke_s, gt_s, o_s):
    tb = x_ref.shape[0]
    first = pl.program_id(1) == 0

    @pl.when(first)
    def _():
        s_ref[...] = jnp.zeros(s_ref.shape, F32)
        xp_ref[0:8, :] = jnp.zeros((8, 768), F32)

    @pl.when(jnp.logical_not(first))
    def _():
        xp_ref[0:8, :] = xp_ref[tb:tb + 8, :]

    xp_ref[8:tb + 8, :] = x_ref[:, 0:768]
    conv = sum(cw_ref[j:j + 1, :] * xp_ref[8 - (GDN_CONV - 1) + j:8 - (GDN_CONV - 1) + j + tb, :]
               for j in range(GDN_CONV))
    c = _silu(conv)
    hsum = _head_ones(256, GDN_DK)
    q = c[:, 0:256]
    k = c[:, 256:512]
    q_ref[...] = q * lax.rsqrt(_dot01_right(q * q, hsum) + L2_EPS) * (GDN_DK ** -0.5)
    k_ref[...] = k * lax.rsqrt(_dot01_right(k * k, hsum) + L2_EPS)
    v_ref[...] = c[:, 512:768]
    gates = x_ref[:, 1024:1152]
    beta_ref[...] = _sigmoid(gates)
    gd_ref[...] = -jnp.exp(alog_ref[...]) * _softplus(gates + dtb_ref[...])

    ltri = _ltri(CHUNK)
    masks = _lane_masks(256, GDN_DK)
    expand = lambda off: (_iota((128, 256), 0) == _div(_iota((128, 256), 1), GDN_DK) + off).astype(BF16)
    exp_beta, exp_g = expand(0), expand(N_HEADS)
    r = _iota((256, 256), 0)
    cidx = _iota((256, 256), 1)
    same = _div(r, CHUNK) == _div(cidx, CHUNK)
    incl = jnp.logical_and(same, r >= cidx)
    strict = jnp.logical_and(same, r > cidx)
    eye = (r == cidx).astype(F32)
    bd = same.astype(F32)
    nw = nw_ref[...]

    def prepare(cis):
        n = range(len(cis))
        rows = [pl.ds(pl.multiple_of(ci * CHUNK, CHUNK), CHUNK) for ci in cis]
        srows = [pl.ds(pl.multiple_of(ci * (N_HEADS * CHUNK), N_HEADS * CHUNK), N_HEADS * CHUNK) for ci in cis]
        q = [q_ref[r, :] for r in rows]
        k = [k_ref[r, :] for r in rows]
        v = [v_ref[r, :] for r in rows]
        bexp = [_dot01_right(beta_ref[r, :], exp_beta) for r in rows]
        gcum = [_dot01_left(ltri, gd_ref[r, :]) for r in rows]
        gexp = [_dot01_right(g, exp_g) for g in gcum]
        kst = [_stack(x, masks) for x in k]
        kb = [k[i] * bexp[i] for i in n]
        kk = [_dot_nt(_stack(kb[i], masks), kst[i]) for i in n]
        qk = [_dot_nt(_stack(q[i], masks), kst[i]) for i in n]
        dec = []
        for g in gexp:
            gcol = jnp.sum(_stack(g, masks), axis=1, keepdims=True) * (1.0 / GDN_DK)
            grow = jnp.sum(eye * gcol, axis=0, keepdims=True)
            dec.append(jnp.exp(jnp.minimum(gcol - grow, 0.0)))
        for i in n:
            attn_s[srows[i], :] = jnp.where(incl, qk[i] * dec[i], 0.0).astype(BF16)
        t = _neumann_inverse([-jnp.where(strict, kk[i] * dec[i], 0.0) for i in n], eye)
        u = [_dot(t[i], _stack(v[i] * bexp[i], masks)) for i in n]
        w = [_dot(t[i], _stack(kb[i] * jnp.exp(gexp[i]), masks)) for i in n]
        for i in n:
            u_s[rows[i], :] = _unstack(u[i], CHUNK)
            w_s[rows[i], :] = _unstack(w[i], CHUNK).astype(BF16)
            g_last = gexp[i][CHUNK - 1:CHUNK]
            qd_s[rows[i], :] = (q[i] * jnp.exp(gexp[i])).astype(BF16)
            ke_s[rows[i], :] = (k[i] * jnp.exp(g_last - gexp[i])).astype(BF16)
            gt_s[pl.ds(pl.multiple_of(cis[i] * 8, 8), 8), :] = jnp.broadcast_to(jnp.exp(g_last), (8, 256))

    def advance(cis):
        for ci in cis:
            advance_one(ci)

    def advance_one(ci):
        rows = pl.ds(pl.multiple_of(ci * CHUNK, CHUNK), CHUNK)
        srows = pl.ds(pl.multiple_of(ci * (N_HEADS * CHUNK), N_HEADS * CHUNK), N_HEADS * CHUNK)
        s = s_ref[...]
        sb = s.astype(BF16)
        v_new = u_s[rows, :] - jnp.dot(w_s[rows, :], sb, preferred_element_type=F32)
        o_s[rows, :] = (jnp.dot(qd_s[rows, :], sb, preferred_element_type=F32)
                        + _unstack(_dot(attn_s[srows, :], _stack(v_new, masks)), CHUNK))
        g_tot = gt_s[pl.ds(pl.multiple_of(ci * 8, 8), 1), :]
        s_ref[...] = s * g_tot + bd * _dot_tn(ke_s[rows, :], v_new)

    _for_chunks(tb // CHUNK, prepare, PREP_GROUP)
    _for_chunks(tb // CHUNK, advance, 2)
    o = o_s[...]
    ms = _dot01_right(o * o, hsum) * (1.0 / GDN_DK)
    y_ref[...] = (o * lax.rsqrt(ms + NORM_EPS) * nw * _silu(x_ref[:, 768:1024])).astype(y_ref.dtype)


def _gdn(x, cw, alog, dtb, nw):
    b, tp, _ = x.shape
    tb = _pick(tp, (640, 128, 64))
    nc = tb // CHUNK
    full = lambda a: pl.BlockSpec(a.shape, lambda i, j: (0, 0))
    return pl.pallas_call(
        _gdn_kernel,
        grid=(b, tp // tb),
        in_specs=[pl.BlockSpec((None, tb, W_GDN), lambda i, j: (i, j, 0)),
                  full(cw), full(alog), full(dtb), full(nw)],
        out_specs=pl.BlockSpec((None, tb, 256), lambda i, j: (i, j, 0)),
        out_shape=jax.ShapeDtypeStruct((b, tp, 256), BF16),
        scratch_shapes=[pltpu.VMEM((256, 256), F32), pltpu.VMEM((tb + 8, 768), F32),
                        pltpu.VMEM((tb, 256), F32), pltpu.VMEM((tb, 256), F32), pltpu.VMEM((tb, 256), F32),
                        pltpu.VMEM((tb, 128), F32), pltpu.VMEM((tb, 128), F32),
                        pltpu.VMEM((tb, 256), F32), pltpu.VMEM((tb, 256), BF16),
                        pltpu.VMEM((nc * N_HEADS * CHUNK, 256), BF16),
                        pltpu.VMEM((tb, 256), BF16), pltpu.VMEM((tb, 256), BF16),
                        pltpu.VMEM((nc * 8, 256), F32), pltpu.VMEM((tb, 256), F32)],
        compiler_params=_cparams("parallel", "arbitrary"),
        name="gdn",
    )(x, cw, alog, dtb, nw)


def _rwkv_kernel(x_ref, mu_ref, w2a2_ref, w0_ref, a0_ref, g2_ref, kk_ref, ka_ref, rk_ref,
                 lnw_ref, lnb_ref, y_ref,
                 s_ref, xp_ref, r_s, k_s, v_s, kk_s, b_s, lw_s, g_s,
                 at_s, z_s, arb_s, yv_s, rt_s, be_s, vk_s, gt_s, y_s):
    tb = x_ref.shape[0]
    first = pl.program_id(1) == 0

    @pl.when(first)
    def _():
        s_ref[...] = jnp.zeros(s_ref.shape, F32)
        xp_ref[0:8, :] = jnp.zeros((8, W_RWKV), F32)

    @pl.when(jnp.logical_not(first))
    def _():
        xp_ref[0:8, :] = xp_ref[tb:tb + 8, :]

    x = x_ref[...]
    xp_ref[8:tb + 8, :] = x
    z = x + (xp_ref[7:tb + 7, :] - x) * mu_ref[...]
    r = z[:, 0:256]
    k = z[:, 256:512]
    wa = z[:, 768:896]
    wa = jnp.where(_iota(wa.shape, 1) < 64, jnp.tanh(wa), wa)
    pre = _dot(wa, w2a2_ref[...])
    w_log = -_softplus(-(w0_ref[...] + pre[:, 0:256])) - 0.5
    a = _sigmoid(a0_ref[...] + pre[:, 256:512])
    hsum = _head_ones(256, RWKV_N)
    kkv = k * kk_ref[...]
    kkn = kkv * lax.rsqrt(_dot01_right(kkv * kkv, hsum) + L2_EPS)
    r_s[...] = r
    k_s[...] = k * (1.0 + (a - 1.0) * ka_ref[...])
    v_s[...] = z[:, 512:768]
    kk_s[...] = kkn
    b_s[...] = kkn * a
    lw_s[...] = -jnp.exp(w_log)
    g_s[...] = _dot(_sigmoid(z[:, 896:1152]), g2_ref[...])

    ltri = _ltri(CHUNK)
    masks = _lane_masks(256, RWKV_N)
    rr = _iota((256, 256), 0)
    cc = _iota((256, 256), 1)
    same = _div(rr, CHUNK) == _div(cc, CHUNK)
    incl = jnp.logical_and(same, rr >= cc)
    strict = jnp.logical_and(same, rr > cc)
    eye = (rr == cc).astype(F32)
    bd = same.astype(F32)
    rk = rk_ref[...]
    lnw = lnw_ref[...]
    lnb = lnb_ref[...]

    n4 = N_HEADS * CHUNK

    def prepare(cis):
        n = range(len(cis))
        rows = [pl.ds(pl.multiple_of(ci * CHUNK, CHUNK), CHUNK) for ci in cis]
        srows = [pl.ds(pl.multiple_of(ci * n4, n4), n4) for ci in cis]
        lw = [lw_s[r, :] for r in rows]
        gl = [_dot01_left(ltri, x) for x in lw]
        v = [v_s[r, :] for r in rows]
        k = [k_s[r, :] for r in rows]
        b = [b_s[r, :] for r in rows]
        e_neg = [jnp.exp(-g) for g in gl]
        a_st = [_stack(-kk_s[rows[i], :] * jnp.exp(gl[i] - lw[i]), masks) for i in n]
        r_t = [r_s[rows[i], :] * jnp.exp(gl[i]) for i in n]
        amat = []
        for i in n:
            lhs = jnp.concatenate([a_st[i], _stack(r_t[i], masks)], axis=0)
            rhs = jnp.concatenate([_stack(b[i] * e_neg[i], masks), _stack(k[i] * e_neg[i], masks)], axis=0)
            amat.append(_dot_nt(lhs, rhs))
        vst = [_stack(x, masks) for x in v]
        av = [_dot(jnp.where(strict, amat[i][0:n4, n4:], 0.0), vst[i]) for i in n]
        yv = [_dot(jnp.where(incl, amat[i][n4:, n4:], 0.0), vst[i]) for i in n]
        t = _neumann_inverse([jnp.where(strict, m[0:n4, 0:n4], 0.0) for m in amat], eye)
        z = [_dot(t[i], av[i]) for i in n]
        at = [_dot(t[i], a_st[i]) for i in n]
        for i in n:
            g_last = gl[i][CHUNK - 1:CHUNK]
            e_end = jnp.exp(g_last - gl[i])
            z_s[srows[i], :] = z[i]
            at_s[srows[i], :] = at[i].astype(BF16)
            arb_s[srows[i], :] = jnp.where(incl, amat[i][n4:, 0:n4], 0.0).astype(BF16)
            yv_s[rows[i], :] = _unstack(yv[i], CHUNK)
            rt_s[rows[i], :] = r_t[i].astype(BF16)
            be_s[rows[i], :] = (b[i] * e_end).astype(BF16)
            vk_s[srows[i], :] = bd * _dot_tn(v[i], k[i] * e_end)
            gt_s[pl.ds(pl.multiple_of(cis[i] * 8, 8), 8), :] = jnp.broadcast_to(jnp.exp(g_last), (8, 256))

    def advance(cis):
        for ci in cis:
            rows = pl.ds(pl.multiple_of(ci * CHUNK, CHUNK), CHUNK)
            srows = pl.ds(pl.multiple_of(ci * n4, n4), n4)
            s = s_ref[...]
            sb = s.astype(BF16)
            u_st = _dot_nt(at_s[srows, :], sb) + z_s[srows, :]
            y_s[rows, :] = (_dot_nt(rt_s[rows, :], sb) + _unstack(_dot(arb_s[srows, :], u_st), CHUNK)
                            + yv_s[rows, :])
            g_tot = gt_s[pl.ds(pl.multiple_of(ci * 8, 8), 1), :]
            s_ref[...] = s * g_tot + bd * _dot_tn(_unstack(u_st, CHUNK), be_s[rows, :]) + vk_s[srows, :]

    _for_chunks(tb // CHUNK, prepare, PREP_GROUP)
    _for_chunks(tb // CHUNK, advance, 2)
    y = y_s[...]
    v = v_s[...]
    mean = _dot01_right(y, hsum) * (1.0 / RWKV_N)
    d = y - mean
    var = _dot01_right(d * d, hsum) * (1.0 / RWKV_N)
    yn = d * lax.rsqrt(var + RWKV_LN_EPS) * lnw + lnb
    bonus = _dot01_right(r_s[...] * k_s[...] * rk, hsum) * v
    y_ref[...] = ((yn + bonus) * g_s[...]).astype(y_ref.dtype)


def _rwkv(x, mu, w2a2, w0, a0, g2, kk, ka, rk, lnw, lnb):
    b, tp, _ = x.shape
    tb = _pick(tp, (640, 128, 64))
    nc = tb // CHUNK
    full = lambda a: pl.BlockSpec(a.shape, lambda i, j: (0, 0))
    small = (mu, w2a2, w0, a0, g2, kk, ka, rk, lnw, lnb)
    stacked = lambda dt: pltpu.VMEM((nc * N_HEADS * CHUNK, 256), dt)
    return pl.pallas_call(
        _rwkv_kernel,
        grid=(b, tp // tb),
        in_specs=[pl.BlockSpec((None, tb, W_RWKV), lambda i, j: (i, j, 0))] + [full(a) for a in small],
        out_specs=pl.BlockSpec((None, tb, 256), lambda i, j: (i, j, 0)),
        out_shape=jax.ShapeDtypeStruct((b, tp, 256), BF16),
        scratch_shapes=[pltpu.VMEM((256, 256), F32), pltpu.VMEM((tb + 8, W_RWKV), F32)]
        + [pltpu.VMEM((tb, 256), F32) for _ in range(7)]
        + [stacked(BF16), stacked(F32), stacked(BF16), pltpu.VMEM((tb, 256), F32),
           pltpu.VMEM((tb, 256), BF16), pltpu.VMEM((tb, 256), BF16), stacked(F32),
           pltpu.VMEM((nc * 8, 256), F32), pltpu.VMEM((tb, 256), F32)],
        compiler_params=_cparams("parallel", "arbitrary"),
        name="rwkv",
    )(x, *small)


def _mla_prep_kernel(x_ref, qnw_ref, kvnw_ref, wqa_ref, wqb_ref, wk_ref, wv_ref, ea_ref, eb_ref,
                     c1_ref, s1_ref, q_ref, k_ref, v_ref):
    x = x_ref[...]
    qn = _rms(x[:, 0:256], qnw_ref[...]).astype(BF16)
    kvn = _rms(x[:, 256:384], kvnw_ref[...]).astype(BF16)
    kpe = x[:, 384:512].astype(BF16)
    c1 = c1_ref[...]
    s1 = s1_ref[...]
    qa = jnp.dot(qn, wqa_ref[...], preferred_element_type=F32)
    qb = jnp.dot(qn, wqb_ref[...], preferred_element_type=F32)
    kn = jnp.dot(kvn, wk_ref[...], preferred_element_type=F32)
    kp = (jnp.dot(kpe, ea_ref[...], preferred_element_type=F32) * c1
          + jnp.dot(kpe, eb_ref[...], preferred_element_type=F32) * s1)
    ones_lane = ((_iota((1, N_HEADS * MLA_SLOT), 1) & (MLA_SLOT - 1)) == MLA_V).astype(F32)
    v_ref[...] = (jnp.dot(kvn, wv_ref[...], preferred_element_type=F32) + ones_lane).astype(BF16)
    for h in range(N_HEADS):
        sl = slice(h * MLA_SLOT, (h + 1) * MLA_SLOT)
        q_ref[:, sl] = (qa[:, sl] * c1 + qb[:, sl] * s1).astype(BF16)
        k_ref[:, sl] = (kn[:, sl] + kp).astype(BF16)


def _mla_prep(x, qnw, kvnw, wqa, wqb, wk, wv, ea, eb, c1, s1):
    b, tp, _ = x.shape
    tm = _pick(tp, (640, 128, 64))
    full = lambda a: pl.BlockSpec(a.shape, lambda i, j: (0, 0))
    wide = N_HEADS * MLA_SLOT
    out = pl.BlockSpec((None, tm, wide), lambda i, j: (i, j, 0))
    tab = pl.BlockSpec((tm, MLA_SLOT), lambda i, j: (j, 0))
    return pl.pallas_call(
        _mla_prep_kernel,
        grid=(b, tp // tm),
        in_specs=[pl.BlockSpec((None, tm, W_MLA), lambda i, j: (i, j, 0))]
        + [full(a) for a in (qnw, kvnw, wqa, wqb, wk, wv, ea, eb)] + [tab, tab],
        out_specs=[out, out, out],
        out_shape=[jax.ShapeDtypeStruct((b, tp, wide), BF16)] * 3,
        compiler_params=_cparams("parallel", "parallel"),
        name="mla_prep",
    )(x, qnw, kvnw, wqa, wqb, wk, wv, ea, eb, c1, s1)


FLASH_HEADS = 4


def _flash_kernel(q_ref, k_ref, v_ref, o_ref, m_ref, acc_ref):
    qi = pl.program_id(2)
    t = q_ref.shape[0]
    m_ref[...] = jnp.full(m_ref.shape, NEG_INF, F32)
    acc_ref[...] = jnp.zeros(acc_ref.shape, F32)

    def block(j, diagonal):
        rows = pl.ds(pl.multiple_of(j * t, t), t)
        for h in range(FLASH_HEADS):
            sl = slice(h * MLA_SLOT, (h + 1) * MLA_SLOT)
            s = lax.dot_general(q_ref[:, sl], k_ref[rows, sl], (((1,), (1,)), ((), ())),
                                preferred_element_type=F32)
            if diagonal:
                s = jnp.where(_iota(s.shape, 0) >= _iota(s.shape, 1), s, NEG_INF)
            m_old = m_ref[h]
            m_new = jnp.maximum(m_old, jnp.max(s, axis=-1, keepdims=True))
            p = jnp.exp2(s - m_new).astype(BF16)
            acc_ref[h] = (jnp.exp2(m_old - m_new) * acc_ref[h]
                          + jnp.dot(p, v_ref[rows, sl], preferred_element_type=F32))
            m_ref[h] = m_new

    def full_block(j, carry):
        block(j, False)
        return carry

    lax.fori_loop(0, qi, full_block, 0)
    block(qi, True)
    for h in range(FLASH_HEADS):
        acc = acc_ref[h]
        o_ref[:, h * MLA_SLOT:(h + 1) * MLA_SLOT] = (acc / acc[:, MLA_V:MLA_V + 1]).astype(o_ref.dtype)


def _flash(q, k, v):
    b, tp, wide = q.shape
    t = _pick(tp, (640, 128, 64))
    w = FLASH_HEADS * MLA_SLOT
    qspec = pl.BlockSpec((None, t, w), lambda i, h, qi: (i, qi, h))
    kspec = pl.BlockSpec((None, tp, w), lambda i, h, qi: (i, 0, h))
    return pl.pallas_call(
        _flash_kernel,
        grid=(b, wide // w, tp // t),
        in_specs=[qspec, kspec, kspec],
        out_specs=qspec,
        out_shape=jax.ShapeDtypeStruct((b, tp, wide), BF16),
        scratch_shapes=[pltpu.VMEM((FLASH_HEADS, t, 1), F32), pltpu.VMEM((FLASH_HEADS, t, MLA_SLOT), F32)],
        compiler_params=_cparams("parallel", "parallel", "arbitrary"),
        name="mla_flash",
    )(q, k, v)


def _router_kernel(h_ref, nw_ref, wr_ref, xn_ref, info_ref):
    xn = _rms(h_ref[...], nw_ref[...])
    xn_ref[...] = xn
    logits = jnp.dot(xn, wr_ref[...], preferred_element_type=F32, precision=lax.Precision.HIGHEST)
    lane = _iota(logits.shape, 1).astype(F32)
    valid = lane < N_EXPERTS
    l0 = jnp.where(valid, logits, NEG_INF)
    m1 = jnp.max(l0, axis=-1, keepdims=True)
    i1 = jnp.min(jnp.where(l0 == m1, lane, float(LANE)), axis=-1, keepdims=True)
    l1 = jnp.where(lane == i1, NEG_INF, l0)
    m2 = jnp.max(l1, axis=-1, keepdims=True)
    i2 = jnp.min(jnp.where(l1 == m2, lane, float(LANE)), axis=-1, keepdims=True)
    e2 = jnp.exp(m2 - m1)
    g1 = 1.0 / (1.0 + e2)
    g2 = e2 / (1.0 + e2)
    info = jnp.where(lane == 0, i1, 0.0)
    info = jnp.where(lane == 1, i2, info)
    info = jnp.where(lane == 2, g1, info)
    info = jnp.where(lane == 3, g2, info)
    info_ref[...] = info


def _router(h, nw, wr):
    n = h.shape[0]
    tm = _pick(n, (512, 256, 128, 64))
    return pl.pallas_call(
        _router_kernel,
        grid=(n // tm,),
        in_specs=[pl.BlockSpec((tm, D_MODEL), lambda i: (i, 0)), pl.BlockSpec(nw.shape, lambda i: (0, 0)),
                  pl.BlockSpec(wr.shape, lambda i: (0, 0))],
        out_specs=[pl.BlockSpec((tm, D_MODEL), lambda i: (i, 0)), pl.BlockSpec((tm, LANE), lambda i: (i, 0))],
        out_shape=[jax.ShapeDtypeStruct((n, D_MODEL), F32), jax.ShapeDtypeStruct((n, LANE), F32)],
        compiler_params=_cparams("parallel"),
        name="moe_router",
    )(h, nw, wr)


def _expert_kernel(be_ref, x_ref, gate_ref, wg_ref, wu_ref, wd_ref, out_ref, acc_ref):
    f = pl.program_id(1)

    @pl.when(f == 0)
    def _():
        acc_ref[...] = jnp.zeros(acc_ref.shape, F32)

    xb = x_ref[...].astype(BF16)
    a = jnp.dot(xb, wg_ref[...], preferred_element_type=F32)
    b = jnp.dot(xb, wu_ref[...], preferred_element_type=F32)
    acc_ref[...] += jnp.dot((_silu(a) * b).astype(BF16), wd_ref[...], preferred_element_type=F32)

    @pl.when(f == pl.num_programs(1) - 1)
    def _():
        out_ref[...] = acc_ref[...] * gate_ref[...]


def _experts(block_expert, x_rows, row_gate, wg, wu, wd, tm):
    cap = x_rows.shape[0]
    dff = wg.shape[2]
    tf = _pick(dff, (1792, 512, 256, 128))
    grid_spec = pltpu.PrefetchScalarGridSpec(
        num_scalar_prefetch=1,
        grid=(cap // tm, dff // tf),
        in_specs=[pl.BlockSpec((tm, D_MODEL), lambda i, f, be: (i, 0)),
                  pl.BlockSpec((tm, 1), lambda i, f, be: (i, 0)),
                  pl.BlockSpec((None, D_MODEL, tf), lambda i, f, be: (be[i], 0, f)),
                  pl.BlockSpec((None, D_MODEL, tf), lambda i, f, be: (be[i], 0, f)),
                  pl.BlockSpec((None, tf, D_MODEL), lambda i, f, be: (be[i], f, 0))],
        out_specs=pl.BlockSpec((tm, D_MODEL), lambda i, f, be: (i, 0)),
        scratch_shapes=[pltpu.VMEM((tm, D_MODEL), F32)],
    )
    return pl.pallas_call(
        _expert_kernel,
        grid_spec=grid_spec,
        out_shape=jax.ShapeDtypeStruct((cap, D_MODEL), F32),
        compiler_params=_cparams("parallel", "arbitrary"),
        name="moe_experts",
    )(block_expert, x_rows, row_gate, wg, wu, wd)


def _combine_kernel(h_ref, ya_ref, yb_ref, out_ref):
    out_ref[...] = h_ref[...] + (ya_ref[...] + yb_ref[...])


def _combine(h, y2):
    n = h.shape[0]
    tm = _pick(n, (1024, 512, 256, 128, 64))
    spec = pl.BlockSpec((tm, D_MODEL), lambda i: (i, 0))
    second = pl.BlockSpec((tm, D_MODEL), lambda i: (i + n // tm, 0))
    return pl.pallas_call(
        _combine_kernel, grid=(n // tm,), in_specs=[spec, spec, second], out_specs=spec,
        out_shape=jax.ShapeDtypeStruct((n, D_MODEL), F32),
        compiler_params=_cparams("parallel"), name="moe_combine",
    )(h, y2, y2)


def _sc_gather(table, idx):
    rows = idx.shape[0]
    d = table.shape[1]
    info = plsc.get_sparse_core_info()
    workers = info.num_cores * info.num_subcores
    assert rows % (workers * SC_CHUNK) == 0, (rows, workers)
    per_worker = rows // workers
    mesh = plsc.VectorSubcoreMesh(core_axis_name="c", subcore_axis_name="s")

    @functools.partial(
        pl.kernel, mesh=mesh, out_type=jax.ShapeDtypeStruct((rows, d), table.dtype),
        scratch_types=[pltpu.VMEM((SC_CHUNK,), jnp.int32), pltpu.VMEM((SC_CHUNK, d), table.dtype),
                       pltpu.SemaphoreType.DMA],
        name="sc_gather")
    def gather(table_hbm, idx_hbm, out_hbm, idx_v, rows_v, sem):
        base = (lax.axis_index("s") * info.num_cores + lax.axis_index("c")) * per_worker

        @pl.loop(0, per_worker // SC_CHUNK)
        def _(i):
            off = pl.multiple_of(base + i * SC_CHUNK, 8)
            pltpu.sync_copy(idx_hbm.at[pl.ds(off, SC_CHUNK)], idx_v)
            pltpu.async_copy(table_hbm.at[idx_v], rows_v, sem).wait()
            pltpu.sync_copy(rows_v, out_hbm.at[pl.ds(off, SC_CHUNK)])

    return gather(table, idx)


def _moe(h, nw, router, wg, wu, wd):
    n = h.shape[0]
    tm = _pick(n, (512, 64))
    wr = jnp.pad(router.astype(F32), ((0, 0), (0, LANE - N_EXPERTS)))
    xn, info = _router(h, nw, wr)
    expert = info[:, 0:2].astype(jnp.int32).reshape(-1)
    gate = info[:, 2:4].reshape(-1)
    n_assign = 2 * n
    order = jnp.argsort(expert)
    onehot = (expert[:, None] == jnp.arange(N_EXPERTS, dtype=jnp.int32)[None, :]).astype(jnp.int32)
    running = jnp.cumsum(onehot, axis=0)
    counts = running[-1]
    padded = (counts + tm - 1) // tm * tm
    pad_end = jnp.cumsum(padded)
    pad_start = pad_end - padded
    start = jnp.cumsum(counts) - counts
    n_blocks = -(-n_assign // tm) + N_EXPERTS
    cap = n_blocks * tm
    block_start = jnp.arange(n_blocks, dtype=jnp.int32) * tm
    block_expert = jnp.minimum(jnp.sum(block_start[:, None] >= pad_end[None, :], axis=1), N_EXPERTS - 1)
    block_expert = block_expert.astype(jnp.int32)
    rank = (block_start - pad_start[block_expert])[:, None] + jnp.arange(tm, dtype=jnp.int32)[None, :]
    valid = (rank < counts[block_expert][:, None]).reshape(cap)
    src = order[jnp.clip(start[block_expert][:, None] + rank, 0, n_assign - 1).reshape(cap)]
    row_token = jnp.where(valid, src // 2, 0)
    row_gate = jnp.where(valid, gate[src], 0.0)
    dest = jnp.sum(onehot * (pad_start[None, :] + running - 1), axis=1)
    x_rows = _sc_gather(xn, row_token)
    y_rows = _experts(block_expert, x_rows, row_gate[:, None], wg, wu, wd, tm)
    return _combine(h, _sc_gather(y_rows, dest.reshape(n, 2).T.reshape(n_assign)))


def _pad_cols(a, width):
    return jnp.pad(a, ((0, 0), (0, width - a.shape[1])))


def _row(a, width=None):
    a = a.reshape(1, -1).astype(F32)
    return a if width is None else _pad_cols(a, width)


def _rope_tables(tp):
    pos = jnp.arange(tp, dtype=F32)
    inv_freq = ROPE_THETA ** (-jnp.arange(0, MLA_ROPE, 2, dtype=F32) / MLA_ROPE)
    ang = pos[:, None] * inv_freq[None, :]
    cos, sin = jnp.cos(ang), jnp.sin(ang)
    ones = jnp.ones((tp, MLA_NOPE), F32)
    zeros = jnp.zeros((tp, MLA_SLOT - MLA_NOPE - MLA_ROPE), F32)
    c1 = jnp.concatenate([ones, cos, cos, zeros], axis=1)
    s1 = jnp.concatenate([0.0 * ones, -sin, sin, zeros], axis=1)
    return c1, s1


def _mla_weights(w_uq, w_ukv):
    half = MLA_ROPE // 2
    scale = (MLA_NOPE + MLA_ROPE) ** -0.5 * math.log2(math.e)
    zq =jnp.zeros((w_uq.shape[0], MLA_SLOT - MLA_NOPE - MLA_ROPE), F32)
    zn = jnp.zeros((w_uq.shape[0], MLA_NOPE), F32)
    zk = jnp.zeros((w_ukv.shape[0], MLA_SLOT - MLA_NOPE), F32)
    wqa, wqb, wk, wv = [], [], [], []
    for h in range(N_HEADS):
        q = w_uq[:, h * 96:(h + 1) * 96] * scale
        nope, x1, x2 = q[:, :MLA_NOPE], q[:, MLA_NOPE:MLA_NOPE + half], q[:, MLA_NOPE + half:]
        wqa += [nope, x1, x2, zq]
        wqb += [zn, x2, x1, zq]
        kv = w_ukv[:, h * 128:(h + 1) * 128]
        wk += [kv[:, :MLA_NOPE], zk]
        wv += [kv[:, MLA_NOPE:], zk]
    cat = lambda parts: jnp.concatenate(parts, axis=1).astype(BF16)
    ea = np.zeros((MLA_SLOT, MLA_SLOT), np.float32)
    eb = np.zeros((MLA_SLOT, MLA_SLOT), np.float32)
    for i in range(MLA_ROPE):
        ea[i, MLA_NOPE + i] = 1.0
        eb[(i + half) % MLA_ROPE, MLA_NOPE + i] = 1.0
    return cat(wqa), cat(wqb), cat(wk), cat(wv), jnp.asarray(ea, BF16), jnp.asarray(eb, BF16)


def _branch_weights(w_branch):
    wb = w_branch.astype(BF16)
    z = jnp.zeros((MLA_SLOT - MLA_V, D_MODEL), BF16)
    parts = []
    for h in range(N_HEADS):
        parts += [wb[1, h * MLA_V:(h + 1) * MLA_V], z]
    return wb[0], jnp.concatenate(parts, axis=0), wb[2], wb[3]


def _token_mixing(h, b, tp, p, l, tables):
    w_in = p["w_in"][l]
    col = lambda i, j: w_in[:, _OFF[i]:_OFF[j]]
    wg = jnp.concatenate([col(0, 3), col(4, 5), _pad_cols(col(3, 4), LANE)], axis=1).astype(BF16)
    wm = jnp.concatenate([col(5, 7), _pad_cols(col(7, 8), LANE)], axis=1).astype(BF16)
    wd = jnp.concatenate([col(8, 10), _pad_cols(col(10, 12), LANE)], axis=1).astype(BF16)
    wr = _pad_cols(col(12, 13), W_RWKV).astype(BF16)
    nw = _row(p["norm_mix"][l])
    xg, xm, xd, xr = _inproj(h, nw, wg, wm, wd, wr)
    shape3 = lambda a: a.reshape(b, tp, a.shape[1])

    gup = jnp.pad(p["gla_gate_up"][l], ((0, LANE - 16), (0, 0))).astype(BF16)
    y_gla = _gla(shape3(xg), gup, _row(p["gla_gate_bias"][l]), _row(jnp.tile(p["gla_norm"][l], N_HEADS)))

    wqa, wqb, wk, wv, ea, eb = _mla_weights(p["mla_w_uq"][l], p["mla_w_ukv"][l])
    q, k, v = _mla_prep(shape3(xm), _row(p["mla_q_norm"][l]), _row(p["mla_kv_norm"][l]),
                        wqa, wqb, wk, wv, ea, eb, *tables)
    y_mla = _flash(q, k, v)

    lanes4 = lambda a: jnp.pad(a.reshape(1, N_HEADS).astype(F32), ((0, 0), (N_HEADS, LANE - 2 * N_HEADS)))
    y_gdn = _gdn(shape3(xd), p["gdn_conv"][l].astype(F32), lanes4(p["gdn_a_log"][l]),
                 lanes4(p["gdn_dt_bias"][l]), _row(jnp.tile(p["gdn_norm"][l], N_HEADS)))

    w2a2 = jnp.zeros((LANE, 512), F32)
    w2a2 = w2a2.at[0:64, 0:256].set(p["rwkv_w2"][l]).at[64:128, 256:512].set(p["rwkv_a2"][l]).astype(BF16)
    g2 = jnp.pad(p["rwkv_g2"][l], ((0, 256 - 160), (0, 0))).astype(BF16)
    y_rwkv = _rwkv(shape3(xr), _row(p["rwkv_mu"][l], W_RWKV), w2a2, _row(p["rwkv_w0"][l]),
                   _row(p["rwkv_a0"][l]), g2, _row(p["rwkv_k_k"][l]), _row(p["rwkv_k_a"][l]),
                   _row(p["rwkv_r_k"][l]), _row(p["rwkv_ln_w"][l]), _row(p["rwkv_ln_b"][l]))

    flat = lambda a: a.reshape(b * tp, a.shape[2])
    ys = [flat(y_gla), flat(y_mla), flat(y_gdn), flat(y_rwkv)]
    return _merge(h, nw, col(13, 14).astype(BF16), ys, _branch_weights(p["w_branch"][l]),
                  p["w_out"][l].astype(BF16))


def kernel(x, meta_tokens, norm_mix, w_in, gla_gate_up, gla_gate_bias, gla_norm, mla_q_norm, mla_w_uq, mla_kv_norm, mla_w_ukv, gdn_conv, gdn_a_log, gdn_dt_bias, gdn_norm, rwkv_mu, rwkv_w0, rwkv_w2, rwkv_a0, rwkv_a2, rwkv_g2, rwkv_k_k, rwkv_k_a, rwkv_r_k, rwkv_ln_w, rwkv_ln_b, w_branch, w_out, norm_ffn, ffn_w_gate, ffn_w_up, ffn_w_down, moe_router, moe_w_gate, moe_w_up, moe_w_down, norm_final):
    p = dict(norm_mix=norm_mix, w_in=w_in, gla_gate_up=gla_gate_up, gla_gate_bias=gla_gate_bias,
             gla_norm=gla_norm, mla_q_norm=mla_q_norm, mla_w_uq=mla_w_uq, mla_kv_norm=mla_kv_norm,
             mla_w_ukv=mla_w_ukv, gdn_conv=gdn_conv, gdn_a_log=gdn_a_log, gdn_dt_bias=gdn_dt_bias,
             gdn_norm=gdn_norm, rwkv_mu=rwkv_mu, rwkv_w0=rwkv_w0, rwkv_w2=rwkv_w2, rwkv_a0=rwkv_a0,
             rwkv_a2=rwkv_a2, rwkv_g2=rwkv_g2, rwkv_k_k=rwkv_k_k, rwkv_k_a=rwkv_k_a, rwkv_r_k=rwkv_r_k,
             rwkv_ln_w=rwkv_ln_w, rwkv_ln_b=rwkv_ln_b, w_branch=w_branch, w_out=w_out)
    b, seq, d = x.shape
    t_real = N_META + seq
    tp = -(-t_real // 128) * 128
    meta = jnp.broadcast_to(meta_tokens[None].astype(x.dtype), (b, N_META, d))
    h = jnp.concatenate([meta, x, jnp.zeros((b, tp - t_real, d), x.dtype)], axis=1).reshape(b * tp, d)
    tables = _rope_tables(tp)
    depth = norm_mix.shape[0]
    for l in range(depth):
        h = _token_mixing(h, b, tp, p, l, tables)
        nw = _row(norm_ffn[l])
        if l % 2 == 0:
            h = _ffn(h, nw, ffn_w_gate[l // 2].astype(BF16), ffn_w_up[l // 2].astype(BF16),
                     ffn_w_down[l // 2].astype(BF16))
        else:
            h = _moe(h, nw, moe_router[l // 2], moe_w_gate[l // 2].astype(BF16),
                     moe_w_up[l // 2].astype(BF16), moe_w_down[l // 2].astype(BF16))
    out = _final_norm(h, _row(norm_final))
    return out.reshape(b, tp, d)[:, N_META:t_real]
```

```python
import functools
import math

import jax
import jax.numpy as jnp
import numpy as np
from jax import lax
from jax.experimental import pallas as pl
from jax.experimental.pallas import tpu as pltpu
from jax.experimental.pallas import tpu_sc as plsc

F32 = jnp.float32
BF16 = jnp.bfloat16

D_MODEL = 1024
N_META = 16
N_HEADS = 4
GLA_DK = 32
GLA_DV = 64
GLA_TAU = 16.0
MLA_NOPE = 64
MLA_ROPE = 32
MLA_V = 64
MLA_SLOT = 128
ROPE_THETA = 10000.0
GDN_DK = 64
GDN_CONV = 4
RWKV_N = 64
RWKV_LN_EPS = RWKV_N * 1e-5
CHUNK = 64
SUB = 16
PREP_GROUP = 5
MIX_BATCH = 2
N_EXPERTS = 8
SC_CHUNK = 64
NORM_EPS = 1e-6
L2_EPS = 1e-6
NEG_INF = -1e30
EXP_CLAMP = 80.0

LANE = 128
VMEM_LIMIT = 56 * 1024 * 1024

_OFF = np.cumsum([0, 128, 128, 256, 16, 256, 256, 128, 32, 768, 256, 4, 4, 1056, 4096]).tolist()
W_GLA, W_MLA, W_GDN, W_RWKV = 896, 512, 1152, 1152


def _cparams(*sem):
    return pltpu.CompilerParams(dimension_semantics=sem, vmem_limit_bytes=VMEM_LIMIT)


def _pick(n, prefs):
    for p in prefs:
        if n % p == 0:
            return p
    raise ValueError(f"no tile for {n}")


def _dot(a, b):
    return jnp.dot(a.astype(BF16), b.astype(BF16), preferred_element_type=F32)


def _dot_nt(a, b):
    return lax.dot_general(a.astype(BF16), b.astype(BF16), (((1,), (1,)), ((), ())),
                           preferred_element_type=F32)


def _dot_tn(a, b):
    return lax.dot_general(a.astype(BF16), b.astype(BF16), (((0,), (0,)), ((), ())),
                           preferred_element_type=F32)


def _split3(x):
    hi = x.astype(BF16)
    r1 = x - hi.astype(F32)
    mid = r1.astype(BF16)
    lo = (r1 - mid.astype(F32)).astype(BF16)
    return hi, mid, lo


def _dot01_left(m01, x):
    return sum(jnp.dot(m01, p, preferred_element_type=F32) for p in _split3(x))


def _dot01_right(x, m01):
    return sum(jnp.dot(p, m01, preferred_element_type=F32) for p in _split3(x))


def _iota(shape, dim):
    return lax.broadcasted_iota(jnp.int32, shape, dim)


def _div(x, w):
    return x >> int(math.log2(w))


def _ltri(n):
    return (_iota((n, n), 0) >= _iota((n, n), 1)).astype(BF16)


def _head_ones(n, w):
    return (_div(_iota((n, n), 0), w) == _div(_iota((n, n), 1), w)).astype(BF16)


def _lane_masks(width, w):
    lane = _div(_iota((1, width), 1), w)
    return [(lane == h).astype(F32) for h in range(width // w)]


def _stack(x, masks):
    return jnp.concatenate([x * m for m in masks], axis=0)


def _unstack(y, n):
    out = y[0:n]
    for h in range(1, y.shape[0] // n):
        out = out + y[h * n:(h + 1) * n]
    return out


def _rms(x, w):
    return x * lax.rsqrt(jnp.mean(x * x, axis=-1, keepdims=True) + NORM_EPS) * w


def _sigmoid(x):
    return 1.0 / (1.0 + jnp.exp(-x))


def _silu(x):
    return x * _sigmoid(x)


def _softplus(x):
    return jnp.maximum(x, 0.0) + jnp.log(1.0 + jnp.exp(-jnp.abs(x)))


def _neumann_inverse(xs, eye):
    ts = [eye + x for x in xs]
    ps = list(xs)
    for _ in range(int(math.log2(CHUNK)) - 1):
        ps = [_dot(p, p) for p in ps]
        ts = [t + _dot(t, p) for t, p in zip(ts, ps)]
    return ts


def _for_chunks(n, body, group):
    def trip(i, carry):
        body([i * group + g for g in range(group)])
        return carry

    if n >= group:
        lax.fori_loop(0, n // group, trip, 0)
    if n % group:
        body(list(range(n - n % group, n)))


def _inproj_kernel(h_ref, nw_ref, wg_ref, wm_ref, wd_ref, wr_ref, og_ref, om_ref, od_ref, or_ref):
    xb = _rms(h_ref[...], nw_ref[...]).astype(BF16)
    og_ref[...] = jnp.dot(xb, wg_ref[...], preferred_element_type=F32)
    om_ref[...] = jnp.dot(xb, wm_ref[...], preferred_element_type=F32)
    od_ref[...] = jnp.dot(xb, wd_ref[...], preferred_element_type=F32)
    or_ref[...] = jnp.dot(xb, wr_ref[...], preferred_element_type=F32)


def _inproj(h, nw, wg, wm, wd, wr):
    n = h.shape[0]
    tm = _pick(n, (512, 256, 128, 64))
    full = lambda a: pl.BlockSpec(a.shape, lambda i: (0, 0))
    row = lambda w: pl.BlockSpec((tm, w), lambda i: (i, 0))
    return pl.pallas_call(
        _inproj_kernel,
        grid=(n // tm,),
        in_specs=[row(D_MODEL), full(nw), full(wg), full(wm), full(wd), full(wr)],
        out_specs=[row(W_GLA), row(W_MLA), row(W_GDN), row(W_RWKV)],
        out_shape=[jax.ShapeDtypeStruct((n, w), F32) for w in (W_GLA, W_MLA, W_GDN, W_RWKV)],
        compiler_params=_cparams("parallel"),
        name="inproj",
    )(h, nw, wg, wm, wd, wr)


def _merge_kernel(h_ref, nw_ref, wgate_ref, yg_ref, ym_ref, yd_ref, yr_ref,
                  wbg_ref, wbm_ref, wbd_ref, wbr_ref, wout_ref, out_ref):
    x = h_ref[...]
    xb = _rms(x, nw_ref[...]).astype(BF16)
    acc = jnp.zeros(x.shape, F32)
    branches = ((yg_ref, wbg_ref), (ym_ref, wbm_ref), (yd_ref, wbd_ref), (yr_ref, wbr_ref))
    for i, (y_ref, wb_ref) in enumerate(branches):
        logits = jnp.dot(xb, wgate_ref[:, i * D_MODEL:(i + 1) * D_MODEL], preferred_element_type=F32)
        proj = jnp.dot(y_ref[...], wb_ref[...], preferred_element_type=F32)
        acc = acc + _sigmoid(logits) * proj
    out_ref[...] = x + jnp.dot(acc.astype(BF16), wout_ref[...], preferred_element_type=F32)


def _merge(h, nw, wgate, ys, wbs, wout):
    n = h.shape[0]
    tm = _pick(n, (512, 256, 128, 64))
    full = lambda a: pl.BlockSpec(a.shape, lambda i: (0, 0))
    row = lambda w: pl.BlockSpec((tm, w), lambda i: (i, 0))
    return pl.pallas_call(
        _merge_kernel,
        grid=(n // tm,),
        in_specs=[row(D_MODEL), full(nw), full(wgate)] + [row(y.shape[1]) for y in ys]
        + [full(w) for w in wbs] + [full(wout)],
        out_specs=row(D_MODEL),
        out_shape=jax.ShapeDtypeStruct((n, D_MODEL), F32),
        compiler_params=_cparams("parallel"),
        name="merge",
    )(h, nw, wgate, *ys, *wbs, wout)


def _ffn_kernel(h_ref, nw_ref, wg_ref, wu_ref, wd_ref, out_ref, xb_ref, acc_ref):
    f = pl.program_id(1)

    @pl.when(f == 0)
    def _():
        xb_ref[...] = _rms(h_ref[...], nw_ref[...]).astype(BF16)
        acc_ref[...] = jnp.zeros(acc_ref.shape, F32)

    xb = xb_ref[...]
    a = jnp.dot(xb, wg_ref[...], preferred_element_type=F32)
    b = jnp.dot(xb, wu_ref[...], preferred_element_type=F32)
    acc_ref[...] += jnp.dot((_silu(a) * b).astype(BF16), wd_ref[...], preferred_element_type=F32)

    @pl.when(f == pl.num_programs(1) - 1)
    def _():
        out_ref[...] = h_ref[...] + acc_ref[...]


def _ffn(h, nw, wg, wu, wd):
    n = h.shape[0]
    dff = wg.shape[1]
    tm = _pick(n, (512, 256, 128, 64))
    tf = _pick(dff, (1408, 512, 256, 128))
    return pl.pallas_call(
        _ffn_kernel,
        grid=(n // tm, dff // tf),
        in_specs=[pl.BlockSpec((tm, D_MODEL), lambda i, f: (i, 0)),
                  pl.BlockSpec(nw.shape, lambda i, f: (0, 0)),
                  pl.BlockSpec((D_MODEL, tf), lambda i, f: (0, f)),
                  pl.BlockSpec((D_MODEL, tf), lambda i, f: (0, f)),
                  pl.BlockSpec((tf, D_MODEL), lambda i, f: (f, 0))],
        out_specs=pl.BlockSpec((tm, D_MODEL), lambda i, f: (i, 0)),
        out_shape=jax.ShapeDtypeStruct((n, D_MODEL), F32),
        scratch_shapes=[pltpu.VMEM((tm, D_MODEL), BF16), pltpu.VMEM((tm, D_MODEL), F32)],
        compiler_params=_cparams("parallel", "arbitrary"),
        name="ffn",
    )(h, nw, wg, wu, wd)


def _final_norm_kernel(h_ref, nw_ref, out_ref):
    out_ref[...] = _rms(h_ref[...], nw_ref[...])


def _final_norm(h, nw):
    n = h.shape[0]
    tm = _pick(n, (1024, 512, 256, 128, 64))
    return pl.pallas_call(
        _final_norm_kernel,
        grid=(n // tm,),
        in_specs=[pl.BlockSpec((tm, D_MODEL), lambda i: (i, 0)), pl.BlockSpec(nw.shape, lambda i: (0, 0))],
        out_specs=pl.BlockSpec((tm, D_MODEL), lambda i: (i, 0)),
        out_shape=jax.ShapeDtypeStruct((n, D_MODEL), F32),
        compiler_params=_cparams("parallel"),
        name="final_norm",
    )(h, nw)


_DONE = object()


def _interleave(generators):
    live = list(generators)
    while live:
        live = [g for g in live if next(g, _DONE) is not _DONE]


def _mixer_kernel(body, n_params, x_ref, *refs):
    params, y_ref, scratch = refs[:n_params], refs[n_params], refs[n_params + 1:]
    parts = [body(x_ref.at[b], *params, y_ref.at[b], *[s.at[b] for s in scratch]) for b in range(x_ref.shape[0])]
    n_chunks = x_ref.shape[1] // CHUNK
    _for_chunks(n_chunks, lambda cis: [prepare(cis) for prepare, _, _ in parts], PREP_GROUP)
    _for_chunks(n_chunks, lambda cis: [_interleave([advance(ci) for _, advance, _ in parts]) for ci in cis], 1)
    for _, _, finish in parts:
        finish()


def _mixer_block(tp):
    return _pick(tp, (320, 128, 64))


def _mixer_call(body, name, x, params, scratch, tb):
    b, tp, width = x.shape
    nb = MIX_BATCH if b % MIX_BATCH == 0 else 1
    full = lambda a: pl.BlockSpec(a.shape, lambda i, j: (0, 0))
    return pl.pallas_call(
        functools.partial(_mixer_kernel, body, len(params)),
        grid=(b // nb, tp // tb),
        in_specs=[pl.BlockSpec((nb, tb, width), lambda i, j: (i, j, 0))] + [full(a) for a in params],
        out_specs=pl.BlockSpec((nb, tb, 256), lambda i, j: (i, j, 0)),
        out_shape=jax.ShapeDtypeStruct((b, tp, 256), BF16),
        scratch_shapes=[pltpu.VMEM((nb,) + shape, dtype) for shape, dtype in scratch],
        compiler_params=_cparams("parallel", "arbitrary"),
        name=name,
    )(x, *params)


def _gla_body(x_ref, gup_ref, gb_ref, nw_ref, y_ref, st_ref, la_ref, o_s, qg_s, kv_s, gt_s):
    @pl.when(pl.program_id(1) == 0)
    def _():
        st_ref[...] = jnp.zeros(st_ref.shape, F32)

    z = _dot(x_ref[:, 768:896], gup_ref[...]) + gb_ref[...]
    la_ref[...] = -_softplus(-z) * (1.0 / GLA_TAU)

    ltri = _ltri(CHUNK)
    qmasks = _lane_masks(N_HEADS * GLA_DK, GLA_DK)
    vmasks = _lane_masks(N_HEADS * GLA_DV, GLA_DV)
    bd = (_div(_iota((256, 128), 0), GLA_DV) == _div(_iota((256, 128), 1), GLA_DK)).astype(F32)
    hsum = _head_ones(N_HEADS * GLA_DV, GLA_DV)
    nw = nw_ref[...]

    n4 = N_HEADS * GLA_DV

    def prepare(cis):
        n = range(len(cis))
        rows = [pl.ds(pl.multiple_of(ci * CHUNK, CHUNK), CHUNK) for ci in cis]
        g = [_dot01_left(ltri, la_ref[r, :]) for r in rows]
        q = [x_ref[r, 0:128] * (GLA_DK ** -0.5) for r in rows]
        k = [x_ref[r, 128:256] for r in rows]
        v = [x_ref[r, 256:512] for r in rows]
        intra = [[] for _ in n]
        for s in range(CHUNK // SUB):
            lo, hi = s * SUB, (s + 1) * SUB
            sc = []
            for i in n:
                gs = jnp.zeros((1, 128), F32) if s == 0 else g[i][lo - 1:lo]
                qs = q[i][lo:hi] * jnp.exp(g[i][lo:hi] - gs)
                kt = k[i][:hi] * jnp.exp(jnp.minimum(gs - g[i][:hi], EXP_CLAMP))
                sc.append(_dot_nt(_stack(qs, qmasks), kt))
            causal = _iota(sc[0].shape, 1) <= lo + (_iota(sc[0].shape, 0) & (SUB - 1))
            p = [_dot(jnp.where(causal, sc[i], 0.0), v[i][:hi]) for i in n]
            for i in n:
                intra[i].append(sum(p[i][h * SUB:(h + 1) * SUB] * vmasks[h] for h in range(N_HEADS)))
        for i in n:
            g_last = g[i][CHUNK - 1:CHUNK]
            o_s[rows[i], :] = jnp.concatenate(intra[i], axis=0)
            qg_s[rows[i], :] = (q[i] * jnp.exp(g[i])).astype(BF16)
            kv_s[pl.ds(pl.multiple_of(cis[i] * n4, n4), n4), :] = bd * _dot_tn(v[i], k[i] * jnp.exp(g_last - g[i]))
            gt_s[pl.ds(pl.multiple_of(cis[i] * 8, 8), 8), :] = jnp.broadcast_to(jnp.exp(g_last), (8, 128))

    def advance(ci):
        rows = pl.ds(pl.multiple_of(ci * CHUNK, CHUNK), CHUNK)
        st = st_ref[...]
        inter = _dot_nt(qg_s[rows, :], st)
        yield
        o_s[rows, :] += inter
        st_ref[...] = (st * gt_s[pl.ds(pl.multiple_of(ci * 8, 8), 1), :]
                       + kv_s[pl.ds(pl.multiple_of(ci * n4, n4), n4), :])

    def finish():
        o = o_s[...]
        ms = _dot01_right(o * o, hsum) * (1.0 / GLA_DV)
        y_ref[...] = (o * lax.rsqrt(ms + NORM_EPS) * nw * _silu(x_ref[:, 512:768])).astype(y_ref.dtype)

    return prepare, advance, finish


def _gla(x, gup, gb, nw):
    tb = _mixer_block(x.shape[1])
    nc = tb // CHUNK
    scratch = [((256, 128), F32), ((tb, 128), F32), ((tb, 256), F32), ((tb, 128), BF16),
               ((nc * N_HEADS * GLA_DV, 128), F32), ((nc * 8, 128), F32)]
    return _mixer_call(_gla_body, "gla", x, (gup, gb, nw), scratch, tb)


def _gdn_body(x_ref, cw_ref, alog_ref, dtb_ref, nw_ref, y_ref,
              s_ref, xp_ref, q_ref, k_ref, v_ref, beta_ref, gd_ref,
              u_s, w_s, attn_s, qd_s, ke_s, gt_s, o_s):
    tb = x_ref.shape[0]
    first = pl.program_id(1) == 0

    @pl.when(first)
    def _():
        s_ref[...] = jnp.zeros(s_ref.shape, F32)
        xp_ref[0:8, :] = jnp.zeros((8, 768), F32)

    @pl.when(jnp.logical_not(first))
    def _():
        xp_ref[0:8, :] = xp_ref[tb:tb + 8, :]

    xp_ref[8:tb + 8, :] = x_ref[:, 0:768]
    conv = sum(cw_ref[j:j + 1, :] * xp_ref[8 - (GDN_CONV - 1) + j:8 - (GDN_CONV - 1) + j + tb, :]
               for j in range(GDN_CONV))
    c = _silu(conv)
    hsum = _head_ones(256, GDN_DK)
    q = c[:, 0:256]
    k = c[:, 256:512]
    q_ref[...] = q * lax.rsqrt(_dot01_right(q * q, hsum) + L2_EPS) * (GDN_DK ** -0.5)
    k_ref[...] = k * lax.rsqrt(_dot01_right(k * k, hsum) + L2_EPS)
    v_ref[...] = c[:, 512:768]
    gates = x_ref[:, 1024:1152]
    beta_ref[...] = _sigmoid(gates)
    gd_ref[...] = -jnp.exp(alog_ref[...]) * _softplus(gates + dtb_ref[...])

    ltri = _ltri(CHUNK)
    masks = _lane_masks(256, GDN_DK)
    expand = lambda off: (_iota((128, 256), 0) == _div(_iota((128, 256), 1), GDN_DK) + off).astype(BF16)
    exp_beta, exp_g = expand(0), expand(N_HEADS)
    r = _iota((256, 256), 0)
    cidx = _iota((256, 256), 1)
    same = _div(r, CHUNK) == _div(cidx, CHUNK)
    incl = jnp.logical_and(same, r >= cidx)
    strict = jnp.logical_and(same, r > cidx)
    eye = (r == cidx).astype(F32)
    bd = same.astype(F32)
    nw = nw_ref[...]

    def prepare(cis):
        n = range(len(cis))
        rows = [pl.ds(pl.multiple_of(ci * CHUNK, CHUNK), CHUNK) for ci in cis]
        srows = [pl.ds(pl.multiple_of(ci * (N_HEADS * CHUNK), N_HEADS * CHUNK), N_HEADS * CHUNK) for ci in cis]
        q = [q_ref[r, :] for r in rows]
        k = [k_ref[r, :] for r in rows]
        v = [v_ref[r, :] for r in rows]
        bexp = [_dot01_right(beta_ref[r, :], exp_beta) for r in rows]
        gcum = [_dot01_left(ltri, gd_ref[r, :]) for r in rows]
        gexp = [_dot01_right(g, exp_g) for g in gcum]
        kst = [_stack(x, masks) for x in k]
        kb = [k[i] * bexp[i] for i in n]
        kk = [_dot_nt(_stack(kb[i], masks), kst[i]) for i in n]
        qk = [_dot_nt(_stack(q[i], masks), kst[i]) for i in n]
        dec = []
        for g in gexp:
            gcol = jnp.sum(_stack(g, masks), axis=1, keepdims=True) * (1.0 / GDN_DK)
            grow = jnp.sum(eye * gcol, axis=0, keepdims=True)
            dec.append(jnp.exp(jnp.minimum(gcol - grow, 0.0)))
        for i in n:
            attn_s[srows[i], :] = jnp.where(incl, qk[i] * dec[i], 0.0).astype(BF16)
        t = _neumann_inverse([-jnp.where(strict, kk[i] * dec[i], 0.0) for i in n], eye)
        u = [_dot(t[i], _stack(v[i] * bexp[i], masks)) for i in n]
        w = [_dot(t[i], _stack(kb[i] * jnp.exp(gexp[i]), masks)) for i in n]
        for i in n:
            u_s[rows[i], :] = _unstack(u[i], CHUNK)
            w_s[rows[i], :] = _unstack(w[i], CHUNK).astype(BF16)
            g_last = gexp[i][CHUNK - 1:CHUNK]
            qd_s[rows[i], :] = (q[i] * jnp.exp(gexp[i])).astype(BF16)
            ke_s[rows[i], :] = (k[i] * jnp.exp(g_last - gexp[i])).astype(BF16)
            gt_s[pl.ds(pl.multiple_of(cis[i] * 8, 8), 8), :] = jnp.broadcast_to(jnp.exp(g_last), (8, 256))

    def advance(ci):
        rows = pl.ds(pl.multiple_of(ci * CHUNK, CHUNK), CHUNK)
        srows = pl.ds(pl.multiple_of(ci * (N_HEADS * CHUNK), N_HEADS * CHUNK), N_HEADS * CHUNK)
        s = s_ref[...]
        sb = s.astype(BF16)
        ws = jnp.dot(w_s[rows, :], sb, preferred_element_type=F32)
        qs = jnp.dot(qd_s[rows, :], sb, preferred_element_type=F32)
        yield
        v_new = u_s[rows, :] - ws
        av = _dot(attn_s[srows, :], _stack(v_new, masks))
        kv = _dot_tn(ke_s[rows, :], v_new)
        yield
        o_s[rows, :] = qs + _unstack(av, CHUNK)
        s_ref[...] = s * gt_s[pl.ds(pl.multiple_of(ci * 8, 8), 1), :] + bd * kv

    def finish():
        o = o_s[...]
        ms = _dot01_right(o * o, hsum) * (1.0 / GDN_DK)
        y_ref[...] = (o * lax.rsqrt(ms + NORM_EPS) * nw * _silu(x_ref[:, 768:1024])).astype(y_ref.dtype)

    return prepare, advance, finish


def _gdn(x, cw, alog, dtb, nw):
    tb = _mixer_block(x.shape[1])
    nc = tb // CHUNK
    scratch = [((256, 256), F32), ((tb + 8, 768), F32),
               ((tb, 256), F32), ((tb, 256), F32), ((tb, 256), F32), ((tb, 128), F32), ((tb, 128), F32),
               ((tb, 256), F32), ((tb, 256), BF16), ((nc * N_HEADS * CHUNK, 256), BF16),
               ((tb, 256), BF16), ((tb, 256), BF16), ((nc * 8, 256), F32), ((tb, 256), F32)]
    return _mixer_call(_gdn_body, "gdn", x, (cw, alog, dtb, nw), scratch, tb)


def _rwkv_body(x_ref, mu_ref, w2a2_ref, w0_ref, a0_ref, g2_ref, kk_ref, ka_ref, rk_ref,
               lnw_ref, lnb_ref, y_ref,
               s_ref, xp_ref, r_s, k_s, v_s, kk_s, b_s, lw_s, g_s,
               at_s, z_s, arb_s, yv_s, rt_s, be_s, vk_s, gt_s, y_s):
    tb = x_ref.shape[0]
    first = pl.program_id(1) == 0

    @pl.when(first)
    def _():
        s_ref[...] = jnp.zeros(s_ref.shape, F32)
        xp_ref[0:8, :] = jnp.zeros((8, W_RWKV), F32)

    @pl.when(jnp.logical_not(first))
    def _():
        xp_ref[0:8, :] = xp_ref[tb:tb + 8, :]

    x = x_ref[...]
    xp_ref[8:tb + 8, :] = x
    z = x + (xp_ref[7:tb + 7, :] - x) * mu_ref[...]
    r = z[:, 0:256]
    k = z[:, 256:512]
    wa = z[:, 768:896]
    wa = jnp.where(_iota(wa.shape, 1) < 64, jnp.tanh(wa), wa)
    pre = _dot(wa, w2a2_ref[...])
    w_log = -_softplus(-(w0_ref[...] + pre[:, 0:256])) - 0.5
    a = _sigmoid(a0_ref[...] + pre[:, 256:512])
    hsum = _head_ones(256, RWKV_N)
    kkv = k * kk_ref[...]
    kkn = kkv * lax.rsqrt(_dot01_right(kkv * kkv, hsum) + L2_EPS)
    r_s[...] = r
    k_s[...] = k * (1.0 + (a - 1.0) * ka_ref[...])
    v_s[...] = z[:, 512:768]
    kk_s[...] = kkn
    b_s[...] = kkn * a
    lw_s[...] = -jnp.exp(w_log)
    g_s[...] = _dot(_sigmoid(z[:, 896:1152]), g2_ref[...])

    ltri = _ltri(CHUNK)
    masks = _lane_masks(256, RWKV_N)
    rr = _iota((256, 256), 0)
    cc = _iota((256, 256), 1)
    same = _div(rr, CHUNK) == _div(cc, CHUNK)
    incl = jnp.logical_and(same, rr >= cc)
    strict = jnp.logical_and(same, rr > cc)
    eye = (rr == cc).astype(F32)
    bd = same.astype(F32)
    rk = rk_ref[...]
    lnw = lnw_ref[...]
    lnb = lnb_ref[...]

    n4 = N_HEADS * CHUNK

    def prepare(cis):
        n = range(len(cis))
        rows = [pl.ds(pl.multiple_of(ci * CHUNK, CHUNK), CHUNK) for ci in cis]
        srows = [pl.ds(pl.multiple_of(ci * n4, n4), n4) for ci in cis]
        lw = [lw_s[r, :] for r in rows]
        gl = [_dot01_left(ltri, x) for x in lw]
        v = [v_s[r, :] for r in rows]
        k = [k_s[r, :] for r in rows]
        b = [b_s[r, :] for r in rows]
        e_neg = [jnp.exp(-g) for g in gl]
        a_st = [_stack(-kk_s[rows[i], :] * jnp.exp(gl[i] - lw[i]), masks) for i in n]
        r_t = [r_s[rows[i], :] * jnp.exp(gl[i]) for i in n]
        amat = []
        for i in n:
            lhs = jnp.concatenate([a_st[i], _stack(r_t[i], masks)], axis=0)
            rhs = jnp.concatenate([_stack(b[i] * e_neg[i], masks), _stack(k[i] * e_neg[i], masks)], axis=0)
            amat.append(_dot_nt(lhs, rhs))
        vst = [_stack(x, masks) for x in v]
        av = [_dot(jnp.where(strict, amat[i][0:n4, n4:], 0.0), vst[i]) for i in n]
        yv = [_dot(jnp.where(incl, amat[i][n4:, n4:], 0.0), vst[i]) for i in n]
        t = _neumann_inverse([jnp.where(strict, m[0:n4, 0:n4], 0.0) for m in amat], eye)
        z = [_dot(t[i], av[i]) for i in n]
        at = [_dot(t[i], a_st[i]) for i in n]
        for i in n:
            g_last = gl[i][CHUNK - 1:CHUNK]
            e_end = jnp.exp(g_last - gl[i])
            z_s[srows[i], :] = z[i]
            at_s[srows[i], :] = at[i].astype(BF16)
            arb_s[srows[i], :] = jnp.where(incl, amat[i][n4:, 0:n4], 0.0).astype(BF16)
            yv_s[rows[i], :] = _unstack(yv[i], CHUNK)
            rt_s[rows[i], :] = r_t[i].astype(BF16)
            be_s[rows[i], :] = (b[i] * e_end).astype(BF16)
            vk_s[srows[i], :] = bd * _dot_tn(v[i], k[i] * e_end)
            gt_s[pl.ds(pl.multiple_of(cis[i] * 8, 8), 8), :] = jnp.broadcast_to(jnp.exp(g_last), (8, 256))

    def advance(ci):
        rows = pl.ds(pl.multiple_of(ci * CHUNK, CHUNK), CHUNK)
        srows = pl.ds(pl.multiple_of(ci * n4, n4), n4)
        s = s_ref[...]
        sb = s.astype(BF16)
        u_st = _dot_nt(at_s[srows, :], sb) + z_s[srows, :]
        rs = _dot_nt(rt_s[rows, :], sb)
        yield
        au = _dot(arb_s[srows, :], u_st)
        ub = _dot_tn(_unstack(u_st, CHUNK), be_s[rows, :])
        yield
        y_s[rows, :] = rs + _unstack(au, CHUNK) + yv_s[rows, :]
        s_ref[...] = s * gt_s[pl.ds(pl.multiple_of(ci * 8, 8), 1), :] + bd * ub + vk_s[srows, :]

    def finish():
        y = y_s[...]
        v = v_s[...]
        mean = _dot01_right(y, hsum) * (1.0 / RWKV_N)
        d = y - mean
        var = _dot01_right(d * d, hsum) * (1.0 / RWKV_N)
        yn = d * lax.rsqrt(var + RWKV_LN_EPS) * lnw + lnb
        bonus = _dot01_right(r_s[...] * k_s[...] * rk, hsum) * v
        y_ref[...] = ((yn + bonus) * g_s[...]).astype(y_ref.dtype)

    return prepare, advance, finish


def _rwkv(x, mu, w2a2, w0, a0, g2, kk, ka, rk, lnw, lnb):
    tb = _mixer_block(x.shape[1])
    nc = tb // CHUNK
    stacked = lambda dt: ((nc * N_HEADS * CHUNK, 256), dt)
    scratch = ([((256, 256), F32), ((tb + 8, W_RWKV), F32)] + [((tb, 256), F32)] * 7
               + [stacked(BF16), stacked(F32), stacked(BF16), ((tb, 256), F32), ((tb, 256), BF16),
                  ((tb, 256), BF16), stacked(F32), ((nc * 8, 256), F32), ((tb, 256), F32)])
    return _mixer_call(_rwkv_body, "rwkv", x, (mu, w2a2, w0, a0, g2, kk, ka, rk, lnw, lnb), scratch, tb)


def _mla_prep_kernel(x_ref, qnw_ref, kvnw_ref, wqa_ref, wqb_ref, wk_ref, wv_ref, ea_ref, eb_ref,
                     c1_ref, s1_ref, q_ref, k_ref, v_ref):
    x = x_ref[...]
    qn = _rms(x[:, 0:256], qnw_ref[...]).astype(BF16)
    kvn = _rms(x[:, 256:384], kvnw_ref[...]).astype(BF16)
    kpe = x[:, 384:512].astype(BF16)
    c1 = c1_ref[...]
    s1 = s1_ref[...]
    qa = jnp.dot(qn, wqa_ref[...], preferred_element_type=F32)
    qb = jnp.dot(qn, wqb_ref[...], preferred_element_type=F32)
    kn = jnp.dot(kvn, wk_ref[...], preferred_element_type=F32)
    kp = (jnp.dot(kpe, ea_ref[...], preferred_element_type=F32) * c1
          + jnp.dot(kpe, eb_ref[...], preferred_element_type=F32) * s1)
    ones_lane = ((_iota((1, N_HEADS * MLA_SLOT), 1) & (MLA_SLOT - 1)) == MLA_V).astype(F32)
    v_ref[...] = (jnp.dot(kvn, wv_ref[...], preferred_element_type=F32) + ones_lane).astype(BF16)
    for h in range(N_HEADS):
        sl = slice(h * MLA_SLOT, (h + 1) * MLA_SLOT)
        q_ref[:, sl] = (qa[:, sl] * c1 + qb[:, sl] * s1).astype(BF16)
        k_ref[:, sl] = (kn[:, sl] + kp).astype(BF16)


def _mla_prep(x, qnw, kvnw, wqa, wqb, wk, wv, ea, eb, c1, s1):
    b, tp, _ = x.shape
    tm = _pick(tp, (640, 128, 64))
    full = lambda a: pl.BlockSpec(a.shape, lambda i, j: (0, 0))
    wide = N_HEADS * MLA_SLOT
    out = pl.BlockSpec((None, tm, wide), lambda i, j: (i, j, 0))
    tab = pl.BlockSpec((tm, MLA_SLOT), lambda i, j: (j, 0))
    return pl.pallas_call(
        _mla_prep_kernel,
        grid=(b, tp // tm),
        in_specs=[pl.BlockSpec((None, tm, W_MLA), lambda i, j: (i, j, 0))]
        + [full(a) for a in (qnw, kvnw, wqa, wqb, wk, wv, ea, eb)] + [tab, tab],
        out_specs=[out, out, out],
        out_shape=[jax.ShapeDtypeStruct((b, tp, wide), BF16)] * 3,
        compiler_params=_cparams("parallel", "parallel"),
        name="mla_prep",
    )(x, qnw, kvnw, wqa, wqb, wk, wv, ea, eb, c1, s1)


FLASH_HEADS = 4


def _flash_kernel(q_ref, k_ref, v_ref, o_ref, m_ref, acc_ref):
    qi = pl.program_id(2)
    t = q_ref.shape[0]
    m_ref[...] = jnp.full(m_ref.shape, NEG_INF, F32)
    acc_ref[...] = jnp.zeros(acc_ref.shape, F32)

    def block(j, diagonal):
        rows = pl.ds(pl.multiple_of(j * t, t), t)
        for h in range(FLASH_HEADS):
            sl = slice(h * MLA_SLOT, (h + 1) * MLA_SLOT)
            s = lax.dot_general(q_ref[:, sl], k_ref[rows, sl], (((1,), (1,)), ((), ())),
                                preferred_element_type=F32)
            if diagonal:
                s = jnp.where(_iota(s.shape, 0) >= _iota(s.shape, 1), s, NEG_INF)
            m_old = m_ref[h]
            m_new = jnp.maximum(m_old, jnp.max(s, axis=-1, keepdims=True))
            p = jnp.concatenate([jnp.exp2(s[:, c * LANE:(c + 1) * LANE] - m_new)
                                 for c in range(s.shape[1] // LANE)], axis=1).astype(BF16)
            acc_ref[h] = (jnp.exp2(m_old - m_new) * acc_ref[h]
                          + jnp.dot(p, v_ref[rows, sl], preferred_element_type=F32))
            m_ref[h] = m_new

    def full_block(j, carry):
        block(j, False)
        return carry

    lax.fori_loop(0, qi, full_block, 0)
    block(qi, True)
    for h in range(FLASH_HEADS):
        acc = acc_ref[h]
        o_ref[:, h * MLA_SLOT:(h + 1) * MLA_SLOT] = (acc / acc[:, MLA_V:MLA_V + 1]).astype(o_ref.dtype)


def _flash(q, k, v):
    b, tp, wide = q.shape
    t = _pick(tp, (640, 128, 64))
    w = FLASH_HEADS * MLA_SLOT
    qspec = pl.BlockSpec((None, t, w), lambda i, h, qi: (i, qi, h))
    kspec = pl.BlockSpec((None, tp, w), lambda i, h, qi: (i, 0, h))
    return pl.pallas_call(
        _flash_kernel,
        grid=(b, wide // w, tp // t),
        in_specs=[qspec, kspec, kspec],
        out_specs=qspec,
        out_shape=jax.ShapeDtypeStruct((b, tp, wide), BF16),
        scratch_shapes=[pltpu.VMEM((FLASH_HEADS, t, LANE), F32), pltpu.VMEM((FLASH_HEADS, t, MLA_SLOT), F32)],
        compiler_params=_cparams("parallel", "parallel", "arbitrary"),
        name="mla_flash",
    )(q, k, v)


def _router_kernel(h_ref, nw_ref, wr_ref, xn_ref, info_ref):
    xn = _rms(h_ref[...], nw_ref[...])
    xn_ref[...] = xn
    logits = jnp.dot(xn, wr_ref[...], preferred_element_type=F32, precision=lax.Precision.HIGHEST)
    lane = _iota(logits.shape, 1).astype(F32)
    valid = lane < N_EXPERTS
    l0 = jnp.where(valid, logits, NEG_INF)
    m1 = jnp.max(l0, axis=-1, keepdims=True)
    i1 = jnp.min(jnp.where(l0 == m1, lane, float(LANE)), axis=-1, keepdims=True)
    l1 = jnp.where(lane == i1, NEG_INF, l0)
    m2 = jnp.max(l1, axis=-1, keepdims=True)
    i2 = jnp.min(jnp.where(l1 == m2, lane, float(LANE)), axis=-1, keepdims=True)
    e2 = jnp.exp(m2 - m1)
    g1 = 1.0 / (1.0 + e2)
    g2 = e2 / (1.0 + e2)
    info = jnp.where(lane == 0, i1, 0.0)
    info = jnp.where(lane == 1, i2, info)
    info = jnp.where(lane == 2, g1, info)
    info = jnp.where(lane == 3, g2, info)
    info_ref[...] = info


def _router(h, nw, wr):
    n = h.shape[0]
    tm = _pick(n, (512, 256, 128, 64))
    return pl.pallas_call(
        _router_kernel,
        grid=(n // tm,),
        in_specs=[pl.BlockSpec((tm, D_MODEL), lambda i: (i, 0)), pl.BlockSpec(nw.shape, lambda i: (0, 0)),
                  pl.BlockSpec(wr.shape, lambda i: (0, 0))],
        out_specs=[pl.BlockSpec((tm, D_MODEL), lambda i: (i, 0)), pl.BlockSpec((tm, LANE), lambda i: (i, 0))],
        out_shape=[jax.ShapeDtypeStruct((n, D_MODEL), F32), jax.ShapeDtypeStruct((n, LANE), F32)],
        compiler_params=_cparams("parallel"),
        name="moe_router",
    )(h, nw, wr)


def _expert_kernel(be_ref, x_ref, gate_ref, wg_ref, wu_ref, wd_ref, out_ref, acc_ref):
    f = pl.program_id(1)

    @pl.when(f == 0)
    def _():
        acc_ref[...] = jnp.zeros(acc_ref.shape, F32)

    xb = x_ref[...].astype(BF16)
    a = jnp.dot(xb, wg_ref[...], preferred_element_type=F32)
    b = jnp.dot(xb, wu_ref[...], preferred_element_type=F32)
    acc_ref[...] += jnp.dot((_silu(a) * b).astype(BF16), wd_ref[...], preferred_element_type=F32)

    @pl.when(f == pl.num_programs(1) - 1)
    def _():
        out_ref[...] = acc_ref[...] * gate_ref[...]


def _experts(block_expert, x_rows, row_gate, wg, wu, wd, tm):
    cap = x_rows.shape[0]
    dff = wg.shape[2]
    tf = _pick(dff, (1792, 512, 256, 128))
    grid_spec = pltpu.PrefetchScalarGridSpec(
        num_scalar_prefetch=1,
        grid=(cap // tm, dff // tf),
        in_specs=[pl.BlockSpec((tm, D_MODEL), lambda i, f, be: (i, 0)),
                  pl.BlockSpec((tm, 1), lambda i, f, be: (i, 0)),
                  pl.BlockSpec((None, D_MODEL, tf), lambda i, f, be: (be[i], 0, f)),
                  pl.BlockSpec((None, D_MODEL, tf), lambda i, f, be: (be[i], 0, f)),
                  pl.BlockSpec((None, tf, D_MODEL), lambda i, f, be: (be[i], f, 0))],
        out_specs=pl.BlockSpec((tm, D_MODEL), lambda i, f, be: (i, 0)),
        scratch_shapes=[pltpu.VMEM((tm, D_MODEL), F32)],
    )
    return pl.pallas_call(
        _expert_kernel,
        grid_spec=grid_spec,
        out_shape=jax.ShapeDtypeStruct((cap, D_MODEL), F32),
        compiler_params=_cparams("parallel", "arbitrary"),
        name="moe_experts",
    )(block_expert, x_rows, row_gate, wg, wu, wd)


def _combine_kernel(h_ref, ya_ref, yb_ref, out_ref):
    out_ref[...] = h_ref[...] + (ya_ref[...] + yb_ref[...])


def _combine(h, y2):
    n = h.shape[0]
    tm = _pick(n, (1024, 512, 256, 128, 64))
    spec = pl.BlockSpec((tm, D_MODEL), lambda i: (i, 0))
    second = pl.BlockSpec((tm, D_MODEL), lambda i: (i + n // tm, 0))
    return pl.pallas_call(
        _combine_kernel, grid=(n // tm,), in_specs=[spec, spec, second], out_specs=spec,
        out_shape=jax.ShapeDtypeStruct((n, D_MODEL), F32),
        compiler_params=_cparams("parallel"), name="moe_combine",
    )(h, y2, y2)


def _sc_gather(table, idx):
    rows = idx.shape[0]
    d = table.shape[1]
    info = plsc.get_sparse_core_info()
    workers = info.num_cores * info.num_subcores
    assert rows % (workers * SC_CHUNK) == 0, (rows, workers)
    per_worker = rows // workers
    mesh = plsc.VectorSubcoreMesh(core_axis_name="c", subcore_axis_name="s")

    @functools.partial(
        pl.kernel, mesh=mesh, out_type=jax.ShapeDtypeStruct((rows, d), table.dtype),
        scratch_types=[pltpu.VMEM((SC_CHUNK,), jnp.int32), pltpu.VMEM((SC_CHUNK, d), table.dtype),
                       pltpu.SemaphoreType.DMA],
        name="sc_gather")
    def gather(table_hbm, idx_hbm, out_hbm, idx_v, rows_v, sem):
        base = (lax.axis_index("s") * info.num_cores + lax.axis_index("c")) * per_worker

        @pl.loop(0, per_worker // SC_CHUNK)
        def _(i):
            off = pl.multiple_of(base + i * SC_CHUNK, 8)
            pltpu.sync_copy(idx_hbm.at[pl.ds(off, SC_CHUNK)], idx_v)
            pltpu.async_copy(table_hbm.at[idx_v], rows_v, sem).wait()
            pltpu.sync_copy(rows_v, out_hbm.at[pl.ds(off, SC_CHUNK)])

    return gather(table, idx)


def _moe(h, nw, router, wg, wu, wd):
    n = h.shape[0]
    tm = _pick(n, (512, 64))
    wr = jnp.pad(router.astype(F32), ((0, 0), (0, LANE - N_EXPERTS)))
    xn, info = _router(h, nw, wr)
    expert = info[:, 0:2].astype(jnp.int32).reshape(-1)
    gate = info[:, 2:4].reshape(-1)
    n_assign = 2 * n
    order = jnp.argsort(expert)
    onehot = (expert[:, None] == jnp.arange(N_EXPERTS, dtype=jnp.int32)[None, :]).astype(jnp.int32)
    running = jnp.cumsum(onehot, axis=0)
    counts = running[-1]
    padded = (counts + tm - 1) // tm * tm
    pad_end = jnp.cumsum(padded)
    pad_start = pad_end - padded
    start = jnp.cumsum(counts) - counts
    n_blocks = -(-n_assign // tm) + N_EXPERTS
    cap = n_blocks * tm
    block_start = jnp.arange(n_blocks, dtype=jnp.int32) * tm
    block_expert = jnp.minimum(jnp.sum(block_start[:, None] >= pad_end[None, :], axis=1), N_EXPERTS - 1)
    block_expert = block_expert.astype(jnp.int32)
    rank = (block_start - pad_start[block_expert])[:, None] + jnp.arange(tm, dtype=jnp.int32)[None, :]
    valid = (rank < counts[block_expert][:, None]).reshape(cap)
    src = order[jnp.clip(start[block_expert][:, None] + rank, 0, n_assign - 1).reshape(cap)]
    row_token = jnp.where(valid, src // 2, 0)
    row_gate = jnp.where(valid, gate[src], 0.0)
    dest = jnp.sum(onehot * (pad_start[None, :] + running - 1), axis=1)
    x_rows = _sc_gather(xn, row_token)
    y_rows = _experts(block_expert, x_rows, row_gate[:, None], wg, wu, wd, tm)
    return _combine(h, _sc_gather(y_rows, dest.reshape(n, 2).T.reshape(n_assign)))


def _pad_cols(a, width):
    return jnp.pad(a, ((0, 0), (0, width - a.shape[1])))


def _row(a, width=None):
    a = a.reshape(1, -1).astype(F32)
    return a if width is None else _pad_cols(a, width)


def _rope_tables(tp):
    pos = jnp.arange(tp, dtype=F32)
    inv_freq = ROPE_THETA ** (-jnp.arange(0, MLA_ROPE, 2, dtype=F32) / MLA_ROPE)
    ang = pos[:, None] * inv_freq[None, :]
    cos, sin = jnp.cos(ang), jnp.sin(ang)
    ones = jnp.ones((tp, MLA_NOPE), F32)
    zeros = jnp.zeros((tp, MLA_SLOT - MLA_NOPE - MLA_ROPE), F32)
    c1 = jnp.concatenate([ones, cos, cos, zeros], axis=1)
    s1 = jnp.concatenate([0.0 * ones, -sin, sin, zeros], axis=1)
    return c1, s1


def _mla_weights(w_uq, w_ukv):
    half = MLA_ROPE // 2
    scale = (MLA_NOPE + MLA_ROPE) ** -0.5 * math.log2(math.e)
    zq =jnp.zeros((w_uq.shape[0], MLA_SLOT - MLA_NOPE - MLA_ROPE), F32)
    zn = jnp.zeros((w_uq.shape[0], MLA_NOPE), F32)
    zk = jnp.zeros((w_ukv.shape[0], MLA_SLOT - MLA_NOPE), F32)
    wqa, wqb, wk, wv = [], [], [], []
    for h in range(N_HEADS):
        q = w_uq[:, h * 96:(h + 1) * 96] * scale
        nope, x1, x2 = q[:, :MLA_NOPE], q[:, MLA_NOPE:MLA_NOPE + half], q[:, MLA_NOPE + half:]
        wqa += [nope, x1, x2, zq]
        wqb += [zn, x2, x1, zq]
        kv = w_ukv[:, h * 128:(h + 1) * 128]
        wk += [kv[:, :MLA_NOPE], zk]
        wv += [kv[:, MLA_NOPE:], zk]
    cat = lambda parts: jnp.concatenate(parts, axis=1).astype(BF16)
    ea = np.zeros((MLA_SLOT, MLA_SLOT), np.float32)
    eb = np.zeros((MLA_SLOT, MLA_SLOT), np.float32)
    for i in range(MLA_ROPE):
        ea[i, MLA_NOPE + i] = 1.0
        eb[(i + half) % MLA_ROPE, MLA_NOPE + i] = 1.0
    return cat(wqa), cat(wqb), cat(wk), cat(wv), jnp.asarray(ea, BF16), jnp.asarray(eb, BF16)


def _branch_weights(w_branch):
    wb = w_branch.astype(BF16)
    z = jnp.zeros((MLA_SLOT - MLA_V, D_MODEL), BF16)
    parts = []
    for h in range(N_HEADS):
        parts += [wb[1, h * MLA_V:(h + 1) * MLA_V], z]
    return wb[0], jnp.concatenate(parts, axis=0), wb[2], wb[3]


def _token_mixing(h, b, tp, p, l, tables):
    w_in = p["w_in"][l]
    col = lambda i, j: w_in[:, _OFF[i]:_OFF[j]]
    wg = jnp.concatenate([col(0, 3), col(4, 5), _pad_cols(col(3, 4), LANE)], axis=1).astype(BF16)
    wm = jnp.concatenate([col(5, 7), _pad_cols(col(7, 8), LANE)], axis=1).astype(BF16)
    wd = jnp.concatenate([col(8, 10), _pad_cols(col(10, 12), LANE)], axis=1).astype(BF16)
    wr = _pad_cols(col(12, 13), W_RWKV).astype(BF16)
    nw = _row(p["norm_mix"][l])
    xg, xm, xd, xr = _inproj(h, nw, wg, wm, wd, wr)
    shape3 = lambda a: a.reshape(b, tp, a.shape[1])

    gup = jnp.pad(p["gla_gate_up"][l], ((0, LANE - 16), (0, 0))).astype(BF16)
    y_gla = _gla(shape3(xg), gup, _row(p["gla_gate_bias"][l]), _row(jnp.tile(p["gla_norm"][l], N_HEADS)))

    wqa, wqb, wk, wv, ea, eb = _mla_weights(p["mla_w_uq"][l], p["mla_w_ukv"][l])
    q, k, v = _mla_prep(shape3(xm), _row(p["mla_q_norm"][l]), _row(p["mla_kv_norm"][l]),
                        wqa, wqb, wk, wv, ea, eb, *tables)
    y_mla = _flash(q, k, v)

    lanes4 = lambda a: jnp.pad(a.reshape(1, N_HEADS).astype(F32), ((0, 0), (N_HEADS, LANE - 2 * N_HEADS)))
    y_gdn = _gdn(shape3(xd), p["gdn_conv"][l].astype(F32), lanes4(p["gdn_a_log"][l]),
                 lanes4(p["gdn_dt_bias"][l]), _row(jnp.tile(p["gdn_norm"][l], N_HEADS)))

    w2a2 = jnp.zeros((LANE, 512), F32)
    w2a2 = w2a2.at[0:64, 0:256].set(p["rwkv_w2"][l]).at[64:128, 256:512].set(p["rwkv_a2"][l]).astype(BF16)
    g2 = jnp.pad(p["rwkv_g2"][l], ((0, 256 - 160), (0, 0))).astype(BF16)
    y_rwkv = _rwkv(shape3(xr), _row(p["rwkv_mu"][l], W_RWKV), w2a2, _row(p["rwkv_w0"][l]),
                   _row(p["rwkv_a0"][l]), g2, _row(p["rwkv_k_k"][l]), _row(p["rwkv_k_a"][l]),
                   _row(p["rwkv_r_k"][l]), _row(p["rwkv_ln_w"][l]), _row(p["rwkv_ln_b"][l]))

    flat = lambda a: a.reshape(b * tp, a.shape[2])
    ys = [flat(y_gla), flat(y_mla), flat(y_gdn), flat(y_rwkv)]
    return _merge(h, nw, col(13, 14).astype(BF16), ys, _branch_weights(p["w_branch"][l]),
                  p["w_out"][l].astype(BF16))


def kernel(x, meta_tokens, norm_mix, w_in, gla_gate_up, gla_gate_bias, gla_norm, mla_q_norm, mla_w_uq, mla_kv_norm, mla_w_ukv, gdn_conv, gdn_a_log, gdn_dt_bias, gdn_norm, rwkv_mu, rwkv_w0, rwkv_w2, rwkv_a0, rwkv_a2, rwkv_g2, rwkv_k_k, rwkv_k_a, rwkv_r_k, rwkv_ln_w, rwkv_ln_b, w_branch, w_out, norm_ffn, ffn_w_gate, ffn_w_up, ffn_w_down, moe_router, moe_w_gate, moe_w_up, moe_w_down, norm_final):
    p = dict(norm_mix=norm_mix, w_in=w_in, gla_gate_up=gla_gate_up, gla_gate_bias=gla_gate_bias,
             gla_norm=gla_norm, mla_q_norm=mla_q_norm, mla_w_uq=mla_w_uq, mla_kv_norm=mla_kv_norm,
             mla_w_ukv=mla_w_ukv, gdn_conv=gdn_conv, gdn_a_log=gdn_a_log, gdn_dt_bias=gdn_dt_bias,
             gdn_norm=gdn_norm, rwkv_mu=rwkv_mu, rwkv_w0=rwkv_w0, rwkv_w2=rwkv_w2, rwkv_a0=rwkv_a0,
             rwkv_a2=rwkv_a2, rwkv_g2=rwkv_g2, rwkv_k_k=rwkv_k_k, rwkv_k_a=rwkv_k_a, rwkv_r_k=rwkv_r_k,
             rwkv_ln_w=rwkv_ln_w, rwkv_ln_b=rwkv_ln_b, w_branch=w_branch, w_out=w_out)
    b, seq, d = x.shape
    t_real = N_META + seq
    tp = -(-t_real // 128) * 128
    meta = jnp.broadcast_to(meta_tokens[None].astype(x.dtype), (b, N_META, d))
    h = jnp.concatenate([meta, x, jnp.zeros((b, tp - t_real, d), x.dtype)], axis=1).reshape(b * tp, d)
    tables = _rope_tables(tp)
    depth = norm_mix.shape[0]
    for l in range(depth):
        h = _token_mixing(h, b, tp, p, l, tables)
        nw = _row(norm_ffn[l])
        if l % 2 == 0:
            h = _ffn(h, nw, ffn_w_gate[l // 2].astype(BF16), ffn_w_up[l // 2].astype(BF16),
                     ffn_w_down[l // 2].astype(BF16))
        else:
            h = _moe(h, nw, moe_router[l // 2], moe_w_gate[l // 2].astype(BF16),
                     moe_w_up[l // 2].astype(BF16), moe_w_down[l // 2].astype(BF16))
    out = _final_norm(h, _row(norm_final))
    return out.reshape(b, tp, d)[:, N_META:t_real]
```

```python
import functools
import math

import jax
import jax.numpy as jnp
import numpy as np
from jax import lax
from jax.experimental import pallas as pl
from jax.experimental.pallas import tpu as pltpu
from jax.experimental.pallas import tpu_sc as plsc

F32 = jnp.float32
BF16 = jnp.bfloat16

D_MODEL = 1024
N_META = 16
N_HEADS = 4
GLA_DK = 32
GLA_DV = 64
GLA_TAU = 16.0
MLA_NOPE = 64
MLA_ROPE = 32
MLA_V = 64
MLA_SLOT = 128
ROPE_THETA = 10000.0
GDN_DK = 64
GDN_CONV = 4
RWKV_N = 64
RWKV_LN_EPS = RWKV_N * 1e-5
CHUNK = 64
SUB = 16
PREP_GROUP = 5
MIX_BATCH = 2
N_EXPERTS = 8
SC_CHUNK = 64
NORM_EPS = 1e-6
L2_EPS = 1e-6
NEG_INF = -1e30
EXP_CLAMP = 80.0

LANE = 128
VMEM_LIMIT = 56 * 1024 * 1024

_OFF = np.cumsum([0, 128, 128, 256, 16, 256, 256, 128, 32, 768, 256, 4, 4, 1056, 4096]).tolist()
W_GLA, W_MLA, W_GDN, W_RWKV = 896, 512, 1152, 1152


def _cparams(*sem):
    return pltpu.CompilerParams(dimension_semantics=sem, vmem_limit_bytes=VMEM_LIMIT)


def _pick(n, prefs):
    for p in prefs:
        if n % p == 0:
            return p
    raise ValueError(f"no tile for {n}")


def _dot(a, b):
    return jnp.dot(a.astype(BF16), b.astype(BF16), preferred_element_type=F32)


def _dot_nt(a, b):
    return lax.dot_general(a.astype(BF16), b.astype(BF16), (((1,), (1,)), ((), ())),
                           preferred_element_type=F32)


def _dot_tn(a, b):
    return lax.dot_general(a.astype(BF16), b.astype(BF16), (((0,), (0,)), ((), ())),
                           preferred_element_type=F32)


def _split3(x):
    hi = x.astype(BF16)
    r1 = x - hi.astype(F32)
    mid = r1.astype(BF16)
    lo = (r1 - mid.astype(F32)).astype(BF16)
    return hi, mid, lo


def _dot01_left(m01, x):
    return sum(jnp.dot(m01, p, preferred_element_type=F32) for p in _split3(x))


def _dot01_right(x, m01):
    return sum(jnp.dot(p, m01, preferred_element_type=F32) for p in _split3(x))


def _iota(shape, dim):
    return lax.broadcasted_iota(jnp.int32, shape, dim)


def _div(x, w):
    return x >> int(math.log2(w))


def _ltri(n):
    return (_iota((n, n), 0) >= _iota((n, n), 1)).astype(BF16)


def _head_ones(n, w):
    return (_div(_iota((n, n), 0), w) == _div(_iota((n, n), 1), w)).astype(BF16)


def _lane_masks(width, w):
    lane = _div(_iota((1, width), 1), w)
    return [(lane == h).astype(F32) for h in range(width // w)]


def _stack(x, masks):
    return jnp.concatenate([x * m for m in masks], axis=0)


def _unstack(y, n):
    out = y[0:n]
    for h in range(1, y.shape[0] // n):
        out = out + y[h * n:(h + 1) * n]
    return out


def _rms(x, w):
    return x * lax.rsqrt(jnp.mean(x * x, axis=-1, keepdims=True) + NORM_EPS) * w


def _sigmoid(x):
    return 1.0 / (1.0 + jnp.exp(-x))


def _silu(x):
    return x * _sigmoid(x)


def _softplus(x):
    return jnp.maximum(x, 0.0) + jnp.log(1.0 + jnp.exp(-jnp.abs(x)))


def _neumann_inverse(xs, eye):
    n = xs[0].shape[0]
    ts = [eye + x for x in xs]
    ps = [_dot(x, x) for x in xs]
    for _ in range(int(math.log2(CHUNK)) - 2):
        yield
        both = [_dot(jnp.concatenate([t, p], axis=0), p) for t, p in zip(ts, ps)]
        ts = [t + tp[:n] for t, tp in zip(ts, both)]
        ps = [tp[n:] for tp in both]
    yield
    return [t + _dot(t, p) for t, p in zip(ts, ps)]


def _for_chunks(n, body, group):
    def trip(i, carry):
        body([i * group + g for g in range(group)])
        return carry

    if n >= group:
        lax.fori_loop(0, n // group, trip, 0)
    if n % group:
        body(list(range(n - n % group, n)))


def _inproj_kernel(h_ref, nw_ref, wg_ref, wm_ref, wd_ref, wr_ref, og_ref, om_ref, od_ref, or_ref):
    xb = _rms(h_ref[...], nw_ref[...]).astype(BF16)
    og_ref[...] = jnp.dot(xb, wg_ref[...], preferred_element_type=F32)
    om_ref[...] = jnp.dot(xb, wm_ref[...], preferred_element_type=F32)
    od_ref[...] = jnp.dot(xb, wd_ref[...], preferred_element_type=F32)
    or_ref[...] = jnp.dot(xb, wr_ref[...], preferred_element_type=F32)


def _inproj(h, nw, wg, wm, wd, wr):
    n = h.shape[0]
    tm = _pick(n, (512, 256, 128, 64))
    full = lambda a: pl.BlockSpec(a.shape, lambda i: (0, 0))
    row = lambda w: pl.BlockSpec((tm, w), lambda i: (i, 0))
    return pl.pallas_call(
        _inproj_kernel,
        grid=(n // tm,),
        in_specs=[row(D_MODEL), full(nw), full(wg), full(wm), full(wd), full(wr)],
        out_specs=[row(W_GLA), row(W_MLA), row(W_GDN), row(W_RWKV)],
        out_shape=[jax.ShapeDtypeStruct((n, w), F32) for w in (W_GLA, W_MLA, W_GDN, W_RWKV)],
        compiler_params=_cparams("parallel"),
        name="inproj",
    )(h, nw, wg, wm, wd, wr)


def _merge_kernel(h_ref, nw_ref, wgate_ref, yg_ref, ym_ref, yd_ref, yr_ref,
                  wbg_ref, wbm_ref, wbd_ref, wbr_ref, wout_ref, out_ref):
    x = h_ref[...]
    xb = _rms(x, nw_ref[...]).astype(BF16)
    acc = jnp.zeros(x.shape, F32)
    branches = ((yg_ref, wbg_ref), (ym_ref, wbm_ref), (yd_ref, wbd_ref), (yr_ref, wbr_ref))
    for i, (y_ref, wb_ref) in enumerate(branches):
        logits = jnp.dot(xb, wgate_ref[:, i * D_MODEL:(i + 1) * D_MODEL], preferred_element_type=F32)
        proj = jnp.dot(y_ref[...], wb_ref[...], preferred_element_type=F32)
        acc = acc + _sigmoid(logits) * proj
    out_ref[...] = x + jnp.dot(acc.astype(BF16), wout_ref[...], preferred_element_type=F32)


def _merge(h, nw, wgate, ys, wbs, wout):
    n = h.shape[0]
    tm = _pick(n, (512, 256, 128, 64))
    full = lambda a: pl.BlockSpec(a.shape, lambda i: (0, 0))
    row = lambda w: pl.BlockSpec((tm, w), lambda i: (i, 0))
    return pl.pallas_call(
        _merge_kernel,
        grid=(n // tm,),
        in_specs=[row(D_MODEL), full(nw), full(wgate)] + [row(y.shape[1]) for y in ys]
        + [full(w) for w in wbs] + [full(wout)],
        out_specs=row(D_MODEL),
        out_shape=jax.ShapeDtypeStruct((n, D_MODEL), F32),
        compiler_params=_cparams("parallel"),
        name="merge",
    )(h, nw, wgate, *ys, *wbs, wout)


def _ffn_kernel(h_ref, nw_ref, wg_ref, wu_ref, wd_ref, out_ref, xb_ref, acc_ref):
    f = pl.program_id(1)

    @pl.when(f == 0)
    def _():
        xb_ref[...] = _rms(h_ref[...], nw_ref[...]).astype(BF16)
        acc_ref[...] = jnp.zeros(acc_ref.shape, F32)

    xb = xb_ref[...]
    a = jnp.dot(xb, wg_ref[...], preferred_element_type=F32)
    b = jnp.dot(xb, wu_ref[...], preferred_element_type=F32)
    acc_ref[...] += jnp.dot((_silu(a) * b).astype(BF16), wd_ref[...], preferred_element_type=F32)

    @pl.when(f == pl.num_programs(1) - 1)
    def _():
        out_ref[...] = h_ref[...] + acc_ref[...]


def _ffn(h, nw, wg, wu, wd):
    n = h.shape[0]
    dff = wg.shape[1]
    tm = _pick(n, (512, 256, 128, 64))
    tf = _pick(dff, (1408, 512, 256, 128))
    return pl.pallas_call(
        _ffn_kernel,
        grid=(n // tm, dff // tf),
        in_specs=[pl.BlockSpec((tm, D_MODEL), lambda i, f: (i, 0)),
                  pl.BlockSpec(nw.shape, lambda i, f: (0, 0)),
                  pl.BlockSpec((D_MODEL, tf), lambda i, f: (0, f)),
                  pl.BlockSpec((D_MODEL, tf), lambda i, f: (0, f)),
                  pl.BlockSpec((tf, D_MODEL), lambda i, f: (f, 0))],
        out_specs=pl.BlockSpec((tm, D_MODEL), lambda i, f: (i, 0)),
        out_shape=jax.ShapeDtypeStruct((n, D_MODEL), F32),
        scratch_shapes=[pltpu.VMEM((tm, D_MODEL), BF16), pltpu.VMEM((tm, D_MODEL), F32)],
        compiler_params=_cparams("parallel", "arbitrary"),
        name="ffn",
    )(h, nw, wg, wu, wd)


def _final_kernel(nw_ref, *refs):
    out_ref = refs[-1]
    x = refs[0][...]
    for extra in refs[1:-1]:
        x = x + extra[...]
    out_ref[...] = _rms(x, nw_ref[...])


def _final(h, nw, b, tp, seq, y2=None):
    tq = _pick(seq, (1024, 512, 256, 128, 64))
    n = b * tp
    rows = lambda shift: pl.BlockSpec((pl.Element(tq), pl.Element(D_MODEL)),
                                      lambda i, j: (pl.multiple_of(shift + i * tp + N_META + j * tq, 8), 0))
    operands = [h] if y2 is None else [h, y2, y2]
    shifts = [0] if y2 is None else [0, 0, n]
    return pl.pallas_call(
        _final_kernel,
        grid=(b, seq // tq),
        in_specs=[pl.BlockSpec(nw.shape, lambda i, j: (0, 0))] + [rows(s) for s in shifts],
        out_specs=pl.BlockSpec((None, tq, D_MODEL), lambda i, j: (i, j, 0)),
        out_shape=jax.ShapeDtypeStruct((b, seq, D_MODEL), F32),
        compiler_params=_cparams("parallel", "parallel"),
        name="final_norm",
    )(nw, *operands)


_DONE = object()


def _interleave(generators):
    live = list(generators)
    while live:
        live = [g for g in live if next(g, _DONE) is not _DONE]


def _mixer_kernel(body, n_params, x_ref, *refs):
    params, y_ref, scratch = refs[:n_params], refs[n_params], refs[n_params + 1:]
    parts = [body(x_ref.at[b], *params, y_ref.at[b], *[s.at[b] for s in scratch]) for b in range(x_ref.shape[0])]
    n_chunks = x_ref.shape[1] // CHUNK
    if n_chunks <= PREP_GROUP:
        pending = []
        for preprocess, prepare, _, _ in parts:
            _interleave([preprocess()] + pending)
            pending = [prepare(list(range(n_chunks)))]
        _interleave(pending)
    else:
        for preprocess, _, _, _ in parts:
            _interleave([preprocess()])
        _for_chunks(n_chunks, lambda cis: [_interleave([prepare(cis)]) for _, prepare, _, _ in parts], PREP_GROUP)
    _for_chunks(n_chunks, lambda cis: [_interleave([advance(ci) for _, _, advance, _ in parts]) for ci in cis], 1)
    for _, _, _, finish in parts:
        finish()


def _mixer_block(tp):
    return _pick(tp, (320, 128, 64))


def _mixer_call(body, name, x, params, scratch, tb):
    b, tp, width = x.shape
    nb = MIX_BATCH if b % MIX_BATCH == 0 else 1
    full = lambda a: pl.BlockSpec(a.shape, lambda i, j: (0, 0))
    return pl.pallas_call(
        functools.partial(_mixer_kernel, body, len(params)),
        grid=(b // nb, tp // tb),
        in_specs=[pl.BlockSpec((nb, tb, width), lambda i, j: (i, j, 0))] + [full(a) for a in params],
        out_specs=pl.BlockSpec((nb, tb, 256), lambda i, j: (i, j, 0)),
        out_shape=jax.ShapeDtypeStruct((b, tp, 256), BF16),
        scratch_shapes=[pltpu.VMEM((nb,) + shape, dtype) for shape, dtype in scratch],
        compiler_params=_cparams("parallel", "arbitrary"),
        name=name,
    )(x, *params)


def _gla_body(x_ref, gup_ref, gb_ref, nw_ref, y_ref, st_ref, la_ref, o_s, qg_s, kv_s, gt_s):
    @pl.when(pl.program_id(1) == 0)
    def _():
        st_ref[...] = jnp.zeros(st_ref.shape, F32)

    def preprocess():
        z = _dot(x_ref[:, 768:896], gup_ref[...]) + gb_ref[...]
        yield
        la_ref[...] = -_softplus(-z) * (1.0 / GLA_TAU)

    ltri = _ltri(CHUNK)
    qmasks = _lane_masks(N_HEADS * GLA_DK, GLA_DK)
    vmasks = _lane_masks(N_HEADS * GLA_DV, GLA_DV)
    bd = (_div(_iota((256, 128), 0), GLA_DV) == _div(_iota((256, 128), 1), GLA_DK)).astype(F32)
    hsum = _head_ones(N_HEADS * GLA_DV, GLA_DV)
    nw = nw_ref[...]

    n4 = N_HEADS * GLA_DV

    def prepare(cis):
        n = range(len(cis))
        rows = [pl.ds(pl.multiple_of(ci * CHUNK, CHUNK), CHUNK) for ci in cis]
        g = [_dot01_left(ltri, la_ref[r, :]) for r in rows]
        q = [x_ref[r, 0:128] * (GLA_DK ** -0.5) for r in rows]
        k = [x_ref[r, 128:256] for r in rows]
        v = [x_ref[r, 256:512] for r in rows]
        intra = [[] for _ in n]
        for s in range(CHUNK // SUB):
            lo, hi = s * SUB, (s + 1) * SUB
            sc = []
            for i in n:
                gs = jnp.zeros((1, 128), F32) if s == 0 else g[i][lo - 1:lo]
                qs = q[i][lo:hi] * jnp.exp(g[i][lo:hi] - gs)
                kt = k[i][:hi] * jnp.exp(jnp.minimum(gs - g[i][:hi], EXP_CLAMP))
                sc.append(_dot_nt(_stack(qs, qmasks), kt))
            causal = _iota(sc[0].shape, 1) <= lo + (_iota(sc[0].shape, 0) & (SUB - 1))
            yield
            p = [_dot(jnp.where(causal, sc[i], 0.0), v[i][:hi]) for i in n]
            for i in n:
                intra[i].append(sum(p[i][h * SUB:(h + 1) * SUB] * vmasks[h] for h in range(N_HEADS)))
            yield
        for i in n:
            g_last = g[i][CHUNK - 1:CHUNK]
            o_s[rows[i], :] = jnp.concatenate(intra[i], axis=0)
            qg_s[rows[i], :] = (q[i] * jnp.exp(g[i])).astype(BF16)
            kv_s[pl.ds(pl.multiple_of(cis[i] * n4, n4), n4), :] = bd * _dot_tn(v[i], k[i] * jnp.exp(g_last - g[i]))
            gt_s[pl.ds(pl.multiple_of(cis[i] * 8, 8), 8), :] = jnp.broadcast_to(jnp.exp(g_last), (8, 128))

    def advance(ci):
        rows = pl.ds(pl.multiple_of(ci * CHUNK, CHUNK), CHUNK)
        st = st_ref[...]
        inter = _dot_nt(qg_s[rows, :], st)
        yield
        o_s[rows, :] += inter
        st_ref[...] = (st * gt_s[pl.ds(pl.multiple_of(ci * 8, 8), 1), :]
                       + kv_s[pl.ds(pl.multiple_of(ci * n4, n4), n4), :])

    def finish():
        o = o_s[...]
        ms = _dot01_right(o * o, hsum) * (1.0 / GLA_DV)
        y_ref[...] = (o * lax.rsqrt(ms + NORM_EPS) * nw * _silu(x_ref[:, 512:768])).astype(y_ref.dtype)

    return preprocess, prepare, advance, finish


def _gla(x, gup, gb, nw):
    tb = _mixer_block(x.shape[1])
    nc = tb // CHUNK
    scratch = [((256, 128), F32), ((tb, 128), F32), ((tb, 256), F32), ((tb, 128), BF16),
               ((nc * N_HEADS * GLA_DV, 128), F32), ((nc * 8, 128), F32)]
    return _mixer_call(_gla_body, "gla", x, (gup, gb, nw), scratch, tb)


def _gdn_body(x_ref, cw_ref, alog_ref, dtb_ref, nw_ref, y_ref,
              s_ref, xp_ref, q_ref, k_ref, v_ref, beta_ref, gd_ref,
              u_s, w_s, attn_s, qd_s, ke_s, gt_s, o_s):
    tb = x_ref.shape[0]
    first = pl.program_id(1) == 0

    @pl.when(first)
    def _():
        s_ref[...] = jnp.zeros(s_ref.shape, F32)
        xp_ref[0:8, :] = jnp.zeros((8, 768), F32)

    @pl.when(jnp.logical_not(first))
    def _():
        xp_ref[0:8, :] = xp_ref[tb:tb + 8, :]

    hsum = _head_ones(256, GDN_DK)

    def preprocess():
        xp_ref[8:tb + 8, :] = x_ref[:, 0:768]
        yield
        conv = cw_ref[0:1, :] * xp_ref[8 - (GDN_CONV - 1):8 - (GDN_CONV - 1) + tb, :]
        for j in range(1, GDN_CONV):
            yield
            conv = conv + cw_ref[j:j + 1, :] * xp_ref[8 - (GDN_CONV - 1) + j:8 - (GDN_CONV - 1) + j + tb, :]
        yield
        c = _silu(conv)
        q = c[:, 0:256]
        k = c[:, 256:512]
        yield
        q_ref[...] = q * lax.rsqrt(_dot01_right(q * q, hsum) + L2_EPS) * (GDN_DK ** -0.5)
        yield
        k_ref[...] = k * lax.rsqrt(_dot01_right(k * k, hsum) + L2_EPS)
        v_ref[...] = c[:, 512:768]
        yield
        gates = x_ref[:, 1024:1152]
        beta_ref[...] = _sigmoid(gates)
        gd_ref[...] = -jnp.exp(alog_ref[...]) * _softplus(gates + dtb_ref[...])

    ltri = _ltri(CHUNK)
    masks = _lane_masks(256, GDN_DK)
    expand = lambda off: (_iota((128, 256), 0) == _div(_iota((128, 256), 1), GDN_DK) + off).astype(BF16)
    exp_beta, exp_g = expand(0), expand(N_HEADS)
    r = _iota((256, 256), 0)
    cidx = _iota((256, 256), 1)
    same = _div(r, CHUNK) == _div(cidx, CHUNK)
    incl = jnp.logical_and(same, r >= cidx)
    strict = jnp.logical_and(same, r > cidx)
    eye = (r == cidx).astype(F32)
    bd = same.astype(F32)
    nw = nw_ref[...]

    def prepare(cis):
        n = range(len(cis))
        rows = [pl.ds(pl.multiple_of(ci * CHUNK, CHUNK), CHUNK) for ci in cis]
        srows = [pl.ds(pl.multiple_of(ci * (N_HEADS * CHUNK), N_HEADS * CHUNK), N_HEADS * CHUNK) for ci in cis]
        q = [q_ref[r, :] for r in rows]
        k = [k_ref[r, :] for r in rows]
        v = [v_ref[r, :] for r in rows]
        bexp = [_dot01_right(beta_ref[r, :], exp_beta) for r in rows]
        gcum = [_dot01_left(ltri, gd_ref[r, :]) for r in rows]
        yield
        gexp = [_dot01_right(g, exp_g) for g in gcum]
        kst = [_stack(x, masks) for x in k]
        kb = [k[i] * bexp[i] for i in n]
        n4 = N_HEADS * CHUNK
        yield
        both = [_dot_nt(jnp.concatenate([_stack(kb[i], masks), _stack(q[i], masks)], axis=0), kst[i]) for i in n]
        kk = [x[:n4] for x in both]
        qk = [x[n4:] for x in both]
        yield
        dec = []
        for g in gexp:
            gcol = jnp.sum(_stack(g, masks), axis=1, keepdims=True) * (1.0 / GDN_DK)
            grow = jnp.sum(eye * gcol, axis=0, keepdims=True)
            dec.append(jnp.exp(jnp.minimum(gcol - grow, 0.0)))
        for i in n:
            attn_s[srows[i], :] = jnp.where(incl, qk[i] * dec[i], 0.0).astype(BF16)
        yield
        t = yield from _neumann_inverse([-jnp.where(strict, kk[i] * dec[i], 0.0) for i in n], eye)
        yield
        u = [_dot(t[i], _stack(v[i] * bexp[i], masks)) for i in n]
        w = [_dot(t[i], _stack(kb[i] * jnp.exp(gexp[i]), masks)) for i in n]
        yield
        for i in n:
            u_s[rows[i], :] = _unstack(u[i], CHUNK)
            w_s[rows[i], :] = _unstack(w[i], CHUNK).astype(BF16)
            g_last = gexp[i][CHUNK - 1:CHUNK]
            qd_s[rows[i], :] = (q[i] * jnp.exp(gexp[i])).astype(BF16)
            ke_s[rows[i], :] = (k[i] * jnp.exp(g_last - gexp[i])).astype(BF16)
            gt_s[pl.ds(pl.multiple_of(cis[i] * 8, 8), 8), :] = jnp.broadcast_to(jnp.exp(g_last), (8, 256))

    def advance(ci):
        rows = pl.ds(pl.multiple_of(ci * CHUNK, CHUNK), CHUNK)
        srows = pl.ds(pl.multiple_of(ci * (N_HEADS * CHUNK), N_HEADS * CHUNK), N_HEADS * CHUNK)
        s = s_ref[...]
        sb = s.astype(BF16)
        ws = jnp.dot(w_s[rows, :], sb, preferred_element_type=F32)
        qs = jnp.dot(qd_s[rows, :], sb, preferred_element_type=F32)
        yield
        v_new = u_s[rows, :] - ws
        av = _dot(attn_s[srows, :], _stack(v_new, masks))
        kv = _dot_tn(ke_s[rows, :], v_new)
        yield
        o_s[rows, :] = qs + _unstack(av, CHUNK)
        s_ref[...] = s * gt_s[pl.ds(pl.multiple_of(ci * 8, 8), 1), :] + bd * kv

    def finish():
        o = o_s[...]
        ms = _dot01_right(o * o, hsum) * (1.0 / GDN_DK)
        y_ref[...] = (o * lax.rsqrt(ms + NORM_EPS) * nw * _silu(x_ref[:, 768:1024])).astype(y_ref.dtype)

    return preprocess, prepare, advance, finish


def _gdn(x, cw, alog, dtb, nw):
    tb = _mixer_block(x.shape[1])
    nc = tb // CHUNK
    scratch = [((256, 256), F32), ((tb + 8, 768), F32),
               ((tb, 256), F32), ((tb, 256), F32), ((tb, 256), F32), ((tb, 128), F32), ((tb, 128), F32),
               ((tb, 256), F32), ((tb, 256), BF16), ((nc * N_HEADS * CHUNK, 256), BF16),
               ((tb, 256), BF16), ((tb, 256), BF16), ((nc * 8, 256), F32), ((tb, 256), F32)]
    return _mixer_call(_gdn_body, "gdn", x, (cw, alog, dtb, nw), scratch, tb)


def _rwkv_body(x_ref, mu_ref, w2a2_ref, w0_ref, a0_ref, g2_ref, kk_ref, ka_ref, rk_ref,
               lnw_ref, lnb_ref, y_ref,
               s_ref, xp_ref, r_s, k_s, v_s, kk_s, b_s, lw_s, g_s,
               at_s, z_s, arb_s, yv_s, rt_s, be_s, vk_s, gt_s, y_s):
    tb = x_ref.shape[0]
    first = pl.program_id(1) == 0

    @pl.when(first)
    def _():
        s_ref[...] = jnp.zeros(s_ref.shape, F32)
        xp_ref[0:8, :] = jnp.zeros((8, W_RWKV), F32)

    @pl.when(jnp.logical_not(first))
    def _():
        xp_ref[0:8, :] = xp_ref[tb:tb + 8, :]

    hsum = _head_ones(256, RWKV_N)

    def preprocess():
        x = x_ref[...]
        xp_ref[8:tb + 8, :] = x
        yield
        z = x + (xp_ref[7:tb + 7, :] - x) * mu_ref[...]
        r = z[:, 0:256]
        k = z[:, 256:512]
        wa = z[:, 768:896]
        wa = jnp.where(_iota(wa.shape, 1) < 64, jnp.tanh(wa), wa)
        yield
        pre = _dot(wa, w2a2_ref[...])
        yield
        w_log = -_softplus(-(w0_ref[...] + pre[:, 0:256])) - 0.5
        lw_s[...] = -jnp.exp(w_log)
        yield
        a = _sigmoid(a0_ref[...] + pre[:, 256:512])
        kkv = k * kk_ref[...]
        yield
        kkn = kkv * lax.rsqrt(_dot01_right(kkv * kkv, hsum) + L2_EPS)
        yield
        r_s[...] = r
        k_s[...] = k * (1.0 + (a - 1.0) * ka_ref[...])
        v_s[...] = z[:, 512:768]
        kk_s[...] = kkn
        b_s[...] = kkn * a
        yield
        g_s[...] = _dot(_sigmoid(z[:, 896:1152]), g2_ref[...])

    ltri = _ltri(CHUNK)
    masks = _lane_masks(256, RWKV_N)
    rr = _iota((256, 256), 0)
    cc = _iota((256, 256), 1)
    same = _div(rr, CHUNK) == _div(cc, CHUNK)
    incl = jnp.logical_and(same, rr >= cc)
    strict = jnp.logical_and(same, rr > cc)
    eye = (rr == cc).astype(F32)
    bd = same.astype(F32)
    rk = rk_ref[...]
    lnw = lnw_ref[...]
    lnb = lnb_ref[...]

    n4 = N_HEADS * CHUNK

    def prepare(cis):
        n = range(len(cis))
        rows = [pl.ds(pl.multiple_of(ci * CHUNK, CHUNK), CHUNK) for ci in cis]
        srows = [pl.ds(pl.multiple_of(ci * n4, n4), n4) for ci in cis]
        lw = [lw_s[r, :] for r in rows]
        gl = [_dot01_left(ltri, x) for x in lw]
        v = [v_s[r, :] for r in rows]
        k = [k_s[r, :] for r in rows]
        b = [b_s[r, :] for r in rows]
        yield
        e_neg = [jnp.exp(-g) for g in gl]
        a_st = [_stack(-kk_s[rows[i], :] * jnp.exp(gl[i] - lw[i]), masks) for i in n]
        r_t = [r_s[rows[i], :] * jnp.exp(gl[i]) for i in n]
        yield
        amat = []
        for i in n:
            lhs = jnp.concatenate([a_st[i], _stack(r_t[i], masks)], axis=0)
            rhs = jnp.concatenate([_stack(b[i] * e_neg[i], masks), _stack(k[i] * e_neg[i], masks)], axis=0)
            amat.append(_dot_nt(lhs, rhs))
        yield
        vst = [_stack(x, masks) for x in v]
        both = [_dot(jnp.concatenate([jnp.where(strict, amat[i][0:n4, n4:], 0.0),
                                      jnp.where(incl, amat[i][n4:, n4:], 0.0)], axis=0), vst[i]) for i in n]
        av = [x[:n4] for x in both]
        yv = [x[n4:] for x in both]
        yield
        t = yield from _neumann_inverse([jnp.where(strict, m[0:n4, 0:n4], 0.0) for m in amat], eye)
        yield
        z = [_dot(t[i], av[i]) for i in n]
        at = [_dot(t[i], a_st[i]) for i in n]
        yield
        for i in n:
            g_last = gl[i][CHUNK - 1:CHUNK]
            e_end = jnp.exp(g_last - gl[i])
            z_s[srows[i], :] = z[i]
            at_s[srows[i], :] = at[i].astype(BF16)
            arb_s[srows[i], :] = jnp.where(incl, amat[i][n4:, 0:n4], 0.0).astype(BF16)
            yv_s[rows[i], :] = _unstack(yv[i], CHUNK)
            rt_s[rows[i], :] = r_t[i].astype(BF16)
            be_s[rows[i], :] = (b[i] * e_end).astype(BF16)
            vk_s[srows[i], :] = bd * _dot_tn(v[i], k[i] * e_end)
            gt_s[pl.ds(pl.multiple_of(cis[i] * 8, 8), 8), :] = jnp.broadcast_to(jnp.exp(g_last), (8, 256))

    def advance(ci):
        rows = pl.ds(pl.multiple_of(ci * CHUNK, CHUNK), CHUNK)
        srows = pl.ds(pl.multiple_of(ci * n4, n4), n4)
        s = s_ref[...]
        sb = s.astype(BF16)
        u_st = _dot_nt(at_s[srows, :], sb) + z_s[srows, :]
        rs = _dot_nt(rt_s[rows, :], sb)
        yield
        au = _dot(arb_s[srows, :], u_st)
        ub = _dot_tn(_unstack(u_st, CHUNK), be_s[rows, :])
        yield
        y_s[rows, :] = rs + _unstack(au, CHUNK) + yv_s[rows, :]
        s_ref[...] = s * gt_s[pl.ds(pl.multiple_of(ci * 8, 8), 1), :] + bd * ub + vk_s[srows, :]

    def finish():
        y = y_s[...]
        v = v_s[...]
        mean = _dot01_right(y, hsum) * (1.0 / RWKV_N)
        d = y - mean
        var = _dot01_right(d * d, hsum) * (1.0 / RWKV_N)
        yn = d * lax.rsqrt(var + RWKV_LN_EPS) * lnw + lnb
        bonus = _dot01_right(r_s[...] * k_s[...] * rk, hsum) * v
        y_ref[...] = ((yn + bonus) * g_s[...]).astype(y_ref.dtype)

    return preprocess, prepare, advance, finish


def _rwkv(x, mu, w2a2, w0, a0, g2, kk, ka, rk, lnw, lnb):
    tb = _mixer_block(x.shape[1])
    nc = tb // CHUNK
    stacked = lambda dt: ((nc * N_HEADS * CHUNK, 256), dt)
    scratch = ([((256, 256), F32), ((tb + 8, W_RWKV), F32)] + [((tb, 256), F32)] * 7
               + [stacked(BF16), stacked(F32), stacked(BF16), ((tb, 256), F32), ((tb, 256), BF16),
                  ((tb, 256), BF16), stacked(F32), ((nc * 8, 256), F32), ((tb, 256), F32)])
    return _mixer_call(_rwkv_body, "rwkv", x, (mu, w2a2, w0, a0, g2, kk, ka, rk, lnw, lnb), scratch, tb)


def _mla_prep_kernel(x_ref, qnw_ref, kvnw_ref, wqa_ref, wqb_ref, wk_ref, wv_ref, ea_ref, eb_ref,
                     c1_ref, s1_ref, q_ref, k_ref, v_ref):
    x = x_ref[...]
    qn = _rms(x[:, 0:256], qnw_ref[...]).astype(BF16)
    kvn = _rms(x[:, 256:384], kvnw_ref[...]).astype(BF16)
    kpe = x[:, 384:512].astype(BF16)
    c1 = c1_ref[...]
    s1 = s1_ref[...]
    qa = jnp.dot(qn, wqa_ref[...], preferred_element_type=F32)
    qb = jnp.dot(qn, wqb_ref[...], preferred_element_type=F32)
    kn = jnp.dot(kvn, wk_ref[...], preferred_element_type=F32)
    kp = (jnp.dot(kpe, ea_ref[...], preferred_element_type=F32) * c1
          + jnp.dot(kpe, eb_ref[...], preferred_element_type=F32) * s1)
    ones_lane = ((_iota((1, N_HEADS * MLA_SLOT), 1) & (MLA_SLOT - 1)) == MLA_V).astype(F32)
    v_ref[...] = (jnp.dot(kvn, wv_ref[...], preferred_element_type=F32) + ones_lane).astype(BF16)
    for h in range(N_HEADS):
        sl = slice(h * MLA_SLOT, (h + 1) * MLA_SLOT)
        q_ref[:, sl] = (qa[:, sl] * c1 + qb[:, sl] * s1).astype(BF16)
        k_ref[:, sl] = (kn[:, sl] + kp).astype(BF16)


def _mla_prep(x, qnw, kvnw, wqa, wqb, wk, wv, ea, eb, c1, s1):
    b, tp, _ = x.shape
    tm = _pick(tp, (640, 128, 64))
    full = lambda a: pl.BlockSpec(a.shape, lambda i, j: (0, 0))
    wide = N_HEADS * MLA_SLOT
    out = pl.BlockSpec((None, tm, wide), lambda i, j: (i, j, 0))
    tab = pl.BlockSpec((tm, MLA_SLOT), lambda i, j: (j, 0))
    return pl.pallas_call(
        _mla_prep_kernel,
        grid=(b, tp // tm),
        in_specs=[pl.BlockSpec((None, tm, W_MLA), lambda i, j: (i, j, 0))]
        + [full(a) for a in (qnw, kvnw, wqa, wqb, wk, wv, ea, eb)] + [tab, tab],
        out_specs=[out, out, out],
        out_shape=[jax.ShapeDtypeStruct((b, tp, wide), BF16)] * 3,
        compiler_params=_cparams("parallel", "parallel"),
        name="mla_prep",
    )(x, qnw, kvnw, wqa, wqb, wk, wv, ea, eb, c1, s1)


FLASH_HEADS = 4


def _flash_kernel(q_ref, k_ref, v_ref, o_ref, m_ref, acc_ref):
    qi = pl.program_id(2)
    t = q_ref.shape[0]
    m_ref[...] = jnp.full(m_ref.shape, NEG_INF, F32)
    acc_ref[...] = jnp.zeros(acc_ref.shape, F32)

    def block(j, diagonal):
        rows = pl.ds(pl.multiple_of(j * t, t), t)
        for h in range(FLASH_HEADS):
            sl = slice(h * MLA_SLOT, (h + 1) * MLA_SLOT)
            s = lax.dot_general(q_ref[:, sl], k_ref[rows, sl], (((1,), (1,)), ((), ())),
                                preferred_element_type=F32)
            if diagonal:
                s = jnp.where(_iota(s.shape, 0) >= _iota(s.shape, 1), s, NEG_INF)
            m_old = m_ref[h]
            m_new = jnp.maximum(m_old, jnp.max(s, axis=-1, keepdims=True))
            p = jnp.concatenate([jnp.exp2(s[:, c * LANE:(c + 1) * LANE] - m_new)
                                 for c in range(s.shape[1] // LANE)], axis=1).astype(BF16)
            acc_ref[h] = (jnp.exp2(m_old - m_new) * acc_ref[h]
                          + jnp.dot(p, v_ref[rows, sl], preferred_element_type=F32))
            m_ref[h] = m_new

    def full_block(j, carry):
        block(j, False)
        return carry

    lax.fori_loop(0, qi, full_block, 0)
    block(qi, True)
    for h in range(FLASH_HEADS):
        acc = acc_ref[h]
        o_ref[:, h * MLA_SLOT:(h + 1) * MLA_SLOT] = (acc / acc[:, MLA_V:MLA_V + 1]).astype(o_ref.dtype)


def _flash(q, k, v):
    b, tp, wide = q.shape
    t = _pick(tp, (640, 128, 64))
    w = FLASH_HEADS * MLA_SLOT
    qspec = pl.BlockSpec((None, t, w), lambda i, h, qi: (i, qi, h))
    kspec = pl.BlockSpec((None, tp, w), lambda i, h, qi: (i, 0, h))
    return pl.pallas_call(
        _flash_kernel,
        grid=(b, wide // w, tp // t),
        in_specs=[qspec, kspec, kspec],
        out_specs=qspec,
        out_shape=jax.ShapeDtypeStruct((b, tp, wide), BF16),
        scratch_shapes=[pltpu.VMEM((FLASH_HEADS, t, LANE), F32), pltpu.VMEM((FLASH_HEADS, t, MLA_SLOT), F32)],
        compiler_params=_cparams("parallel", "parallel", "arbitrary"),
        name="mla_flash",
    )(q, k, v)


def _router_kernel(h_ref, nw_ref, wr_ref, xn_ref, info_ref):
    xn = _rms(h_ref[...], nw_ref[...])
    xn_ref[...] = xn
    logits = jnp.dot(xn, wr_ref[...], preferred_element_type=F32, precision=lax.Precision.HIGHEST)
    lane = _iota(logits.shape, 1).astype(F32)
    valid = lane < N_EXPERTS
    l0 = jnp.where(valid, logits, NEG_INF)
    m1 = jnp.max(l0, axis=-1, keepdims=True)
    i1 = jnp.min(jnp.where(l0 == m1, lane, float(LANE)), axis=-1, keepdims=True)
    l1 = jnp.where(lane == i1, NEG_INF, l0)
    m2 = jnp.max(l1, axis=-1, keepdims=True)
    i2 = jnp.min(jnp.where(l1 == m2, lane, float(LANE)), axis=-1, keepdims=True)
    e2 = jnp.exp(m2 - m1)
    g1 = 1.0 / (1.0 + e2)
    g2 = e2 / (1.0 + e2)
    info = jnp.where(lane == 0, i1, 0.0)
    info = jnp.where(lane == 1, i2, info)
    info = jnp.where(lane == 2, g1, info)
    info = jnp.where(lane == 3, g2, info)
    info_ref[...] = info


def _router(h, nw, wr):
    n = h.shape[0]
    tm = _pick(n, (512, 256, 128, 64))
    return pl.pallas_call(
        _router_kernel,
        grid=(n // tm,),
        in_specs=[pl.BlockSpec((tm, D_MODEL), lambda i: (i, 0)), pl.BlockSpec(nw.shape, lambda i: (0, 0)),
                  pl.BlockSpec(wr.shape, lambda i: (0, 0))],
        out_specs=[pl.BlockSpec((tm, D_MODEL), lambda i: (i, 0)), pl.BlockSpec((tm, LANE), lambda i: (i, 0))],
        out_shape=[jax.ShapeDtypeStruct((n, D_MODEL), F32), jax.ShapeDtypeStruct((n, LANE), F32)],
        compiler_params=_cparams("parallel"),
        name="moe_router",
    )(h, nw, wr)


def _expert_kernel(be_ref, x_ref, gate_ref, wg_ref, wu_ref, wd_ref, out_ref, acc_ref):
    f = pl.program_id(1)

    @pl.when(f == 0)
    def _():
        acc_ref[...] = jnp.zeros(acc_ref.shape, F32)

    xb = x_ref[...].astype(BF16)
    a = jnp.dot(xb, wg_ref[...], preferred_element_type=F32)
    b = jnp.dot(xb, wu_ref[...], preferred_element_type=F32)
    acc_ref[...] += jnp.dot((_silu(a) * b).astype(BF16), wd_ref[...], preferred_element_type=F32)

    @pl.when(f == pl.num_programs(1) - 1)
    def _():
        out_ref[...] = acc_ref[...] * gate_ref[...]


def _experts(block_expert, x_rows, row_gate, wg, wu, wd, tm):
    cap = x_rows.shape[0]
    dff = wg.shape[2]
    tf = _pick(dff, (1792, 512, 256, 128))
    grid_spec = pltpu.PrefetchScalarGridSpec(
        num_scalar_prefetch=1,
        grid=(cap // tm, dff // tf),
        in_specs=[pl.BlockSpec((tm, D_MODEL), lambda i, f, be: (i, 0)),
                  pl.BlockSpec((tm, 1), lambda i, f, be: (i, 0)),
                  pl.BlockSpec((None, D_MODEL, tf), lambda i, f, be: (be[i], 0, f)),
                  pl.BlockSpec((None, D_MODEL, tf), lambda i, f, be: (be[i], 0, f)),
                  pl.BlockSpec((None, tf, D_MODEL), lambda i, f, be: (be[i], f, 0))],
        out_specs=pl.BlockSpec((tm, D_MODEL), lambda i, f, be: (i, 0)),
        scratch_shapes=[pltpu.VMEM((tm, D_MODEL), F32)],
    )
    return pl.pallas_call(
        _expert_kernel,
        grid_spec=grid_spec,
        out_shape=jax.ShapeDtypeStruct((cap, D_MODEL), F32),
        compiler_params=_cparams("parallel", "arbitrary"),
        name="moe_experts",
    )(block_expert, x_rows, row_gate, wg, wu, wd)


def _combine_kernel(h_ref, ya_ref, yb_ref, out_ref):
    out_ref[...] = h_ref[...] + (ya_ref[...] + yb_ref[...])


def _combine(h, y2):
    n = h.shape[0]
    tm = _pick(n, (1024, 512, 256, 128, 64))
    spec = pl.BlockSpec((tm, D_MODEL), lambda i: (i, 0))
    second = pl.BlockSpec((tm, D_MODEL), lambda i: (i + n // tm, 0))
    return pl.pallas_call(
        _combine_kernel, grid=(n // tm,), in_specs=[spec, spec, second], out_specs=spec,
        out_shape=jax.ShapeDtypeStruct((n, D_MODEL), F32),
        compiler_params=_cparams("parallel"), name="moe_combine",
    )(h, y2, y2)


def _sc_gather(table, idx):
    rows = idx.shape[0]
    d = table.shape[1]
    info = plsc.get_sparse_core_info()
    workers = info.num_cores * info.num_subcores
    assert rows % (workers * SC_CHUNK) == 0, (rows, workers)
    per_worker = rows // workers
    mesh = plsc.VectorSubcoreMesh(core_axis_name="c", subcore_axis_name="s")

    @functools.partial(
        pl.kernel, mesh=mesh, out_type=jax.ShapeDtypeStruct((rows, d), table.dtype),
        scratch_types=[pltpu.VMEM((SC_CHUNK,), jnp.int32), pltpu.VMEM((SC_CHUNK, d), table.dtype),
                       pltpu.SemaphoreType.DMA],
        name="sc_gather")
    def gather(table_hbm, idx_hbm, out_hbm, idx_v, rows_v, sem):
        base = (lax.axis_index("s") * info.num_cores + lax.axis_index("c")) * per_worker

        @pl.loop(0, per_worker // SC_CHUNK)
        def _(i):
            off = pl.multiple_of(base + i * SC_CHUNK, 8)
            pltpu.sync_copy(idx_hbm.at[pl.ds(off, SC_CHUNK)], idx_v)
            pltpu.async_copy(table_hbm.at[idx_v], rows_v, sem).wait()
            pltpu.sync_copy(rows_v, out_hbm.at[pl.ds(off, SC_CHUNK)])

    return gather(table, idx)


def _moe(h, nw, router, wg, wu, wd, defer_combine=False):
    n = h.shape[0]
    tm = _pick(n, (512, 64))
    wr = jnp.pad(router.astype(F32), ((0, 0), (0, LANE - N_EXPERTS)))
    xn, info = _router(h, nw, wr)
    expert = info[:, 0:2].astype(jnp.int32).reshape(-1)
    gate = info[:, 2:4].reshape(-1)
    n_assign = 2 * n
    order = jnp.argsort(expert)
    onehot = (expert[:, None] == jnp.arange(N_EXPERTS, dtype=jnp.int32)[None, :]).astype(jnp.int32)
    running = jnp.cumsum(onehot, axis=0)
    counts = running[-1]
    padded = (counts + tm - 1) // tm * tm
    pad_end = jnp.cumsum(padded)
    pad_start = pad_end - padded
    start = jnp.cumsum(counts) - counts
    n_blocks = -(-n_assign // tm) + N_EXPERTS
    cap = n_blocks * tm
    block_start = jnp.arange(n_blocks, dtype=jnp.int32) * tm
    block_expert = jnp.minimum(jnp.sum(block_start[:, None] >= pad_end[None, :], axis=1), N_EXPERTS - 1)
    block_expert = block_expert.astype(jnp.int32)
    rank = (block_start - pad_start[block_expert])[:, None] + jnp.arange(tm, dtype=jnp.int32)[None, :]
    valid = (rank < counts[block_expert][:, None]).reshape(cap)
    src = order[jnp.clip(start[block_expert][:, None] + rank, 0, n_assign - 1).reshape(cap)]
    row_token = jnp.where(valid, src // 2, 0)
    row_gate = jnp.where(valid, gate[src], 0.0)
    dest = jnp.sum(onehot * (pad_start[None, :] + running - 1), axis=1)
    x_rows = _sc_gather(xn, row_token)
    y_rows = _experts(block_expert, x_rows, row_gate[:, None], wg, wu, wd, tm)
    y2 = _sc_gather(y_rows, dest.reshape(n, 2).T.reshape(n_assign))
    return (h, y2) if defer_combine else _combine(h, y2)


def _pad_cols(a, width):
    return jnp.pad(a, ((0, 0), (0, width - a.shape[1])))


def _row(a, width=None):
    a = a.reshape(1, -1).astype(F32)
    return a if width is None else _pad_cols(a, width)


def _rope_tables(tp):
    pos = jnp.arange(tp, dtype=F32)
    inv_freq = ROPE_THETA ** (-jnp.arange(0, MLA_ROPE, 2, dtype=F32) / MLA_ROPE)
    ang = pos[:, None] * inv_freq[None, :]
    cos, sin = jnp.cos(ang), jnp.sin(ang)
    ones = jnp.ones((tp, MLA_NOPE), F32)
    zeros = jnp.zeros((tp, MLA_SLOT - MLA_NOPE - MLA_ROPE), F32)
    c1 = jnp.concatenate([ones, cos, cos, zeros], axis=1)
    s1 = jnp.concatenate([0.0 * ones, -sin, sin, zeros], axis=1)
    return c1, s1


def _mla_weights(w_uq, w_ukv):
    half = MLA_ROPE // 2
    scale = (MLA_NOPE + MLA_ROPE) ** -0.5 * math.log2(math.e)
    zq =jnp.zeros((w_uq.shape[0], MLA_SLOT - MLA_NOPE - MLA_ROPE), F32)
    zn = jnp.zeros((w_uq.shape[0], MLA_NOPE), F32)
    zk = jnp.zeros((w_ukv.shape[0], MLA_SLOT - MLA_NOPE), F32)
    wqa, wqb, wk, wv = [], [], [], []
    for h in range(N_HEADS):
        q = w_uq[:, h * 96:(h + 1) * 96] * scale
        nope, x1, x2 = q[:, :MLA_NOPE], q[:, MLA_NOPE:MLA_NOPE + half], q[:, MLA_NOPE + half:]
        wqa += [nope, x1, x2, zq]
        wqb += [zn, x2, x1, zq]
        kv = w_ukv[:, h * 128:(h + 1) * 128]
        wk += [kv[:, :MLA_NOPE], zk]
        wv += [kv[:, MLA_NOPE:], zk]
    cat = lambda parts: jnp.concatenate(parts, axis=1).astype(BF16)
    ea = np.zeros((MLA_SLOT, MLA_SLOT), np.float32)
    eb = np.zeros((MLA_SLOT, MLA_SLOT), np.float32)
    for i in range(MLA_ROPE):
        ea[i, MLA_NOPE + i] = 1.0
        eb[(i + half) % MLA_ROPE, MLA_NOPE + i] = 1.0
    return cat(wqa), cat(wqb), cat(wk), cat(wv), jnp.asarray(ea, BF16), jnp.asarray(eb, BF16)


def _branch_weights(w_branch):
    wb = w_branch.astype(BF16)
    z = jnp.zeros((MLA_SLOT - MLA_V, D_MODEL), BF16)
    parts = []
    for h in range(N_HEADS):
        parts += [wb[1, h * MLA_V:(h + 1) * MLA_V], z]
    return wb[0], jnp.concatenate(parts, axis=0), wb[2], wb[3]


def _token_mixing(h, b, tp, p, l, tables):
    w_in = p["w_in"][l]
    col = lambda i, j: w_in[:, _OFF[i]:_OFF[j]]
    wg = jnp.concatenate([col(0, 3), col(4, 5), _pad_cols(col(3, 4), LANE)], axis=1).astype(BF16)
    wm = jnp.concatenate([col(5, 7), _pad_cols(col(7, 8), LANE)], axis=1).astype(BF16)
    wd = jnp.concatenate([col(8, 10), _pad_cols(col(10, 12), LANE)], axis=1).astype(BF16)
    wr = _pad_cols(col(12, 13), W_RWKV).astype(BF16)
    nw = _row(p["norm_mix"][l])
    xg, xm, xd, xr = _inproj(h, nw, wg, wm, wd, wr)
    shape3 = lambda a: a.reshape(b, tp, a.shape[1])

    gup = jnp.pad(p["gla_gate_up"][l], ((0, LANE - 16), (0, 0))).astype(BF16)
    y_gla = _gla(shape3(xg), gup, _row(p["gla_gate_bias"][l]), _row(jnp.tile(p["gla_norm"][l], N_HEADS)))

    wqa, wqb, wk, wv, ea, eb = _mla_weights(p["mla_w_uq"][l], p["mla_w_ukv"][l])
    q, k, v = _mla_prep(shape3(xm), _row(p["mla_q_norm"][l]), _row(p["mla_kv_norm"][l]),
                        wqa, wqb, wk, wv, ea, eb, *tables)
    y_mla = _flash(q, k, v)

    lanes4 = lambda a: jnp.pad(a.reshape(1, N_HEADS).astype(F32), ((0, 0), (N_HEADS, LANE - 2 * N_HEADS)))
    y_gdn = _gdn(shape3(xd), p["gdn_conv"][l].astype(F32), lanes4(p["gdn_a_log"][l]),
                 lanes4(p["gdn_dt_bias"][l]), _row(jnp.tile(p["gdn_norm"][l], N_HEADS)))

    w2a2 = jnp.zeros((LANE, 512), F32)
    w2a2 = w2a2.at[0:64, 0:256].set(p["rwkv_w2"][l]).at[64:128, 256:512].set(p["rwkv_a2"][l]).astype(BF16)
    g2 = jnp.pad(p["rwkv_g2"][l], ((0, 256 - 160), (0, 0))).astype(BF16)
    y_rwkv = _rwkv(shape3(xr), _row(p["rwkv_mu"][l], W_RWKV), w2a2, _row(p["rwkv_w0"][l]),
                   _row(p["rwkv_a0"][l]), g2, _row(p["rwkv_k_k"][l]), _row(p["rwkv_k_a"][l]),
                   _row(p["rwkv_r_k"][l]), _row(p["rwkv_ln_w"][l]), _row(p["rwkv_ln_b"][l]))

    flat = lambda a: a.reshape(b * tp, a.shape[2])
    ys = [flat(y_gla), flat(y_mla), flat(y_gdn), flat(y_rwkv)]
    return _merge(h, nw, col(13, 14).astype(BF16), ys, _branch_weights(p["w_branch"][l]),
                  p["w_out"][l].astype(BF16))


def kernel(x, meta_tokens, norm_mix, w_in, gla_gate_up, gla_gate_bias, gla_norm, mla_q_norm, mla_w_uq, mla_kv_norm, mla_w_ukv, gdn_conv, gdn_a_log, gdn_dt_bias, gdn_norm, rwkv_mu, rwkv_w0, rwkv_w2, rwkv_a0, rwkv_a2, rwkv_g2, rwkv_k_k, rwkv_k_a, rwkv_r_k, rwkv_ln_w, rwkv_ln_b, w_branch, w_out, norm_ffn, ffn_w_gate, ffn_w_up, ffn_w_down, moe_router, moe_w_gate, moe_w_up, moe_w_down, norm_final):
    p = dict(norm_mix=norm_mix, w_in=w_in, gla_gate_up=gla_gate_up, gla_gate_bias=gla_gate_bias,
             gla_norm=gla_norm, mla_q_norm=mla_q_norm, mla_w_uq=mla_w_uq, mla_kv_norm=mla_kv_norm,
             mla_w_ukv=mla_w_ukv, gdn_conv=gdn_conv, gdn_a_log=gdn_a_log, gdn_dt_bias=gdn_dt_bias,
             gdn_norm=gdn_norm, rwkv_mu=rwkv_mu, rwkv_w0=rwkv_w0, rwkv_w2=rwkv_w2, rwkv_a0=rwkv_a0,
             rwkv_a2=rwkv_a2, rwkv_g2=rwkv_g2, rwkv_k_k=rwkv_k_k, rwkv_k_a=rwkv_k_a, rwkv_r_k=rwkv_r_k,
             rwkv_ln_w=rwkv_ln_w, rwkv_ln_b=rwkv_ln_b, w_branch=w_branch, w_out=w_out)
    b, seq, d = x.shape
    t_real = N_META + seq
    tp = -(-t_real // 128) * 128
    meta = jnp.broadcast_to(meta_tokens[None].astype(x.dtype), (b, N_META, d))
    h = jnp.concatenate([meta, x, jnp.zeros((b, tp - t_real, d), x.dtype)], axis=1).reshape(b * tp, d)
    tables = _rope_tables(tp)
    depth = norm_mix.shape[0]
    y2 = None
    for l in range(depth):
        h = _token_mixing(h, b, tp, p, l, tables)
        nw = _row(norm_ffn[l])
        if l % 2 == 0:
            h = _ffn(h, nw, ffn_w_gate[l // 2].astype(BF16), ffn_w_up[l // 2].astype(BF16),
                     ffn_w_down[l // 2].astype(BF16))
        elif l == depth - 1:
            h, y2 = _moe(h, nw, moe_router[l // 2], moe_w_gate[l // 2].astype(BF16),
                         moe_w_up[l // 2].astype(BF16), moe_w_down[l // 2].astype(BF16), defer_combine=True)
        else:
            h = _moe(h, nw, moe_router[l // 2], moe_w_gate[l // 2].astype(BF16),
                     moe_w_up[l // 2].astype(BF16), moe_w_down[l // 2].astype(BF16))
    return _final(h, _row(norm_final), b, tp, seq, y2)
```

```python
import functools
import math

import jax
import jax.numpy as jnp
import numpy as np
from jax import lax
from jax.experimental import pallas as pl
from jax.experimental.pallas import tpu as pltpu
from jax.experimental.pallas import tpu_sc as plsc

F32 = jnp.float32
BF16 = jnp.bfloat16

D_MODEL = 1024
N_META = 16
N_HEADS = 4
GLA_DK = 32
GLA_DV = 64
GLA_TAU = 16.0
MLA_NOPE = 64
MLA_ROPE = 32
MLA_V = 64
MLA_SLOT = 128
ROPE_THETA = 10000.0
GDN_DK = 64
GDN_CONV = 4
RWKV_N = 64
RWKV_LN_EPS = RWKV_N * 1e-5
CHUNK = 64
SUB = 16
PREP_GROUP = 5
MIX_BATCH = 2
N_EXPERTS = 8
SC_CHUNK = 64
NORM_EPS = 1e-6
L2_EPS = 1e-6
NEG_INF = -1e30
EXP_CLAMP = 80.0

LANE = 128
VMEM_LIMIT = 56 * 1024 * 1024

_OFF = np.cumsum([0, 128, 128, 256, 16, 256, 256, 128, 32, 768, 256, 4, 4, 1056, 4096]).tolist()
W_GLA, W_MLA, W_GDN, W_RWKV = 896, 512, 1152, 1152


def _cparams(*sem):
    return pltpu.CompilerParams(dimension_semantics=sem, vmem_limit_bytes=VMEM_LIMIT)


def _pick(n, prefs):
    for p in prefs:
        if n % p == 0:
            return p
    raise ValueError(f"no tile for {n}")


def _dot(a, b):
    return jnp.dot(a.astype(BF16), b.astype(BF16), preferred_element_type=F32)


def _dot_nt(a, b):
    return lax.dot_general(a.astype(BF16), b.astype(BF16), (((1,), (1,)), ((), ())),
                           preferred_element_type=F32)


def _dot_tn(a, b):
    return lax.dot_general(a.astype(BF16), b.astype(BF16), (((0,), (0,)), ((), ())),
                           preferred_element_type=F32)


def _split3(x):
    hi = x.astype(BF16)
    r1 = x - hi.astype(F32)
    mid = r1.astype(BF16)
    lo = (r1 - mid.astype(F32)).astype(BF16)
    return hi, mid, lo


def _dot01_left(m01, x):
    return sum(jnp.dot(m01, p, preferred_element_type=F32) for p in _split3(x))


def _dot01_right(x, m01):
    return sum(jnp.dot(p, m01, preferred_element_type=F32) for p in _split3(x))


def _seg_sum(x, m01):
    hi = x.astype(BF16)
    lo = (x - hi.astype(F32)).astype(BF16)
    return jnp.dot(hi, m01, preferred_element_type=F32) + jnp.dot(lo, m01, preferred_element_type=F32)


def _iota(shape, dim):
    return lax.broadcasted_iota(jnp.int32, shape, dim)


def _div(x, w):
    return x >> int(math.log2(w))


def _ltri(n):
    return (_iota((n, n), 0) >= _iota((n, n), 1)).astype(BF16)


def _head_ones(n, w):
    return (_div(_iota((n, n), 0), w) == _div(_iota((n, n), 1), w)).astype(BF16)


def _lane_masks(width, w):
    lane = _div(_iota((1, width), 1), w)
    return [(lane == h).astype(F32) for h in range(width // w)]


def _stack(x, masks):
    return jnp.concatenate([x * m for m in masks], axis=0)


def _unstack(y, n):
    out = y[0:n]
    for h in range(1, y.shape[0] // n):
        out = out + y[h * n:(h + 1) * n]
    return out


def _rms(x, w):
    return x * lax.rsqrt(jnp.mean(x * x, axis=-1, keepdims=True) + NORM_EPS) * w


def _sigmoid(x):
    return 1.0 / (1.0 + jnp.exp(-x))


def _silu(x):
    return x * _sigmoid(x)


def _softplus(x):
    return jnp.maximum(x, 0.0) + jnp.log(1.0 + jnp.exp(-jnp.abs(x)))


def _neumann_inverse(xs, eye):
    n = xs[0].shape[0]
    ts = [eye + x for x in xs]
    ps = [_dot(x, x) for x in xs]
    for _ in range(int(math.log2(CHUNK)) - 2):
        yield
        both = [_dot(jnp.concatenate([t, p], axis=0), p) for t, p in zip(ts, ps)]
        ts = [t + tp[:n] for t, tp in zip(ts, both)]
        ps = [tp[n:] for tp in both]
    yield
    return [t + _dot(t, p) for t, p in zip(ts, ps)]


def _for_chunks(n, body, group):
    def trip(i, carry):
        body([i * group + g for g in range(group)])
        return carry

    if n >= group:
        lax.fori_loop(0, n // group, trip, 0)
    if n % group:
        body(list(range(n - n % group, n)))


def _inproj_kernel(h_ref, nw_ref, wg_ref, wm_ref, wd_ref, wr_ref, og_ref, om_ref, od_ref, or_ref):
    xb = _rms(h_ref[...], nw_ref[...]).astype(BF16)
    og_ref[...] = jnp.dot(xb, wg_ref[...], preferred_element_type=F32)
    om_ref[...] = jnp.dot(xb, wm_ref[...], preferred_element_type=F32)
    od_ref[...] = jnp.dot(xb, wd_ref[...], preferred_element_type=F32)
    or_ref[...] = jnp.dot(xb, wr_ref[...], preferred_element_type=F32)


def _inproj(h, nw, wg, wm, wd, wr):
    n = h.shape[0]
    tm = _pick(n, (512, 256, 128, 64))
    full = lambda a: pl.BlockSpec(a.shape, lambda i: (0, 0))
    row = lambda w: pl.BlockSpec((tm, w), lambda i: (i, 0))
    return pl.pallas_call(
        _inproj_kernel,
        grid=(n // tm,),
        in_specs=[row(D_MODEL), full(nw), full(wg), full(wm), full(wd), full(wr)],
        out_specs=[row(W_GLA), row(W_MLA), row(W_GDN), row(W_RWKV)],
        out_shape=[jax.ShapeDtypeStruct((n, w), F32) for w in (W_GLA, W_MLA, W_GDN, W_RWKV)],
        compiler_params=_cparams("parallel"),
        name="inproj",
    )(h, nw, wg, wm, wd, wr)


def _merge_kernel(h_ref, nw_ref, wgate_ref, yg_ref, ym_ref, yd_ref, yr_ref,
                  wbg_ref, wbm_ref, wbd_ref, wbr_ref, wout_ref, out_ref):
    x = h_ref[...]
    xb = _rms(x, nw_ref[...]).astype(BF16)
    acc = jnp.zeros(x.shape, F32)
    branches = ((yg_ref, wbg_ref), (ym_ref, wbm_ref), (yd_ref, wbd_ref), (yr_ref, wbr_ref))
    for i, (y_ref, wb_ref) in enumerate(branches):
        logits = jnp.dot(xb, wgate_ref[:, i * D_MODEL:(i + 1) * D_MODEL], preferred_element_type=F32)
        proj = jnp.dot(y_ref[...], wb_ref[...], preferred_element_type=F32)
        acc = acc + _sigmoid(logits) * proj
    out_ref[...] = x + jnp.dot(acc.astype(BF16), wout_ref[...], preferred_element_type=F32)


def _merge(h, nw, wgate, ys, wbs, wout):
    n = h.shape[0]
    tm = _pick(n, (512, 256, 128, 64))
    full = lambda a: pl.BlockSpec(a.shape, lambda i: (0, 0))
    row = lambda w: pl.BlockSpec((tm, w), lambda i: (i, 0))
    return pl.pallas_call(
        _merge_kernel,
        grid=(n // tm,),
        in_specs=[row(D_MODEL), full(nw), full(wgate)] + [row(y.shape[1]) for y in ys]
        + [full(w) for w in wbs] + [full(wout)],
        out_specs=row(D_MODEL),
        out_shape=jax.ShapeDtypeStruct((n, D_MODEL), F32),
        compiler_params=_cparams("parallel"),
        name="merge",
    )(h, nw, wgate, *ys, *wbs, wout)


def _ffn_kernel(h_ref, nw_ref, wg_ref, wu_ref, wd_ref, out_ref, xb_ref, acc_ref):
    f = pl.program_id(1)

    @pl.when(f == 0)
    def _():
        xb_ref[...] = _rms(h_ref[...], nw_ref[...]).astype(BF16)
        acc_ref[...] = jnp.zeros(acc_ref.shape, F32)

    xb = xb_ref[...]
    a = jnp.dot(xb, wg_ref[...], preferred_element_type=F32)
    b = jnp.dot(xb, wu_ref[...], preferred_element_type=F32)
    acc_ref[...] += jnp.dot((_silu(a) * b).astype(BF16), wd_ref[...], preferred_element_type=F32)

    @pl.when(f == pl.num_programs(1) - 1)
    def _():
        out_ref[...] = h_ref[...] + acc_ref[...]


def _ffn(h, nw, wg, wu, wd):
    n = h.shape[0]
    dff = wg.shape[1]
    tm = _pick(n, (512, 256, 128, 64))
    tf = _pick(dff, (1408, 512, 256, 128))
    return pl.pallas_call(
        _ffn_kernel,
        grid=(n // tm, dff // tf),
        in_specs=[pl.BlockSpec((tm, D_MODEL), lambda i, f: (i, 0)),
                  pl.BlockSpec(nw.shape, lambda i, f: (0, 0)),
                  pl.BlockSpec((D_MODEL, tf), lambda i, f: (0, f)),
                  pl.BlockSpec((D_MODEL, tf), lambda i, f: (0, f)),
                  pl.BlockSpec((tf, D_MODEL), lambda i, f: (f, 0))],
        out_specs=pl.BlockSpec((tm, D_MODEL), lambda i, f: (i, 0)),
        out_shape=jax.ShapeDtypeStruct((n, D_MODEL), F32),
        scratch_shapes=[pltpu.VMEM((tm, D_MODEL), BF16), pltpu.VMEM((tm, D_MODEL), F32)],
        compiler_params=_cparams("parallel", "arbitrary"),
        name="ffn",
    )(h, nw, wg, wu, wd)


def _final_kernel(nw_ref, *refs):
    out_ref = refs[-1]
    x = refs[0][...]
    for extra in refs[1:-1]:
        x = x + extra[...]
    out_ref[...] = _rms(x, nw_ref[...])


def _final(h, nw, b, tp, seq, y2=None):
    tq = _pick(seq, (1024, 512, 256, 128, 64))
    n = b * tp
    rows = lambda shift: pl.BlockSpec((pl.Element(tq), pl.Element(D_MODEL)),
                                      lambda i, j: (pl.multiple_of(shift + i * tp + N_META + j * tq, 8), 0))
    operands = [h] if y2 is None else [h, y2, y2]
    shifts = [0] if y2 is None else [0, 0, n]
    return pl.pallas_call(
        _final_kernel,
        grid=(b, seq // tq),
        in_specs=[pl.BlockSpec(nw.shape, lambda i, j: (0, 0))] + [rows(s) for s in shifts],
        out_specs=pl.BlockSpec((None, tq, D_MODEL), lambda i, j: (i, j, 0)),
        out_shape=jax.ShapeDtypeStruct((b, seq, D_MODEL), F32),
        compiler_params=_cparams("parallel", "parallel"),
        name="final_norm",
    )(nw, *operands)


_DONE = object()


def _interleave(generators):
    live = list(generators)
    while live:
        live = [g for g in live if next(g, _DONE) is not _DONE]


def _mixer_kernel(body, n_params, x_ref, *refs):
    params, y_ref, scratch = refs[:n_params], refs[n_params], refs[n_params + 1:]
    parts = [body(x_ref.at[b], *params, y_ref.at[b], *[s.at[b] for s in scratch]) for b in range(x_ref.shape[0])]
    n_chunks = x_ref.shape[1] // CHUNK
    if n_chunks <= PREP_GROUP:
        pending = []
        for preprocess, prepare, _, _ in parts:
            _interleave([preprocess()] + pending)
            pending = [prepare(list(range(n_chunks)))]
        _interleave(pending)
    else:
        for preprocess, _, _, _ in parts:
            _interleave([preprocess()])
        _for_chunks(n_chunks, lambda cis: [_interleave([prepare(cis)]) for _, prepare, _, _ in parts], PREP_GROUP)
    _for_chunks(n_chunks, lambda cis: [_interleave([advance(ci) for _, _, advance, _ in parts]) for ci in cis], 1)
    for _, _, _, finish in parts:
        finish()


def _mixer_block(tp):
    return _pick(tp, (320, 128, 64))


def _mixer_call(body, name, x, params, scratch, tb):
    b, tp, width = x.shape
    nb = MIX_BATCH if b % MIX_BATCH == 0 else 1
    full = lambda a: pl.BlockSpec(a.shape, lambda i, j: (0, 0))
    return pl.pallas_call(
        functools.partial(_mixer_kernel, body, len(params)),
        grid=(b // nb, tp // tb),
        in_specs=[pl.BlockSpec((nb, tb, width), lambda i, j: (i, j, 0))] + [full(a) for a in params],
        out_specs=pl.BlockSpec((nb, tb, 256), lambda i, j: (i, j, 0)),
        out_shape=jax.ShapeDtypeStruct((b, tp, 256), BF16),
        scratch_shapes=[pltpu.VMEM((nb,) + shape, dtype) for shape, dtype in scratch],
        compiler_params=_cparams("parallel", "arbitrary"),
        name=name,
    )(x, *params)


def _gla_body(x_ref, gup_ref, gb_ref, nw_ref, y_ref, st_ref, la_ref, o_s, qg_s, kv_s, gt_s):
    @pl.when(pl.program_id(1) == 0)
    def _():
        st_ref[...] = jnp.zeros(st_ref.shape, F32)

    def preprocess():
        z = _dot(x_ref[:, 768:896], gup_ref[...]) + gb_ref[...]
        yield
        la_ref[...] = -_softplus(-z) * (1.0 / GLA_TAU)

    ltri = _ltri(CHUNK)
    qmasks = _lane_masks(N_HEADS * GLA_DK, GLA_DK)
    vmasks = _lane_masks(N_HEADS * GLA_DV, GLA_DV)
    bd = (_div(_iota((256, 128), 0), GLA_DV) == _div(_iota((256, 128), 1), GLA_DK)).astype(F32)
    hsum = _head_ones(N_HEADS * GLA_DV, GLA_DV)
    nw = nw_ref[...]

    n4 = N_HEADS * GLA_DV

    def prepare(cis):
        n = range(len(cis))
        rows = [pl.ds(pl.multiple_of(ci * CHUNK, CHUNK), CHUNK) for ci in cis]
        g = [_dot01_left(ltri, la_ref[r, :]) for r in rows]
        q = [x_ref[r, 0:128] * (GLA_DK ** -0.5) for r in rows]
        k = [x_ref[r, 128:256] for r in rows]
        v = [x_ref[r, 256:512] for r in rows]
        intra = [[] for _ in n]
        for s in range(CHUNK // SUB):
            lo, hi = s * SUB, (s + 1) * SUB
            sc = []
            for i in n:
                gs = jnp.zeros((1, 128), F32) if s == 0 else g[i][lo - 1:lo]
                qs = q[i][lo:hi] * jnp.exp(g[i][lo:hi] - gs)
                kt = k[i][:hi] * jnp.exp(jnp.minimum(gs - g[i][:hi], EXP_CLAMP))
                sc.append(_dot_nt(_stack(qs, qmasks), kt))
            causal = _iota(sc[0].shape, 1) <= lo + (_iota(sc[0].shape, 0) & (SUB - 1))
            yield
            p = [_dot(jnp.where(causal, sc[i], 0.0), v[i][:hi]) for i in n]
            for i in n:
                intra[i].append(sum(p[i][h * SUB:(h + 1) * SUB] * vmasks[h] for h in range(N_HEADS)))
            yield
        for i in n:
            g_last = g[i][CHUNK - 1:CHUNK]
            o_s[rows[i], :] = jnp.concatenate(intra[i], axis=0)
            qg_s[rows[i], :] = (q[i] * jnp.exp(g[i])).astype(BF16)
            kv_s[pl.ds(pl.multiple_of(cis[i] * n4, n4), n4), :] = bd * _dot_tn(v[i], k[i] * jnp.exp(g_last - g[i]))
            gt_s[pl.ds(pl.multiple_of(cis[i] * 8, 8), 8), :] = jnp.broadcast_to(jnp.exp(g_last), (8, 128))

    def advance(ci):
        rows = pl.ds(pl.multiple_of(ci * CHUNK, CHUNK), CHUNK)
        st = st_ref[...]
        inter = _dot_nt(qg_s[rows, :], st)
        yield
        o_s[rows, :] += inter
        st_ref[...] = (st * gt_s[pl.ds(pl.multiple_of(ci * 8, 8), 1), :]
                       + kv_s[pl.ds(pl.multiple_of(ci * n4, n4), n4), :])

    def finish():
        o = o_s[...]
        ms = _seg_sum(o * o, hsum) * (1.0 / GLA_DV)
        y_ref[...] = (o * lax.rsqrt(ms + NORM_EPS) * nw * _silu(x_ref[:, 512:768])).astype(y_ref.dtype)

    return preprocess, prepare, advance, finish


def _gla(x, gup, gb, nw):
    tb = _mixer_block(x.shape[1])
    nc = tb // CHUNK
    scratch = [((256, 128), F32), ((tb, 128), F32), ((tb, 256), F32), ((tb, 128), BF16),
               ((nc * N_HEADS * GLA_DV, 128), F32), ((nc * 8, 128), F32)]
    return _mixer_call(_gla_body, "gla", x, (gup, gb, nw), scratch, tb)


def _gdn_body(x_ref, cw_ref, alog_ref, dtb_ref, nw_ref, y_ref,
              s_ref, xp_ref, q_ref, k_ref, v_ref, beta_ref, gd_ref,
              u_s, w_s, attn_s, qd_s, ke_s, gt_s, o_s):
    tb = x_ref.shape[0]
    first = pl.program_id(1) == 0

    @pl.when(first)
    def _():
        s_ref[...] = jnp.zeros(s_ref.shape, F32)
        xp_ref[0:8, :] = jnp.zeros((8, 768), F32)

    @pl.when(jnp.logical_not(first))
    def _():
        xp_ref[0:8, :] = xp_ref[tb:tb + 8, :]

    hsum = _head_ones(256, GDN_DK)

    def preprocess():
        xp_ref[8:tb + 8, :] = x_ref[:, 0:768]
        yield
        conv = cw_ref[0:1, :] * xp_ref[8 - (GDN_CONV - 1):8 - (GDN_CONV - 1) + tb, :]
        for j in range(1, GDN_CONV):
            yield
            conv = conv + cw_ref[j:j + 1, :] * xp_ref[8 - (GDN_CONV - 1) + j:8 - (GDN_CONV - 1) + j + tb, :]
        yield
        c = _silu(conv)
        q = c[:, 0:256]
        k = c[:, 256:512]
        yield
        q_ref[...] = q * lax.rsqrt(_seg_sum(q * q, hsum) + L2_EPS) * (GDN_DK ** -0.5)
        yield
        k_ref[...] = k * lax.rsqrt(_seg_sum(k * k, hsum) + L2_EPS)
        v_ref[...] = c[:, 512:768]
        yield
        gates = x_ref[:, 1024:1152]
        beta_ref[...] = _sigmoid(gates)
        gd_ref[...] = -jnp.exp(alog_ref[...]) * _softplus(gates + dtb_ref[...])

    ltri = _ltri(CHUNK)
    masks = _lane_masks(256, GDN_DK)
    expand = lambda off: (_iota((128, 256), 0) == _div(_iota((128, 256), 1), GDN_DK) + off).astype(BF16)
    exp_beta, exp_g = expand(0), expand(N_HEADS)
    r = _iota((256, 256), 0)
    cidx = _iota((256, 256), 1)
    same = _div(r, CHUNK) == _div(cidx, CHUNK)
    incl = jnp.logical_and(same, r >= cidx)
    strict = jnp.logical_and(same, r > cidx)
    eye = (r == cidx).astype(F32)
    bd = same.astype(F32)
    nw = nw_ref[...]

    def prepare(cis):
        n = range(len(cis))
        rows = [pl.ds(pl.multiple_of(ci * CHUNK, CHUNK), CHUNK) for ci in cis]
        srows = [pl.ds(pl.multiple_of(ci * (N_HEADS * CHUNK), N_HEADS * CHUNK), N_HEADS * CHUNK) for ci in cis]
        q = [q_ref[r, :] for r in rows]
        k = [k_ref[r, :] for r in rows]
        v = [v_ref[r, :] for r in rows]
        bexp = [_dot01_right(beta_ref[r, :], exp_beta) for r in rows]
        gcum = [_dot01_left(ltri, gd_ref[r, :]) for r in rows]
        yield
        gexp = [_dot01_right(g, exp_g) for g in gcum]
        kst = [_stack(x, masks) for x in k]
        kb = [k[i] * bexp[i] for i in n]
        n4 = N_HEADS * CHUNK
        yield
        both = [_dot_nt(jnp.concatenate([_stack(kb[i], masks), _stack(q[i], masks)], axis=0), kst[i]) for i in n]
        kk = [x[:n4] for x in both]
        qk = [x[n4:] for x in both]
        yield
        dec = []
        for g in gexp:
            gcol = jnp.sum(_stack(g, masks), axis=1, keepdims=True) * (1.0 / GDN_DK)
            grow = jnp.sum(eye * gcol, axis=0, keepdims=True)
            dec.append(jnp.exp(jnp.minimum(gcol - grow, 0.0)))
        for i in n:
            attn_s[srows[i], :] = jnp.where(incl, qk[i] * dec[i], 0.0).astype(BF16)
        yield
        t = yield from _neumann_inverse([-jnp.where(strict, kk[i] * dec[i], 0.0) for i in n], eye)
        yield
        u = [_dot(t[i], _stack(v[i] * bexp[i], masks)) for i in n]
        w = [_dot(t[i], _stack(kb[i] * jnp.exp(gexp[i]), masks)) for i in n]
        yield
        for i in n:
            u_s[rows[i], :] = _unstack(u[i], CHUNK)
            w_s[rows[i], :] = _unstack(w[i], CHUNK).astype(BF16)
            g_last = gexp[i][CHUNK - 1:CHUNK]
            qd_s[rows[i], :] = (q[i] * jnp.exp(gexp[i])).astype(BF16)
            ke_s[rows[i], :] = (k[i] * jnp.exp(g_last - gexp[i])).astype(BF16)
            gt_s[pl.ds(pl.multiple_of(cis[i] * 8, 8), 8), :] = jnp.broadcast_to(jnp.exp(g_last), (8, 256))

    def advance(ci):
        rows = pl.ds(pl.multiple_of(ci * CHUNK, CHUNK), CHUNK)
        srows = pl.ds(pl.multiple_of(ci * (N_HEADS * CHUNK), N_HEADS * CHUNK), N_HEADS * CHUNK)
        s = s_ref[...]
        sb = s.astype(BF16)
        ws = jnp.dot(w_s[rows, :], sb, preferred_element_type=F32)
        qs = jnp.dot(qd_s[rows, :], sb, preferred_element_type=F32)
        yield
        v_new = u_s[rows, :] - ws
        av = _dot(attn_s[srows, :], _stack(v_new, masks))
        kv = _dot_tn(ke_s[rows, :], v_new)
        yield
        o_s[rows, :] = qs + _unstack(av, CHUNK)
        s_ref[...] = s * gt_s[pl.ds(pl.multiple_of(ci * 8, 8), 1), :] + bd * kv

    def finish():
        o = o_s[...]
        ms = _seg_sum(o * o, hsum) * (1.0 / GDN_DK)
        y_ref[...] = (o * lax.rsqrt(ms + NORM_EPS) * nw * _silu(x_ref[:, 768:1024])).astype(y_ref.dtype)

    return preprocess, prepare, advance, finish


def _gdn(x, cw, alog, dtb, nw):
    tb = _mixer_block(x.shape[1])
    nc = tb // CHUNK
    scratch = [((256, 256), F32), ((tb + 8, 768), F32),
               ((tb, 256), F32), ((tb, 256), F32), ((tb, 256), F32), ((tb, 128), F32), ((tb, 128), F32),
               ((tb, 256), F32), ((tb, 256), BF16), ((nc * N_HEADS * CHUNK, 256), BF16),
               ((tb, 256), BF16), ((tb, 256), BF16), ((nc * 8, 256), F32), ((tb, 256), F32)]
    return _mixer_call(_gdn_body, "gdn", x, (cw, alog, dtb, nw), scratch, tb)


def _rwkv_body(x_ref, mu_ref, w2a2_ref, w0_ref, a0_ref, g2_ref, kk_ref, ka_ref, rk_ref,
               lnw_ref, lnb_ref, y_ref,
               s_ref, xp_ref, r_s, k_s, v_s, kk_s, b_s, lw_s, g_s,
               at_s, z_s, arb_s, yv_s, rt_s, be_s, vk_s, gt_s, y_s):
    tb = x_ref.shape[0]
    first = pl.program_id(1) == 0

    @pl.when(first)
    def _():
        s_ref[...] = jnp.zeros(s_ref.shape, F32)
        xp_ref[0:8, :] = jnp.zeros((8, W_RWKV), F32)

    @pl.when(jnp.logical_not(first))
    def _():
        xp_ref[0:8, :] = xp_ref[tb:tb + 8, :]

    hsum = _head_ones(256, RWKV_N)

    def preprocess():
        x = x_ref[...]
        xp_ref[8:tb + 8, :] = x
        yield
        z = x + (xp_ref[7:tb + 7, :] - x) * mu_ref[...]
        r = z[:, 0:256]
        k = z[:, 256:512]
        wa = z[:, 768:896]
        wa = jnp.where(_iota(wa.shape, 1) < 64, jnp.tanh(wa), wa)
        yield
        pre = _dot(wa, w2a2_ref[...])
        yield
        w_log = -_softplus(-(w0_ref[...] + pre[:, 0:256])) - 0.5
        lw_s[...] = -jnp.exp(w_log)
        yield
        a = _sigmoid(a0_ref[...] + pre[:, 256:512])
        kkv = k * kk_ref[...]
        yield
        kkn = kkv * lax.rsqrt(_seg_sum(kkv * kkv, hsum) + L2_EPS)
        yield
        r_s[...] = r
        k_s[...] = k * (1.0 + (a - 1.0) * ka_ref[...])
        v_s[...] = z[:, 512:768]
        kk_s[...] = kkn
        b_s[...] = kkn * a
        yield
        g_s[...] = _dot(_sigmoid(z[:, 896:1152]), g2_ref[...])

    ltri = _ltri(CHUNK)
    masks = _lane_masks(256, RWKV_N)
    rr = _iota((256, 256), 0)
    cc = _iota((256, 256), 1)
    same = _div(rr, CHUNK) == _div(cc, CHUNK)
    incl = jnp.logical_and(same, rr >= cc)
    strict = jnp.logical_and(same, rr > cc)
    eye = (rr == cc).astype(F32)
    bd = same.astype(F32)
    rk = rk_ref[...]
    lnw = lnw_ref[...]
    lnb = lnb_ref[...]

    n4 = N_HEADS * CHUNK

    def prepare(cis):
        n = range(len(cis))
        rows = [pl.ds(pl.multiple_of(ci * CHUNK, CHUNK), CHUNK) for ci in cis]
        srows = [pl.ds(pl.multiple_of(ci * n4, n4), n4) for ci in cis]
        lw = [lw_s[r, :] for r in rows]
        gl = [_dot01_left(ltri, x) for x in lw]
        v = [v_s[r, :] for r in rows]
        k = [k_s[r, :] for r in rows]
        b = [b_s[r, :] for r in rows]
        yield
        e_neg = [jnp.exp(-g) for g in gl]
        a_st = [_stack(-kk_s[rows[i], :] * jnp.exp(gl[i] - lw[i]), masks) for i in n]
        r_t = [r_s[rows[i], :] * jnp.exp(gl[i]) for i in n]
        yield
        amat = []
        for i in n:
            lhs = jnp.concatenate([a_st[i], _stack(r_t[i], masks)], axis=0)
            rhs = jnp.concatenate([_stack(b[i] * e_neg[i], masks), _stack(k[i] * e_neg[i], masks)], axis=0)
            amat.append(_dot_nt(lhs, rhs))
        yield
        vst = [_stack(x, masks) for x in v]
        both = [_dot(jnp.concatenate([jnp.where(strict, amat[i][0:n4, n4:], 0.0),
                                      jnp.where(incl, amat[i][n4:, n4:], 0.0)], axis=0), vst[i]) for i in n]
        av = [x[:n4] for x in both]
        yv = [x[n4:] for x in both]
        yield
        t = yield from _neumann_inverse([jnp.where(strict, m[0:n4, 0:n4], 0.0) for m in amat], eye)
        yield
        z = [_dot(t[i], av[i]) for i in n]
        at = [_dot(t[i], a_st[i]) for i in n]
        yield
        for i in n:
            g_last = gl[i][CHUNK - 1:CHUNK]
            e_end = jnp.exp(g_last - gl[i])
            z_s[srows[i], :] = z[i]
            at_s[srows[i], :] = at[i].astype(BF16)
            arb_s[srows[i], :] = jnp.where(incl, amat[i][n4:, 0:n4], 0.0).astype(BF16)
            yv_s[rows[i], :] = _unstack(yv[i], CHUNK)
            rt_s[rows[i], :] = r_t[i].astype(BF16)
            be_s[rows[i], :] = (b[i] * e_end).astype(BF16)
            vk_s[srows[i], :] = bd * _dot_tn(v[i], k[i] * e_end)
            gt_s[pl.ds(pl.multiple_of(cis[i] * 8, 8), 8), :] = jnp.broadcast_to(jnp.exp(g_last), (8, 256))

    def advance(ci):
        rows = pl.ds(pl.multiple_of(ci * CHUNK, CHUNK), CHUNK)
        srows = pl.ds(pl.multiple_of(ci * n4, n4), n4)
        s = s_ref[...]
        sb = s.astype(BF16)
        u_st = _dot_nt(at_s[srows, :], sb) + z_s[srows, :]
        rs = _dot_nt(rt_s[rows, :], sb)
        yield
        au = _dot(arb_s[srows, :], u_st)
        ub = _dot_tn(_unstack(u_st, CHUNK), be_s[rows, :])
        yield
        y_s[rows, :] = rs + _unstack(au, CHUNK) + yv_s[rows, :]
        s_ref[...] = s * gt_s[pl.ds(pl.multiple_of(ci * 8, 8), 1), :] + bd * ub + vk_s[srows, :]

    def finish():
        y = y_s[...]
        v = v_s[...]
        mean = _seg_sum(y, hsum) * (1.0 / RWKV_N)
        d = y - mean
        var = _seg_sum(d * d, hsum) * (1.0 / RWKV_N)
        yn = d * lax.rsqrt(var + RWKV_LN_EPS) * lnw + lnb
        bonus = _seg_sum(r_s[...] * k_s[...] * rk, hsum) * v
        y_ref[...] = ((yn + bonus) * g_s[...]).astype(y_ref.dtype)

    return preprocess, prepare, advance, finish


def _rwkv(x, mu, w2a2, w0, a0, g2, kk, ka, rk, lnw, lnb):
    tb = _mixer_block(x.shape[1])
    nc = tb // CHUNK
    stacked = lambda dt: ((nc * N_HEADS * CHUNK, 256), dt)
    scratch = ([((256, 256), F32), ((tb + 8, W_RWKV), F32)] + [((tb, 256), F32)] * 7
               + [stacked(BF16), stacked(F32), stacked(BF16), ((tb, 256), F32), ((tb, 256), BF16),
                  ((tb, 256), BF16), stacked(F32), ((nc * 8, 256), F32), ((tb, 256), F32)])
    return _mixer_call(_rwkv_body, "rwkv", x, (mu, w2a2, w0, a0, g2, kk, ka, rk, lnw, lnb), scratch, tb)


def _mla_prep_kernel(x_ref, qnw_ref, kvnw_ref, wqa_ref, wqb_ref, wk_ref, wv_ref, ea_ref, eb_ref,
                     c1_ref, s1_ref, q_ref, k_ref, v_ref):
    x = x_ref[...]
    qn = _rms(x[:, 0:256], qnw_ref[...]).astype(BF16)
    kvn = _rms(x[:, 256:384], kvnw_ref[...]).astype(BF16)
    kpe = x[:, 384:512].astype(BF16)
    c1 = c1_ref[...]
    s1 = s1_ref[...]
    qa = jnp.dot(qn, wqa_ref[...], preferred_element_type=F32)
    qb = jnp.dot(qn, wqb_ref[...], preferred_element_type=F32)
    kn = jnp.dot(kvn, wk_ref[...], preferred_element_type=F32)
    kp = (jnp.dot(kpe, ea_ref[...], preferred_element_type=F32) * c1
          + jnp.dot(kpe, eb_ref[...], preferred_element_type=F32) * s1)
    ones_lane = ((_iota((1, N_HEADS * MLA_SLOT), 1) & (MLA_SLOT - 1)) == MLA_V).astype(F32)
    v_ref[...] = (jnp.dot(kvn, wv_ref[...], preferred_element_type=F32) + ones_lane).astype(BF16)
    for h in range(N_HEADS):
        sl = slice(h * MLA_SLOT, (h + 1) * MLA_SLOT)
        q_ref[:, sl] = (qa[:, sl] * c1 + qb[:, sl] * s1).astype(BF16)
        k_ref[:, sl] = (kn[:, sl] + kp).astype(BF16)


def _mla_prep(x, qnw, kvnw, wqa, wqb, wk, wv, ea, eb, c1, s1):
    b, tp, _ = x.shape
    tm = _pick(tp, (640, 128, 64))
    full = lambda a: pl.BlockSpec(a.shape, lambda i, j: (0, 0))
    wide = N_HEADS * MLA_SLOT
    out = pl.BlockSpec((None, tm, wide), lambda i, j: (i, j, 0))
    tab = pl.BlockSpec((tm, MLA_SLOT), lambda i, j: (j, 0))
    return pl.pallas_call(
        _mla_prep_kernel,
        grid=(b, tp // tm),
        in_specs=[pl.BlockSpec((None, tm, W_MLA), lambda i, j: (i, j, 0))]
        + [full(a) for a in (qnw, kvnw, wqa, wqb, wk, wv, ea, eb)] + [tab, tab],
        out_specs=[out, out, out],
        out_shape=[jax.ShapeDtypeStruct((b, tp, wide), BF16)] * 3,
        compiler_params=_cparams("parallel", "parallel"),
        name="mla_prep",
    )(x, qnw, kvnw, wqa, wqb, wk, wv, ea, eb, c1, s1)


FLASH_HEADS = 4


def _flash_kernel(q_ref, k_ref, v_ref, o_ref, m_ref, acc_ref):
    qi = pl.program_id(2)
    t = q_ref.shape[0]
    m_ref[...] = jnp.full(m_ref.shape, NEG_INF, F32)
    acc_ref[...] = jnp.zeros(acc_ref.shape, F32)

    def block(j, diagonal):
        rows = pl.ds(pl.multiple_of(j * t, t), t)
        for h in range(FLASH_HEADS):
            sl = slice(h * MLA_SLOT, (h + 1) * MLA_SLOT)
            s = lax.dot_general(q_ref[:, sl], k_ref[rows, sl], (((1,), (1,)), ((), ())),
                                preferred_element_type=F32)
            if diagonal:
                s = jnp.where(_iota(s.shape, 0) >= _iota(s.shape, 1), s, NEG_INF)
            m_old = m_ref[h]
            m_new = jnp.maximum(m_old, jnp.max(s, axis=-1, keepdims=True))
            p = jnp.concatenate([jnp.exp2((s[:, c * LANE:(c + 1) * LANE] - m_new).astype(BF16))
                                 for c in range(s.shape[1] // LANE)], axis=1)
            acc_ref[h] = (jnp.exp2(m_old - m_new) * acc_ref[h]
                          + jnp.dot(p, v_ref[rows, sl], preferred_element_type=F32))
            m_ref[h] = m_new

    def full_block(j, carry):
        block(j, False)
        return carry

    lax.fori_loop(0, qi, full_block, 0)
    block(qi, True)
    for h in range(FLASH_HEADS):
        acc = acc_ref[h]
        o_ref[:, h * MLA_SLOT:(h + 1) * MLA_SLOT] = (acc / acc[:, MLA_V:MLA_V + 1]).astype(o_ref.dtype)


def _flash(q, k, v):
    b, tp, wide = q.shape
    t = _pick(tp, (640, 128, 64))
    w = FLASH_HEADS * MLA_SLOT
    qspec = pl.BlockSpec((None, t, w), lambda i, h, qi: (i, qi, h))
    kspec = pl.BlockSpec((None, tp, w), lambda i, h, qi: (i, 0, h))
    return pl.pallas_call(
        _flash_kernel,
        grid=(b, wide // w, tp // t),
        in_specs=[qspec, kspec, kspec],
        out_specs=qspec,
        out_shape=jax.ShapeDtypeStruct((b, tp, wide), BF16),
        scratch_shapes=[pltpu.VMEM((FLASH_HEADS, t, LANE), F32), pltpu.VMEM((FLASH_HEADS, t, MLA_SLOT), F32)],
        compiler_params=_cparams("parallel", "parallel", "arbitrary"),
        name="mla_flash",
    )(q, k, v)


def _router_kernel(h_ref, nw_ref, wr_ref, xn_ref, info_ref):
    xn = _rms(h_ref[...], nw_ref[...])
    bits = pltpu.bitcast(xn.astype(BF16).astype(F32), jnp.uint32)
    half = D_MODEL // 2
    xn_ref[...] = (bits[:, :half] >> 16) | bits[:, half:]
    logits = jnp.dot(xn, wr_ref[...], preferred_element_type=F32, precision=lax.Precision.HIGHEST)
    lane = _iota(logits.shape, 1).astype(F32)
    valid = lane < N_EXPERTS
    l0 = jnp.where(valid, logits, NEG_INF)
    m1 = jnp.max(l0, axis=-1, keepdims=True)
    i1 = jnp.min(jnp.where(l0 == m1, lane, float(LANE)), axis=-1, keepdims=True)
    l1 = jnp.where(lane == i1, NEG_INF, l0)
    m2 = jnp.max(l1, axis=-1, keepdims=True)
    i2 = jnp.min(jnp.where(l1 == m2, lane, float(LANE)), axis=-1, keepdims=True)
    e2 = jnp.exp(m2 - m1)
    g1 = 1.0 / (1.0 + e2)
    g2 = e2 / (1.0 + e2)
    info = jnp.where(lane == 0, i1, 0.0)
    info = jnp.where(lane == 1, i2, info)
    info = jnp.where(lane == 2, g1, info)
    info = jnp.where(lane == 3, g2, info)
    info_ref[...] = info


def _router(h, nw, wr):
    n = h.shape[0]
    tm = _pick(n, (512, 256, 128, 64))
    return pl.pallas_call(
        _router_kernel,
        grid=(n // tm,),
        in_specs=[pl.BlockSpec((tm, D_MODEL), lambda i: (i, 0)), pl.BlockSpec(nw.shape, lambda i: (0, 0)),
                  pl.BlockSpec(wr.shape, lambda i: (0, 0))],
        out_specs=[pl.BlockSpec((tm, D_MODEL // 2), lambda i: (i, 0)), pl.BlockSpec((tm, LANE), lambda i: (i, 0))],
        out_shape=[jax.ShapeDtypeStruct((n, D_MODEL // 2), jnp.uint32), jax.ShapeDtypeStruct((n, LANE), F32)],
        compiler_params=_cparams("parallel"),
        name="moe_router",
    )(h, nw, wr)


def _expert_kernel(be_ref, x_ref, gate_ref, wg_ref, wu_ref, wd_ref, out_ref, acc_ref):
    f = pl.program_id(1)

    @pl.when(f == 0)
    def _():
        acc_ref[...] = jnp.zeros(acc_ref.shape, F32)

    packed = x_ref[...]
    low = pltpu.bitcast(packed << 16, F32)
    high = pltpu.bitcast(packed & jnp.uint32(0xFFFF0000), F32)
    xb = jnp.concatenate([low, high], axis=1).astype(BF16)
    a = jnp.dot(xb, wg_ref[...], preferred_element_type=F32)
    b = jnp.dot(xb, wu_ref[...], preferred_element_type=F32)
    acc_ref[...] += jnp.dot((_silu(a) * b).astype(BF16), wd_ref[...], preferred_element_type=F32)

    @pl.when(f == pl.num_programs(1) - 1)
    def _():
        out_ref[...] = acc_ref[...] * gate_ref[...]


def _experts(block_expert, x_rows, row_gate, wg, wu, wd, tm):
    cap = x_rows.shape[0]
    dff = wg.shape[2]
    tf = _pick(dff, (1792, 512, 256, 128))
    grid_spec = pltpu.PrefetchScalarGridSpec(
        num_scalar_prefetch=1,
        grid=(cap // tm, dff // tf),
        in_specs=[pl.BlockSpec((tm, D_MODEL // 2), lambda i, f, be: (i, 0)),
                  pl.BlockSpec((tm, 1), lambda i, f, be: (i, 0)),
                  pl.BlockSpec((None, D_MODEL, tf), lambda i, f, be: (be[i], 0, f)),
                  pl.BlockSpec((None, D_MODEL, tf), lambda i, f, be: (be[i], 0, f)),
                  pl.BlockSpec((None, tf, D_MODEL), lambda i, f, be: (be[i], f, 0))],
        out_specs=pl.BlockSpec((tm, D_MODEL), lambda i, f, be: (i, 0)),
        scratch_shapes=[pltpu.VMEM((tm, D_MODEL), F32)],
    )
    return pl.pallas_call(
        _expert_kernel,
        grid_spec=grid_spec,
        out_shape=jax.ShapeDtypeStruct((cap, D_MODEL), F32),
        compiler_params=_cparams("parallel", "arbitrary"),
        name="moe_experts",
    )(block_expert, x_rows, row_gate, wg, wu, wd)


def _combine_kernel(h_ref, ya_ref, yb_ref, out_ref):
    out_ref[...] = h_ref[...] + (ya_ref[...] + yb_ref[...])


def _combine(h, y2):
    n = h.shape[0]
    tm = _pick(n, (1024, 512, 256, 128, 64))
    spec = pl.BlockSpec((tm, D_MODEL), lambda i: (i, 0))
    second = pl.BlockSpec((tm, D_MODEL), lambda i: (i + n // tm, 0))
    return pl.pallas_call(
        _combine_kernel, grid=(n // tm,), in_specs=[spec, spec, second], out_specs=spec,
        out_shape=jax.ShapeDtypeStruct((n, D_MODEL), F32),
        compiler_params=_cparams("parallel"), name="moe_combine",
    )(h, y2, y2)


def _sc_gather(table, idx):
    rows = idx.shape[0]
    d = table.shape[1]
    info = plsc.get_sparse_core_info()
    workers = info.num_cores * info.num_subcores
    assert rows % (workers * SC_CHUNK) == 0, (rows, workers)
    per_worker = rows // workers
    mesh = plsc.VectorSubcoreMesh(core_axis_name="c", subcore_axis_name="s")

    @functools.partial(
        pl.kernel, mesh=mesh, out_type=jax.ShapeDtypeStruct((rows, d), table.dtype),
        scratch_types=[pltpu.VMEM((SC_CHUNK,), jnp.int32), pltpu.VMEM((SC_CHUNK, d), table.dtype),
                       pltpu.SemaphoreType.DMA],
        name="sc_gather")
    def gather(table_hbm, idx_hbm, out_hbm, idx_v, rows_v, sem):
        base = (lax.axis_index("s") * info.num_cores + lax.axis_index("c")) * per_worker

        @pl.loop(0, per_worker // SC_CHUNK)
        def _(i):
            off = pl.multiple_of(base + i * SC_CHUNK, 8)
            pltpu.sync_copy(idx_hbm.at[pl.ds(off, SC_CHUNK)], idx_v)
            pltpu.async_copy(table_hbm.at[idx_v], rows_v, sem).wait()
            pltpu.sync_copy(rows_v, out_hbm.at[pl.ds(off, SC_CHUNK)])

    return gather(table, idx)


def _moe(h, nw, router, wg, wu, wd, defer_combine=False):
    n = h.shape[0]
    tm = _pick(n, (512, 64))
    wr = jnp.pad(router.astype(F32), ((0, 0), (0, LANE - N_EXPERTS)))
    xn, info = _router(h, nw, wr)
    expert = info[:, 0:2].astype(jnp.int32).reshape(-1)
    gate = info[:, 2:4].reshape(-1)
    n_assign = 2 * n
    order = jnp.argsort(expert)
    onehot = (expert[:, None] == jnp.arange(N_EXPERTS, dtype=jnp.int32)[None, :]).astype(jnp.int32)
    running = jnp.cumsum(onehot, axis=0)
    counts = running[-1]
    padded = (counts + tm - 1) // tm * tm
    pad_end = jnp.cumsum(padded)
    pad_start = pad_end - padded
    start = jnp.cumsum(counts) - counts
    n_blocks = -(-n_assign // tm) + N_EXPERTS
    cap = n_blocks * tm
    block_start = jnp.arange(n_blocks, dtype=jnp.int32) * tm
    block_expert = jnp.minimum(jnp.sum(block_start[:, None] >= pad_end[None, :], axis=1), N_EXPERTS - 1)
    block_expert = block_expert.astype(jnp.int32)
    rank = (block_start - pad_start[block_expert])[:, None] + jnp.arange(tm, dtype=jnp.int32)[None, :]
    valid = (rank < counts[block_expert][:, None]).reshape(cap)
    src = order[jnp.clip(start[block_expert][:, None] + rank, 0, n_assign - 1).reshape(cap)]
    row_token = jnp.where(valid, src // 2, 0)
    row_gate = jnp.where(valid, gate[src], 0.0)
    dest = jnp.sum(onehot * (pad_start[None, :] + running - 1), axis=1)
    x_rows = _sc_gather(xn, row_token)
    y_rows = _experts(block_expert, x_rows, row_gate[:, None], wg, wu, wd, tm)
    y2 = _sc_gather(y_rows, dest.reshape(n, 2).T.reshape(n_assign))
    return (h, y2) if defer_combine else _combine(h, y2)


def _pad_cols(a, width):
    return jnp.pad(a, ((0, 0), (0, width - a.shape[1])))


def _row(a, width=None):
    a = a.reshape(1, -1).astype(F32)
    return a if width is None else _pad_cols(a, width)


def _rope_tables(tp):
    pos = jnp.arange(tp, dtype=F32)
    inv_freq = ROPE_THETA ** (-jnp.arange(0, MLA_ROPE, 2, dtype=F32) / MLA_ROPE)
    ang = pos[:, None] * inv_freq[None, :]
    cos, sin = jnp.cos(ang), jnp.sin(ang)
    ones = jnp.ones((tp, MLA_NOPE), F32)
    zeros = jnp.zeros((tp, MLA_SLOT - MLA_NOPE - MLA_ROPE), F32)
    c1 = jnp.concatenate([ones, cos, cos, zeros], axis=1)
    s1 = jnp.concatenate([0.0 * ones, -sin, sin, zeros], axis=1)
    return c1, s1


def _mla_weights(w_uq, w_ukv):
    half = MLA_ROPE // 2
    scale = (MLA_NOPE + MLA_ROPE) ** -0.5 * math.log2(math.e)
    zq =jnp.zeros((w_uq.shape[0], MLA_SLOT - MLA_NOPE - MLA_ROPE), F32)
    zn = jnp.zeros((w_uq.shape[0], MLA_NOPE), F32)
    zk = jnp.zeros((w_ukv.shape[0], MLA_SLOT - MLA_NOPE), F32)
    wqa, wqb, wk, wv = [], [], [], []
    for h in range(N_HEADS):
        q = w_uq[:, h * 96:(h + 1) * 96] * scale
        nope, x1, x2 = q[:, :MLA_NOPE], q[:, MLA_NOPE:MLA_NOPE + half], q[:, MLA_NOPE + half:]
        wqa += [nope, x1, x2, zq]
        wqb += [zn, x2, x1, zq]
        kv = w_ukv[:, h * 128:(h + 1) * 128]
        wk += [kv[:, :MLA_NOPE], zk]
        wv += [kv[:, MLA_NOPE:], zk]
    cat = lambda parts: jnp.concatenate(parts, axis=1).astype(BF16)
    ea = np.zeros((MLA_SLOT, MLA_SLOT), np.float32)
    eb = np.zeros((MLA_SLOT, MLA_SLOT), np.float32)
    for i in range(MLA_ROPE):
        ea[i, MLA_NOPE + i] = 1.0
        eb[(i + half) % MLA_ROPE, MLA_NOPE + i] = 1.0
    return cat(wqa), cat(wqb), cat(wk), cat(wv), jnp.asarray(ea, BF16), jnp.asarray(eb, BF16)


def _branch_weights(w_branch):
    wb = w_branch.astype(BF16)
    z = jnp.zeros((MLA_SLOT - MLA_V, D_MODEL), BF16)
    parts = []
    for h in range(N_HEADS):
        parts += [wb[1, h * MLA_V:(h + 1) * MLA_V], z]
    return wb[0], jnp.concatenate(parts, axis=0), wb[2], wb[3]


def _token_mixing(h, b, tp, p, l, tables):
    w_in = p["w_in"][l]
    col = lambda i, j: w_in[:, _OFF[i]:_OFF[j]]
    wg = jnp.concatenate([col(0, 3), col(4, 5), _pad_cols(col(3, 4), LANE)], axis=1).astype(BF16)
    wm = jnp.concatenate([col(5, 7), _pad_cols(col(7, 8), LANE)], axis=1).astype(BF16)
    wd = jnp.concatenate([col(8, 10), _pad_cols(col(10, 12), LANE)], axis=1).astype(BF16)
    wr = _pad_cols(col(12, 13), W_RWKV).astype(BF16)
    nw = _row(p["norm_mix"][l])
    xg, xm, xd, xr = _inproj(h, nw, wg, wm, wd, wr)
    shape3 = lambda a: a.reshape(b, tp, a.shape[1])

    gup = jnp.pad(p["gla_gate_up"][l], ((0, LANE - 16), (0, 0))).astype(BF16)
    y_gla = _gla(shape3(xg), gup, _row(p["gla_gate_bias"][l]), _row(jnp.tile(p["gla_norm"][l], N_HEADS)))

    wqa, wqb, wk, wv, ea, eb = _mla_weights(p["mla_w_uq"][l], p["mla_w_ukv"][l])
    q, k, v = _mla_prep(shape3(xm), _row(p["mla_q_norm"][l]), _row(p["mla_kv_norm"][l]),
                        wqa, wqb, wk, wv, ea, eb, *tables)
    y_mla = _flash(q, k, v)

    lanes4 = lambda a: jnp.pad(a.reshape(1, N_HEADS).astype(F32), ((0, 0), (N_HEADS, LANE - 2 * N_HEADS)))
    y_gdn = _gdn(shape3(xd), p["gdn_conv"][l].astype(F32), lanes4(p["gdn_a_log"][l]),
                 lanes4(p["gdn_dt_bias"][l]), _row(jnp.tile(p["gdn_norm"][l], N_HEADS)))

    w2a2 = jnp.zeros((LANE, 512), F32)
    w2a2 = w2a2.at[0:64, 0:256].set(p["rwkv_w2"][l]).at[64:128, 256:512].set(p["rwkv_a2"][l]).astype(BF16)
    g2 = jnp.pad(p["rwkv_g2"][l], ((0, 256 - 160), (0, 0))).astype(BF16)
    y_rwkv = _rwkv(shape3(xr), _row(p["rwkv_mu"][l], W_RWKV), w2a2, _row(p["rwkv_w0"][l]),
                   _row(p["rwkv_a0"][l]), g2, _row(p["rwkv_k_k"][l]), _row(p["rwkv_k_a"][l]),
                   _row(p["rwkv_r_k"][l]), _row(p["rwkv_ln_w"][l]), _row(p["rwkv_ln_b"][l]))

    flat = lambda a: a.reshape(b * tp, a.shape[2])
    ys = [flat(y_gla), flat(y_mla), flat(y_gdn), flat(y_rwkv)]
    return _merge(h, nw, col(13, 14).astype(BF16), ys, _branch_weights(p["w_branch"][l]),
                  p["w_out"][l].astype(BF16))


def kernel(x, meta_tokens, norm_mix, w_in, gla_gate_up, gla_gate_bias, gla_norm, mla_q_norm, mla_w_uq, mla_kv_norm, mla_w_ukv, gdn_conv, gdn_a_log, gdn_dt_bias, gdn_norm, rwkv_mu, rwkv_w0, rwkv_w2, rwkv_a0, rwkv_a2, rwkv_g2, rwkv_k_k, rwkv_k_a, rwkv_r_k, rwkv_ln_w, rwkv_ln_b, w_branch, w_out, norm_ffn, ffn_w_gate, ffn_w_up, ffn_w_down, moe_router, moe_w_gate, moe_w_up, moe_w_down, norm_final):
    p = dict(norm_mix=norm_mix, w_in=w_in, gla_gate_up=gla_gate_up, gla_gate_bias=gla_gate_bias,
             gla_norm=gla_norm, mla_q_norm=mla_q_norm, mla_w_uq=mla_w_uq, mla_kv_norm=mla_kv_norm,
             mla_w_ukv=mla_w_ukv, gdn_conv=gdn_conv, gdn_a_log=gdn_a_log, gdn_dt_bias=gdn_dt_bias,
             gdn_norm=gdn_norm, rwkv_mu=rwkv_mu, rwkv_w0=rwkv_w0, rwkv_w2=rwkv_w2, rwkv_a0=rwkv_a0,
             rwkv_a2=rwkv_a2, rwkv_g2=rwkv_g2, rwkv_k_k=rwkv_k_k, rwkv_k_a=rwkv_k_a, rwkv_r_k=rwkv_r_k,
             rwkv_ln_w=rwkv_ln_w, rwkv_ln_b=rwkv_ln_b, w_branch=w_branch, w_out=w_out)
    b, seq, d = x.shape
    t_real = N_META + seq
    tp = -(-t_real // 128) * 128
    meta = jnp.broadcast_to(meta_tokens[None].astype(x.dtype), (b, N_META, d))
    h = jnp.concatenate([meta, x, jnp.zeros((b, tp - t_real, d), x.dtype)], axis=1).reshape(b * tp, d)
    tables = _rope_tables(tp)
    depth = norm_mix.shape[0]
    y2 = None
    for l in range(depth):
        h = _token_mixing(h, b, tp, p, l, tables)
        nw = _row(norm_ffn[l])
        if l % 2 == 0:
            h = _ffn(h, nw, ffn_w_gate[l // 2].astype(BF16), ffn_w_up[l // 2].astype(BF16),
                     ffn_w_down[l // 2].astype(BF16))
        elif l == depth - 1:
            h, y2 = _moe(h, nw, moe_router[l // 2], moe_w_gate[l // 2].astype(BF16),
                         moe_w_up[l // 2].astype(BF16), moe_w_down[l // 2].astype(BF16), defer_combine=True)
        else:
            h = _moe(h, nw, moe_router[l // 2], moe_w_gate[l // 2].astype(BF16),
                     moe_w_up[l // 2].astype(BF16), moe_w_down[l // 2].astype(BF16))
    return _final(h, _row(norm_final), b, tp, seq, y2)
```

```python
import functools
import math

import jax
import jax.numpy as jnp
import numpy as np
from jax import lax
from jax.experimental import pallas as pl
from jax.experimental.pallas import tpu as pltpu
from jax.experimental.pallas import tpu_sc as plsc

F32 = jnp.float32
BF16 = jnp.bfloat16

D_MODEL = 1024
N_META = 16
N_HEADS = 4
GLA_DK = 32
GLA_DV = 64
GLA_TAU = 16.0
MLA_NOPE = 64
MLA_ROPE = 32
MLA_V = 64
MLA_SLOT = 128
ROPE_THETA = 10000.0
GDN_DK = 64
GDN_CONV = 4
RWKV_N = 64
RWKV_LN_EPS = RWKV_N * 1e-5
CHUNK = 64
SUB = 16
PREP_GROUP = 5
GLA_STEP = (2, 320)
DELTA_STEP = (4, 128)
N_EXPERTS = 8
SC_CHUNK = 64
NORM_EPS = 1e-6
L2_EPS = 1e-6
NEG_INF = -1e30
EXP_CLAMP = 80.0

LANE = 128
VMEM_LIMIT = 56 * 1024 * 1024

_OFF = np.cumsum([0, 128, 128, 256, 16, 256, 256, 128, 32, 768, 256, 4, 4, 1056, 4096]).tolist()
W_GLA, W_MLA, W_GDN, W_RWKV = 896, 512, 1152, 1152


def _cparams(*sem):
    return pltpu.CompilerParams(dimension_semantics=sem, vmem_limit_bytes=VMEM_LIMIT)


def _pick(n, prefs):
    for p in prefs:
        if n % p == 0:
            return p
    raise ValueError(f"no tile for {n}")


def _dot(a, b):
    return jnp.dot(a.astype(BF16), b.astype(BF16), preferred_element_type=F32)


def _dot_nt(a, b):
    return lax.dot_general(a.astype(BF16), b.astype(BF16), (((1,), (1,)), ((), ())),
                           preferred_element_type=F32)


def _dot_tn(a, b):
    return lax.dot_general(a.astype(BF16), b.astype(BF16), (((0,), (0,)), ((), ())),
                           preferred_element_type=F32)


def _split3(x):
    hi = x.astype(BF16)
    r1 = x - hi.astype(F32)
    mid = r1.astype(BF16)
    lo = (r1 - mid.astype(F32)).astype(BF16)
    return hi, mid, lo


def _dot01_left(m01, x):
    return sum(jnp.dot(m01, p, preferred_element_type=F32) for p in _split3(x))


def _dot01_right(x, m01):
    return sum(jnp.dot(p, m01, preferred_element_type=F32) for p in _split3(x))


def _seg_sum(x, m01):
    hi = x.astype(BF16)
    lo = (x - hi.astype(F32)).astype(BF16)
    return jnp.dot(hi, m01, preferred_element_type=F32) + jnp.dot(lo, m01, preferred_element_type=F32)


def _iota(shape, dim):
    return lax.broadcasted_iota(jnp.int32, shape, dim)


def _div(x, w):
    return x >> int(math.log2(w))


def _ltri(n):
    return (_iota((n, n), 0) >= _iota((n, n), 1)).astype(BF16)


def _head_ones(n, w):
    return (_div(_iota((n, n), 0), w) == _div(_iota((n, n), 1), w)).astype(BF16)


def _lane_masks(width, w):
    lane = _div(_iota((1, width), 1), w)
    return [(lane == h).astype(F32) for h in range(width // w)]


def _stack(x, masks):
    return jnp.concatenate([x * m for m in masks], axis=0)


def _unstack(y, n):
    out = y[0:n]
    for h in range(1, y.shape[0] // n):
        out = out + y[h * n:(h + 1) * n]
    return out


def _rms(x, w):
    return x * lax.rsqrt(jnp.mean(x * x, axis=-1, keepdims=True) + NORM_EPS) * w


def _sigmoid(x):
    return 1.0 / (1.0 + jnp.exp(-x))


def _silu(x):
    return x * _sigmoid(x)


def _softplus(x):
    return jnp.maximum(x, 0.0) + jnp.log(1.0 + jnp.exp(-jnp.abs(x)))


def _neumann_inverse(xs, eye):
    n = xs[0].shape[0]
    ts = [eye + x for x in xs]
    ps = [_dot(x, x) for x in xs]
    for _ in range(int(math.log2(CHUNK)) - 2):
        yield
        both = [_dot(jnp.concatenate([t, p], axis=0), p) for t, p in zip(ts, ps)]
        ts = [t + tp[:n] for t, tp in zip(ts, both)]
        ps = [tp[n:] for tp in both]
    yield
    return [t + _dot(t, p) for t, p in zip(ts, ps)]


def _for_chunks(n, body, group):
    def trip(i, carry):
        body([i * group + g for g in range(group)])
        return carry

    if n >= group:
        lax.fori_loop(0, n // group, trip, 0)
    if n % group:
        body(list(range(n - n % group, n)))


def _inproj_kernel(h_ref, nw_ref, wg_ref, wm_ref, wd_ref, wr_ref, og_ref, om_ref, od_ref, or_ref):
    xb = _rms(h_ref[...], nw_ref[...]).astype(BF16)
    og_ref[...] = jnp.dot(xb, wg_ref[...], preferred_element_type=F32)
    om_ref[...] = jnp.dot(xb, wm_ref[...], preferred_element_type=F32)
    od_ref[...] = jnp.dot(xb, wd_ref[...], preferred_element_type=F32)
    or_ref[...] = jnp.dot(xb, wr_ref[...], preferred_element_type=F32)


def _inproj(h, nw, wg, wm, wd, wr):
    n = h.shape[0]
    tm = _pick(n, (512, 256, 128, 64))
    full = lambda a: pl.BlockSpec(a.shape, lambda i: (0, 0))
    row = lambda w: pl.BlockSpec((tm, w), lambda i: (i, 0))
    return pl.pallas_call(
        _inproj_kernel,
        grid=(n // tm,),
        in_specs=[row(D_MODEL), full(nw), full(wg), full(wm), full(wd), full(wr)],
        out_specs=[row(W_GLA), row(W_MLA), row(W_GDN), row(W_RWKV)],
        out_shape=[jax.ShapeDtypeStruct((n, w), F32) for w in (W_GLA, W_MLA, W_GDN, W_RWKV)],
        compiler_params=_cparams("parallel"),
        name="inproj",
    )(h, nw, wg, wm, wd, wr)


def _merge_kernel(h_ref, nw_ref, wgate_ref, yg_ref, ym_ref, yd_ref, yr_ref,
                  wbg_ref, wbm_ref, wbd_ref, wbr_ref, wout_ref, out_ref):
    x = h_ref[...]
    xb = _rms(x, nw_ref[...]).astype(BF16)
    acc = jnp.zeros(x.shape, F32)
    branches = ((yg_ref, wbg_ref), (ym_ref, wbm_ref), (yd_ref, wbd_ref), (yr_ref, wbr_ref))
    for i, (y_ref, wb_ref) in enumerate(branches):
        logits = jnp.dot(xb, wgate_ref[:, i * D_MODEL:(i + 1) * D_MODEL], preferred_element_type=F32)
        proj = jnp.dot(y_ref[...], wb_ref[...], preferred_element_type=F32)
        acc = acc + _sigmoid(logits) * proj
    out_ref[...] = x + jnp.dot(acc.astype(BF16), wout_ref[...], preferred_element_type=F32)


def _merge(h, nw, wgate, ys, wbs, wout):
    n = h.shape[0]
    tm = _pick(n, (512, 256, 128, 64))
    full = lambda a: pl.BlockSpec(a.shape, lambda i: (0, 0))
    row = lambda w: pl.BlockSpec((tm, w), lambda i: (i, 0))
    return pl.pallas_call(
        _merge_kernel,
        grid=(n // tm,),
        in_specs=[row(D_MODEL), full(nw), full(wgate)] + [row(y.shape[1]) for y in ys]
        + [full(w) for w in wbs] + [full(wout)],
        out_specs=row(D_MODEL),
        out_shape=jax.ShapeDtypeStruct((n, D_MODEL), F32),
        compiler_params=_cparams("parallel"),
        name="merge",
    )(h, nw, wgate, *ys, *wbs, wout)


def _ffn_kernel(h_ref, nw_ref, wg_ref, wu_ref, wd_ref, out_ref, xb_ref, acc_ref):
    f = pl.program_id(1)

    @pl.when(f == 0)
    def _():
        xb_ref[...] = _rms(h_ref[...], nw_ref[...]).astype(BF16)
        acc_ref[...] = jnp.zeros(acc_ref.shape, F32)

    xb = xb_ref[...]
    a = jnp.dot(xb, wg_ref[...], preferred_element_type=F32)
    b = jnp.dot(xb, wu_ref[...], preferred_element_type=F32)
    acc_ref[...] += jnp.dot((_silu(a) * b).astype(BF16), wd_ref[...], preferred_element_type=F32)

    @pl.when(f == pl.num_programs(1) - 1)
    def _():
        out_ref[...] = h_ref[...] + acc_ref[...]


def _ffn(h, nw, wg, wu, wd):
    n = h.shape[0]
    dff = wg.shape[1]
    tm = _pick(n, (512, 256, 128, 64))
    tf = _pick(dff, (1408, 512, 256, 128))
    return pl.pallas_call(
        _ffn_kernel,
        grid=(n // tm, dff // tf),
        in_specs=[pl.BlockSpec((tm, D_MODEL), lambda i, f: (i, 0)),
                  pl.BlockSpec(nw.shape, lambda i, f: (0, 0)),
                  pl.BlockSpec((D_MODEL, tf), lambda i, f: (0, f)),
                  pl.BlockSpec((D_MODEL, tf), lambda i, f: (0, f)),
                  pl.BlockSpec((tf, D_MODEL), lambda i, f: (f, 0))],
        out_specs=pl.BlockSpec((tm, D_MODEL), lambda i, f: (i, 0)),
        out_shape=jax.ShapeDtypeStruct((n, D_MODEL), F32),
        scratch_shapes=[pltpu.VMEM((tm, D_MODEL), BF16), pltpu.VMEM((tm, D_MODEL), F32)],
        compiler_params=_cparams("parallel", "arbitrary"),
        name="ffn",
    )(h, nw, wg, wu, wd)


def _final_kernel(nw_ref, *refs):
    out_ref = refs[-1]
    x = refs[0][...]
    for extra in refs[1:-1]:
        x = x + extra[...]
    out_ref[...] = _rms(x, nw_ref[...])


def _final(h, nw, b, tp, seq, y2=None):
    tq = _pick(seq, (1024, 512, 256, 128, 64))
    n = b * tp
    rows = lambda shift: pl.BlockSpec((pl.Element(tq), pl.Element(D_MODEL)),
                                      lambda i, j: (pl.multiple_of(shift + i * tp + N_META + j * tq, 8), 0))
    operands = [h] if y2 is None else [h, y2, y2]
    shifts = [0] if y2 is None else [0, 0, n]
    return pl.pallas_call(
        _final_kernel,
        grid=(b, seq // tq),
        in_specs=[pl.BlockSpec(nw.shape, lambda i, j: (0, 0))] + [rows(s) for s in shifts],
        out_specs=pl.BlockSpec((None, tq, D_MODEL), lambda i, j: (i, j, 0)),
        out_shape=jax.ShapeDtypeStruct((b, seq, D_MODEL), F32),
        compiler_params=_cparams("parallel", "parallel"),
        name="final_norm",
    )(nw, *operands)


_DONE = object()


def _interleave(generators):
    live = list(generators)
    while live:
        live = [g for g in live if next(g, _DONE) is not _DONE]


def _mixer_kernel(body, n_params, x_ref, *refs):
    params, y_ref, scratch = refs[:n_params], refs[n_params], refs[n_params + 1:]
    parts = [body(x_ref.at[b], *params, y_ref.at[b], *[s.at[b] for s in scratch]) for b in range(x_ref.shape[0])]
    n_chunks = x_ref.shape[1] // CHUNK
    if n_chunks <= PREP_GROUP:
        half = max(len(parts) // 2, 1)
        pending = []
        for group in [parts[i:i + half] for i in range(0, len(parts), half)]:
            _interleave([preprocess() for preprocess, _, _, _ in group] + pending)
            pending = [prepare(list(range(n_chunks))) for _, prepare, _, _ in group]
        _interleave(pending)
    else:
        for preprocess, _, _, _ in parts:
            _interleave([preprocess()])
        _for_chunks(n_chunks, lambda cis: [_interleave([prepare(cis)]) for _, prepare, _, _ in parts], PREP_GROUP)
    _for_chunks(n_chunks, lambda cis: [_interleave([advance(ci) for _, _, advance, _ in parts]) for ci in cis], 1)
    for _, _, _, finish in parts:
        finish()


def _mixer_block(tp, rows):
    return _pick(tp, (rows, 128, 64))


def _mixer_call(body, name, x, params, scratch, tb, nb):
    b, tp, width = x.shape
    nb = nb if b % nb == 0 else 1
    full = lambda a: pl.BlockSpec(a.shape, lambda i, j: (0, 0))
    return pl.pallas_call(
        functools.partial(_mixer_kernel, body, len(params)),
        grid=(b // nb, tp // tb),
        in_specs=[pl.BlockSpec((nb, tb, width), lambda i, j: (i, j, 0))] + [full(a) for a in params],
        out_specs=pl.BlockSpec((nb, tb, 256), lambda i, j: (i, j, 0)),
        out_shape=jax.ShapeDtypeStruct((b, tp, 256), BF16),
        scratch_shapes=[pltpu.VMEM((nb,) + shape, dtype) for shape, dtype in scratch],
        compiler_params=_cparams("parallel", "arbitrary"),
        name=name,
    )(x, *params)


def _gla_body(x_ref, gup_ref, gb_ref, nw_ref, y_ref, st_ref, la_ref, o_s, qg_s, kv_s, gt_s):
    @pl.when(pl.program_id(1) == 0)
    def _():
        st_ref[...] = jnp.zeros(st_ref.shape, F32)

    def preprocess():
        z = _dot(x_ref[:, 768:896], gup_ref[...]) + gb_ref[...]
        yield
        la_ref[...] = -_softplus(-z) * (1.0 / GLA_TAU)

    ltri = _ltri(CHUNK)
    qmasks = _lane_masks(N_HEADS * GLA_DK, GLA_DK)
    vmasks = _lane_masks(N_HEADS * GLA_DV, GLA_DV)
    bd = (_div(_iota((256, 128), 0), GLA_DV) == _div(_iota((256, 128), 1), GLA_DK)).astype(F32)
    hsum = _head_ones(N_HEADS * GLA_DV, GLA_DV)
    nw = nw_ref[...]

    n4 = N_HEADS * GLA_DV

    def prepare(cis):
        n = range(len(cis))
        rows = [pl.ds(pl.multiple_of(ci * CHUNK, CHUNK), CHUNK) for ci in cis]
        g = [_dot01_left(ltri, la_ref[r, :]) for r in rows]
        q = [x_ref[r, 0:128] * (GLA_DK ** -0.5) for r in rows]
        k = [x_ref[r, 128:256] for r in rows]
        v = [x_ref[r, 256:512] for r in rows]
        intra = [[] for _ in n]
        for s in range(CHUNK // SUB):
            lo, hi = s * SUB, (s + 1) * SUB
            sc = []
            for i in n:
                gs = jnp.zeros((1, 128), F32) if s == 0 else g[i][lo - 1:lo]
                qs = q[i][lo:hi] * jnp.exp(g[i][lo:hi] - gs)
                kt = k[i][:hi] * jnp.exp(jnp.minimum(gs - g[i][:hi], EXP_CLAMP))
                sc.append(_dot_nt(_stack(qs, qmasks), kt))
            causal = _iota(sc[0].shape, 1) <= lo + (_iota(sc[0].shape, 0) & (SUB - 1))
            yield
            p = [_dot(jnp.where(causal, sc[i], 0.0), v[i][:hi]) for i in n]
            for i in n:
                intra[i].append(sum(p[i][h * SUB:(h + 1) * SUB] * vmasks[h] for h in range(N_HEADS)))
            yield
        for i in n:
            g_last = g[i][CHUNK - 1:CHUNK]
            o_s[rows[i], :] = jnp.concatenate(intra[i], axis=0)
            qg_s[rows[i], :] = (q[i] * jnp.exp(g[i])).astype(BF16)
            kv_s[pl.ds(pl.multiple_of(cis[i] * n4, n4), n4), :] = bd * _dot_tn(v[i], k[i] * jnp.exp(g_last - g[i]))
            gt_s[pl.ds(pl.multiple_of(cis[i] * 8, 8), 8), :] = jnp.broadcast_to(jnp.exp(g_last), (8, 128))

    def advance(ci):
        rows = pl.ds(pl.multiple_of(ci * CHUNK, CHUNK), CHUNK)
        st = st_ref[...]
        inter = _dot_nt(qg_s[rows, :], st)
        yield
        o_s[rows, :] += inter
        st_ref[...] = (st * gt_s[pl.ds(pl.multiple_of(ci * 8, 8), 1), :]
                       + kv_s[pl.ds(pl.multiple_of(ci * n4, n4), n4), :])

    def finish():
        o = o_s[...]
        ms = _seg_sum(o * o, hsum) * (1.0 / GLA_DV)
        y_ref[...] = (o * lax.rsqrt(ms + NORM_EPS) * nw * _silu(x_ref[:, 512:768])).astype(y_ref.dtype)

    return preprocess, prepare, advance, finish


def _gla(x, gup, gb, nw):
    nb, rows = GLA_STEP
    tb = _mixer_block(x.shape[1], rows)
    nc = tb // CHUNK
    scratch = [((256, 128), F32), ((tb, 128), F32), ((tb, 256), F32), ((tb, 128), BF16),
               ((nc * N_HEADS * GLA_DV, 128), F32), ((nc * 8, 128), F32)]
    return _mixer_call(_gla_body, "gla", x, (gup, gb, nw), scratch, tb, nb)


def _gdn_body(x_ref, cw_ref, alog_ref, dtb_ref, nw_ref, y_ref,
              s_ref, xp_ref, q_ref, k_ref, v_ref, beta_ref, gd_ref,
              u_s, w_s, attn_s, qd_s, ke_s, gt_s, o_s):
    tb = x_ref.shape[0]
    first = pl.program_id(1) == 0

    @pl.when(first)
    def _():
        s_ref[...] = jnp.zeros(s_ref.shape, F32)
        xp_ref[0:8, :] = jnp.zeros((8, 768), F32)

    @pl.when(jnp.logical_not(first))
    def _():
        xp_ref[0:8, :] = xp_ref[tb:tb + 8, :]

    hsum = _head_ones(256, GDN_DK)

    def preprocess():
        xp_ref[8:tb + 8, :] = x_ref[:, 0:768]
        yield
        conv = cw_ref[0:1, :] * xp_ref[8 - (GDN_CONV - 1):8 - (GDN_CONV - 1) + tb, :]
        for j in range(1, GDN_CONV):
            yield
            conv = conv + cw_ref[j:j + 1, :] * xp_ref[8 - (GDN_CONV - 1) + j:8 - (GDN_CONV - 1) + j + tb, :]
        yield
        c = _silu(conv)
        q = c[:, 0:256]
        k = c[:, 256:512]
        yield
        q_ref[...] = q * lax.rsqrt(_seg_sum(q * q, hsum) + L2_EPS) * (GDN_DK ** -0.5)
        yield
        k_ref[...] = k * lax.rsqrt(_seg_sum(k * k, hsum) + L2_EPS)
        v_ref[...] = c[:, 512:768]
        yield
        gates = x_ref[:, 1024:1152]
        beta_ref[...] = _sigmoid(gates)
        gd_ref[...] = -jnp.exp(alog_ref[...]) * _softplus(gates + dtb_ref[...])

    ltri = _ltri(CHUNK)
    masks = _lane_masks(256, GDN_DK)
    expand = lambda off: (_iota((128, 256), 0) == _div(_iota((128, 256), 1), GDN_DK) + off).astype(BF16)
    exp_beta, exp_g = expand(0), expand(N_HEADS)
    r = _iota((256, 256), 0)
    cidx = _iota((256, 256), 1)
    same = _div(r, CHUNK) == _div(cidx, CHUNK)
    incl = jnp.logical_and(same, r >= cidx)
    strict = jnp.logical_and(same, r > cidx)
    eye = (r == cidx).astype(F32)
    bd = same.astype(F32)
    nw = nw_ref[...]

    def prepare(cis):
        n = range(len(cis))
        rows = [pl.ds(pl.multiple_of(ci * CHUNK, CHUNK), CHUNK) for ci in cis]
        srows = [pl.ds(pl.multiple_of(ci * (N_HEADS * CHUNK), N_HEADS * CHUNK), N_HEADS * CHUNK) for ci in cis]
        q = [q_ref[r, :] for r in rows]
        k = [k_ref[r, :] for r in rows]
        v = [v_ref[r, :] for r in rows]
        bexp = [_dot01_right(beta_ref[r, :], exp_beta) for r in rows]
        gcum = [_dot01_left(ltri, gd_ref[r, :]) for r in rows]
        yield
        gexp = [_dot01_right(g, exp_g) for g in gcum]
        kst = [_stack(x, masks) for x in k]
        kb = [k[i] * bexp[i] for i in n]
        n4 = N_HEADS * CHUNK
        yield
        both = [_dot_nt(jnp.concatenate([_stack(kb[i], masks), _stack(q[i], masks)], axis=0), kst[i]) for i in n]
        kk = [x[:n4] for x in both]
        qk = [x[n4:] for x in both]
        yield
        dec = []
        for g in gexp:
            gcol = jnp.sum(_stack(g, masks), axis=1, keepdims=True) * (1.0 / GDN_DK)
            grow = jnp.sum(eye * gcol, axis=0, keepdims=True)
            dec.append(jnp.exp(jnp.minimum(gcol - grow, 0.0)))
        for i in n:
            attn_s[srows[i], :] = jnp.where(incl, qk[i] * dec[i], 0.0).astype(BF16)
        yield
        t = yield from _neumann_inverse([-jnp.where(strict, kk[i] * dec[i], 0.0) for i in n], eye)
        yield
        u = [_dot(t[i], _stack(v[i] * bexp[i], masks)) for i in n]
        w = [_dot(t[i], _stack(kb[i] * jnp.exp(gexp[i]), masks)) for i in n]
        yield
        for i in n:
            u_s[rows[i], :] = _unstack(u[i], CHUNK)
            w_s[rows[i], :] = _unstack(w[i], CHUNK).astype(BF16)
            g_last = gexp[i][CHUNK - 1:CHUNK]
            qd_s[rows[i], :] = (q[i] * jnp.exp(gexp[i])).astype(BF16)
            ke_s[rows[i], :] = (k[i] * jnp.exp(g_last - gexp[i])).astype(BF16)
            gt_s[pl.ds(pl.multiple_of(cis[i] * 8, 8), 8), :] = jnp.broadcast_to(jnp.exp(g_last), (8, 256))

    def advance(ci):
        rows = pl.ds(pl.multiple_of(ci * CHUNK, CHUNK), CHUNK)
        srows = pl.ds(pl.multiple_of(ci * (N_HEADS * CHUNK), N_HEADS * CHUNK), N_HEADS * CHUNK)
        s = s_ref[...]
        sb = s.astype(BF16)
        ws = jnp.dot(w_s[rows, :], sb, preferred_element_type=F32)
        qs = jnp.dot(qd_s[rows, :], sb, preferred_element_type=F32)
        yield
        v_new = u_s[rows, :] - ws
        av = _dot(attn_s[srows, :], _stack(v_new, masks))
        kv = _dot_tn(ke_s[rows, :], v_new)
        yield
        o_s[rows, :] = qs + _unstack(av, CHUNK)
        s_ref[...] = s * gt_s[pl.ds(pl.multiple_of(ci * 8, 8), 1), :] + bd * kv

    def finish():
        o = o_s[...]
        ms = _seg_sum(o * o, hsum) * (1.0 / GDN_DK)
        y_ref[...] = (o * lax.rsqrt(ms + NORM_EPS) * nw * _silu(x_ref[:, 768:1024])).astype(y_ref.dtype)

    return preprocess, prepare, advance, finish


def _gdn(x, cw, alog, dtb, nw):
    nb, rows = DELTA_STEP
    tb = _mixer_block(x.shape[1], rows)
    nc = tb // CHUNK
    scratch = [((256, 256), F32), ((tb + 8, 768), F32),
               ((tb, 256), F32), ((tb, 256), F32), ((tb, 256), F32), ((tb, 128), F32), ((tb, 128), F32),
               ((tb, 256), F32), ((tb, 256), BF16), ((nc * N_HEADS * CHUNK, 256), BF16),
               ((tb, 256), BF16), ((tb, 256), BF16), ((nc * 8, 256), F32), ((tb, 256), F32)]
    return _mixer_call(_gdn_body, "gdn", x, (cw, alog, dtb, nw), scratch, tb, nb)


def _rwkv_body(x_ref, mu_ref, w2a2_ref, w0_ref, a0_ref, g2_ref, kk_ref, ka_ref, rk_ref,
               lnw_ref, lnb_ref, y_ref,
               s_ref, xp_ref, r_s, k_s, v_s, kk_s, b_s, lw_s, g_s,
               at_s, z_s, arb_s, yv_s, rt_s, be_s, vk_s, gt_s, y_s):
    tb = x_ref.shape[0]
    first = pl.program_id(1) == 0

    @pl.when(first)
    def _():
        s_ref[...] = jnp.zeros(s_ref.shape, F32)
        xp_ref[0:8, :] = jnp.zeros((8, W_RWKV), F32)

    @pl.when(jnp.logical_not(first))
    def _():
        xp_ref[0:8, :] = xp_ref[tb:tb + 8, :]

    hsum = _head_ones(256, RWKV_N)

    def preprocess():
        x = x_ref[...]
        xp_ref[8:tb + 8, :] = x
        yield
        z = x + (xp_ref[7:tb + 7, :] - x) * mu_ref[...]
        r = z[:, 0:256]
        k = z[:, 256:512]
        wa = z[:, 768:896]
        wa = jnp.where(_iota(wa.shape, 1) < 64, jnp.tanh(wa), wa)
        yield
        pre = _dot(wa, w2a2_ref[...])
        yield
        w_log = -_softplus(-(w0_ref[...] + pre[:, 0:256])) - 0.5
        lw_s[...] = -jnp.exp(w_log)
        yield
        a = _sigmoid(a0_ref[...] + pre[:, 256:512])
        kkv = k * kk_ref[...]
        yield
        kkn = kkv * lax.rsqrt(_seg_sum(kkv * kkv, hsum) + L2_EPS)
        yield
        r_s[...] = r
        k_s[...] = k * (1.0 + (a - 1.0) * ka_ref[...])
        v_s[...] = z[:, 512:768]
        kk_s[...] = kkn
        b_s[...] = kkn * a
        yield
        g_s[...] = _dot(_sigmoid(z[:, 896:1152]), g2_ref[...])

    ltri = _ltri(CHUNK)
    masks = _lane_masks(256, RWKV_N)
    rr = _iota((256, 256), 0)
    cc = _iota((256, 256), 1)
    same = _div(rr, CHUNK) == _div(cc, CHUNK)
    incl = jnp.logical_and(same, rr >= cc)
    strict = jnp.logical_and(same, rr > cc)
    eye = (rr == cc).astype(F32)
    bd = same.astype(F32)
    rk = rk_ref[...]
    lnw = lnw_ref[...]
    lnb = lnb_ref[...]

    n4 = N_HEADS * CHUNK

    def prepare(cis):
        n = range(len(cis))
        rows = [pl.ds(pl.multiple_of(ci * CHUNK, CHUNK), CHUNK) for ci in cis]
        srows = [pl.ds(pl.multiple_of(ci * n4, n4), n4) for ci in cis]
        lw = [lw_s[r, :] for r in rows]
        gl = [_dot01_left(ltri, x) for x in lw]
        v = [v_s[r, :] for r in rows]
        k = [k_s[r, :] for r in rows]
        b = [b_s[r, :] for r in rows]
        yield
        e_neg = [jnp.exp(-g) for g in gl]
        a_st = [_stack(-kk_s[rows[i], :] * jnp.exp(gl[i] - lw[i]), masks) for i in n]
        r_t = [r_s[rows[i], :] * jnp.exp(gl[i]) for i in n]
        yield
        amat = []
        for i in n:
            lhs = jnp.concatenate([a_st[i], _stack(r_t[i], masks)], axis=0)
            rhs = jnp.concatenate([_stack(b[i] * e_neg[i], masks), _stack(k[i] * e_neg[i], masks)], axis=0)
            amat.append(_dot_nt(lhs, rhs))
        yield
        vst = [_stack(x, masks) for x in v]
        both = [_dot(jnp.concatenate([jnp.where(strict, amat[i][0:n4, n4:], 0.0),
                                      jnp.where(incl, amat[i][n4:, n4:], 0.0)], axis=0), vst[i]) for i in n]
        av = [x[:n4] for x in both]
        yv = [x[n4:] for x in both]
        yield
        t = yield from _neumann_inverse([jnp.where(strict, m[0:n4, 0:n4], 0.0) for m in amat], eye)
        yield
        z = [_dot(t[i], av[i]) for i in n]
        at = [_dot(t[i], a_st[i]) for i in n]
        yield
        for i in n:
            g_last = gl[i][CHUNK - 1:CHUNK]
            e_end = jnp.exp(g_last - gl[i])
            z_s[srows[i], :] = z[i]
            at_s[srows[i], :] = at[i].astype(BF16)
            arb_s[srows[i], :] = jnp.where(incl, amat[i][n4:, 0:n4], 0.0).astype(BF16)
            yv_s[rows[i], :] = _unstack(yv[i], CHUNK)
            rt_s[rows[i], :] = r_t[i].astype(BF16)
            be_s[rows[i], :] = (b[i] * e_end).astype(BF16)
            vk_s[srows[i], :] = bd * _dot_tn(v[i], k[i] * e_end)
            gt_s[pl.ds(pl.multiple_of(cis[i] * 8, 8), 8), :] = jnp.broadcast_to(jnp.exp(g_last), (8, 256))

    def advance(ci):
        rows = pl.ds(pl.multiple_of(ci * CHUNK, CHUNK), CHUNK)
        srows = pl.ds(pl.multiple_of(ci * n4, n4), n4)
        s = s_ref[...]
        sb = s.astype(BF16)
        u_st = _dot_nt(at_s[srows, :], sb) + z_s[srows, :]
        rs = _dot_nt(rt_s[rows, :], sb)
        yield
        au = _dot(arb_s[srows, :], u_st)
        ub = _dot_tn(_unstack(u_st, CHUNK), be_s[rows, :])
        yield
        y_s[rows, :] = rs + _unstack(au, CHUNK) + yv_s[rows, :]
        s_ref[...] = s * gt_s[pl.ds(pl.multiple_of(ci * 8, 8), 1), :] + bd * ub + vk_s[srows, :]

    def finish():
        y = y_s[...]
        v = v_s[...]
        mean = _seg_sum(y, hsum) * (1.0 / RWKV_N)
        d = y - mean
        var = _seg_sum(d * d, hsum) * (1.0 / RWKV_N)
        yn = d * lax.rsqrt(var + RWKV_LN_EPS) * lnw + lnb
        bonus = _seg_sum(r_s[...] * k_s[...] * rk, hsum) * v
        y_ref[...] = ((yn + bonus) * g_s[...]).astype(y_ref.dtype)

    return preprocess, prepare, advance, finish


def _rwkv(x, mu, w2a2, w0, a0, g2, kk, ka, rk, lnw, lnb):
    nb, rows = DELTA_STEP
    tb = _mixer_block(x.shape[1], rows)
    nc = tb // CHUNK
    stacked = lambda dt: ((nc * N_HEADS * CHUNK, 256), dt)
    scratch = ([((256, 256), F32), ((tb + 8, W_RWKV), F32)] + [((tb, 256), F32)] * 7
               + [stacked(BF16), stacked(F32), stacked(BF16), ((tb, 256), F32), ((tb, 256), BF16),
                  ((tb, 256), BF16), stacked(F32), ((nc * 8, 256), F32), ((tb, 256), F32)])
    return _mixer_call(_rwkv_body, "rwkv", x, (mu, w2a2, w0, a0, g2, kk, ka, rk, lnw, lnb), scratch, tb, nb)


def _mla_prep_kernel(x_ref, qnw_ref, kvnw_ref, wqa_ref, wqb_ref, wk_ref, wv_ref, ea_ref, eb_ref,
                     c1_ref, s1_ref, q_ref, k_ref, v_ref):
    x = x_ref[...]
    qn = _rms(x[:, 0:256], qnw_ref[...]).astype(BF16)
    kvn = _rms(x[:, 256:384], kvnw_ref[...]).astype(BF16)
    kpe = x[:, 384:512].astype(BF16)
    c1 = c1_ref[...]
    s1 = s1_ref[...]
    qa = jnp.dot(qn, wqa_ref[...], preferred_element_type=F32)
    qb = jnp.dot(qn, wqb_ref[...], preferred_element_type=F32)
    kn = jnp.dot(kvn, wk_ref[...], preferred_element_type=F32)
    kp = (jnp.dot(kpe, ea_ref[...], preferred_element_type=F32) * c1
          + jnp.dot(kpe, eb_ref[...], preferred_element_type=F32) * s1)
    ones_lane = ((_iota((1, N_HEADS * MLA_SLOT), 1) & (MLA_SLOT - 1)) == MLA_V).astype(F32)
    v_ref[...] = (jnp.dot(kvn, wv_ref[...], preferred_element_type=F32) + ones_lane).astype(BF16)
    for h in range(N_HEADS):
        sl = slice(h * MLA_SLOT, (h + 1) * MLA_SLOT)
        q_ref[:, sl] = (qa[:, sl] * c1 + qb[:, sl] * s1).astype(BF16)
        k_ref[:, sl] = (kn[:, sl] + kp).astype(BF16)


def _mla_prep(x, qnw, kvnw, wqa, wqb, wk, wv, ea, eb, c1, s1):
    b, tp, _ = x.shape
    tm = _pick(tp, (640, 128, 64))
    full = lambda a: pl.BlockSpec(a.shape, lambda i, j: (0, 0))
    wide = N_HEADS * MLA_SLOT
    out = pl.BlockSpec((None, tm, wide), lambda i, j: (i, j, 0))
    tab = pl.BlockSpec((tm, MLA_SLOT), lambda i, j: (j, 0))
    return pl.pallas_call(
        _mla_prep_kernel,
        grid=(b, tp // tm),
        in_specs=[pl.BlockSpec((None, tm, W_MLA), lambda i, j: (i, j, 0))]
        + [full(a) for a in (qnw, kvnw, wqa, wqb, wk, wv, ea, eb)] + [tab, tab],
        out_specs=[out, out, out],
        out_shape=[jax.ShapeDtypeStruct((b, tp, wide), BF16)] * 3,
        compiler_params=_cparams("parallel", "parallel"),
        name="mla_prep",
    )(x, qnw, kvnw, wqa, wqb, wk, wv, ea, eb, c1, s1)


FLASH_HEADS = 4


def _flash_kernel(q_ref, k_ref, v_ref, o_ref, m_ref, acc_ref):
    qi = pl.program_id(2)
    t = q_ref.shape[0]
    m_ref[...] = jnp.full(m_ref.shape, NEG_INF, F32)
    acc_ref[...] = jnp.zeros(acc_ref.shape, F32)

    def block(j, diagonal):
        rows = pl.ds(pl.multiple_of(j * t, t), t)
        for h in range(FLASH_HEADS):
            sl = slice(h * MLA_SLOT, (h + 1) * MLA_SLOT)
            s = lax.dot_general(q_ref[:, sl], k_ref[rows, sl], (((1,), (1,)), ((), ())),
                                preferred_element_type=F32)
            if diagonal:
                s = jnp.where(_iota(s.shape, 0) >= _iota(s.shape, 1), s, NEG_INF)
            m_old = m_ref[h]
            m_new = jnp.maximum(m_old, jnp.max(s, axis=-1, keepdims=True))
            p = jnp.concatenate([jnp.exp2(s[:, c * LANE:(c + 1) * LANE] - m_new)
                                 for c in range(s.shape[1] // LANE)], axis=1).astype(BF16)
            acc_ref[h] = (jnp.exp2(m_old - m_new) * acc_ref[h]
                          + jnp.dot(p, v_ref[rows, sl], preferred_element_type=F32))
            m_ref[h] = m_new

    def full_block(j, carry):
        block(j, False)
        return carry

    lax.fori_loop(0, qi, full_block, 0)
    block(qi, True)
    for h in range(FLASH_HEADS):
        acc = acc_ref[h]
        o_ref[:, h * MLA_SLOT:(h + 1) * MLA_SLOT] = (acc / acc[:, MLA_V:MLA_V + 1]).astype(o_ref.dtype)


def _flash(q, k, v):
    b, tp, wide = q.shape
    t = _pick(tp, (640, 128, 64))
    w = FLASH_HEADS * MLA_SLOT
    qspec = pl.BlockSpec((None, t, w), lambda i, h, qi: (i, qi, h))
    kspec = pl.BlockSpec((None, tp, w), lambda i, h, qi: (i, 0, h))
    return pl.pallas_call(
        _flash_kernel,
        grid=(b, wide // w, tp // t),
        in_specs=[qspec, kspec, kspec],
        out_specs=qspec,
        out_shape=jax.ShapeDtypeStruct((b, tp, wide), BF16),
        scratch_shapes=[pltpu.VMEM((FLASH_HEADS, t, LANE), F32), pltpu.VMEM((FLASH_HEADS, t, MLA_SLOT), F32)],
        compiler_params=_cparams("parallel", "parallel", "arbitrary"),
        name="mla_flash",
    )(q, k, v)


def _router_kernel(h_ref, nw_ref, wr_ref, xn_ref, info_ref):
    xn = _rms(h_ref[...], nw_ref[...])
    bits = pltpu.bitcast(xn.astype(BF16).astype(F32), jnp.uint32)
    half = D_MODEL // 2
    xn_ref[...] = (bits[:, :half] >> 16) | bits[:, half:]
    logits = jnp.dot(xn, wr_ref[...], preferred_element_type=F32, precision=lax.Precision.HIGHEST)
    lane = _iota(logits.shape, 1).astype(F32)
    valid = lane < N_EXPERTS
    l0 = jnp.where(valid, logits, NEG_INF)
    m1 = jnp.max(l0, axis=-1, keepdims=True)
    i1 = jnp.min(jnp.where(l0 == m1, lane, float(LANE)), axis=-1, keepdims=True)
    l1 = jnp.where(lane == i1, NEG_INF, l0)
    m2 = jnp.max(l1, axis=-1, keepdims=True)
    i2 = jnp.min(jnp.where(l1 == m2, lane, float(LANE)), axis=-1, keepdims=True)
    e2 = jnp.exp(m2 - m1)
    g1 = 1.0 / (1.0 + e2)
    g2 = e2 / (1.0 + e2)
    info = jnp.where(lane == 0, i1, 0.0)
    info = jnp.where(lane == 1, i2, info)
    info = jnp.where(lane == 2, g1, info)
    info = jnp.where(lane == 3, g2, info)
    info_ref[...] = info


def _router(h, nw, wr):
    n = h.shape[0]
    tm = _pick(n, (512, 256, 128, 64))
    return pl.pallas_call(
        _router_kernel,
        grid=(n // tm,),
        in_specs=[pl.BlockSpec((tm, D_MODEL), lambda i: (i, 0)), pl.BlockSpec(nw.shape, lambda i: (0, 0)),
                  pl.BlockSpec(wr.shape, lambda i: (0, 0))],
        out_specs=[pl.BlockSpec((tm, D_MODEL // 2), lambda i: (i, 0)), pl.BlockSpec((tm, LANE), lambda i: (i, 0))],
        out_shape=[jax.ShapeDtypeStruct((n, D_MODEL // 2), jnp.uint32), jax.ShapeDtypeStruct((n, LANE), F32)],
        compiler_params=_cparams("parallel"),
        name="moe_router",
    )(h, nw, wr)


def _expert_kernel(be_ref, x_ref, gate_ref, wg_ref, wu_ref, wd_ref, out_ref, acc_ref):
    f = pl.program_id(1)

    @pl.when(f == 0)
    def _():
        acc_ref[...] = jnp.zeros(acc_ref.shape, F32)

    packed = x_ref[...]
    low = pltpu.bitcast(packed << 16, F32)
    high = pltpu.bitcast(packed & jnp.uint32(0xFFFF0000), F32)
    xb = jnp.concatenate([low, high], axis=1).astype(BF16)
    a = jnp.dot(xb, wg_ref[...], preferred_element_type=F32)
    b = jnp.dot(xb, wu_ref[...], preferred_element_type=F32)
    acc_ref[...] += jnp.dot((_silu(a) * b).astype(BF16), wd_ref[...], preferred_element_type=F32)

    @pl.when(f == pl.num_programs(1) - 1)
    def _():
        out_ref[...] = acc_ref[...] * gate_ref[...]


def _experts(block_expert, x_rows, row_gate, wg, wu, wd, tm):
    cap = x_rows.shape[0]
    dff = wg.shape[2]
    tf = _pick(dff, (1792, 512, 256, 128))
    grid_spec = pltpu.PrefetchScalarGridSpec(
        num_scalar_prefetch=1,
        grid=(cap // tm, dff // tf),
        in_specs=[pl.BlockSpec((tm, D_MODEL // 2), lambda i, f, be: (i, 0)),
                  pl.BlockSpec((tm, 1), lambda i, f, be: (i, 0)),
                  pl.BlockSpec((None, D_MODEL, tf), lambda i, f, be: (be[i], 0, f)),
                  pl.BlockSpec((None, D_MODEL, tf), lambda i, f, be: (be[i], 0, f)),
                  pl.BlockSpec((None, tf, D_MODEL), lambda i, f, be: (be[i], f, 0))],
        out_specs=pl.BlockSpec((tm, D_MODEL), lambda i, f, be: (i, 0)),
        scratch_shapes=[pltpu.VMEM((tm, D_MODEL), F32)],
    )
    return pl.pallas_call(
        _expert_kernel,
        grid_spec=grid_spec,
        out_shape=jax.ShapeDtypeStruct((cap, D_MODEL), F32),
        compiler_params=_cparams("parallel", "arbitrary"),
        name="moe_experts",
    )(block_expert, x_rows, row_gate, wg, wu, wd)


def _combine_kernel(h_ref, ya_ref, yb_ref, out_ref):
    out_ref[...] = h_ref[...] + (ya_ref[...] + yb_ref[...])


def _combine(h, y2):
    n = h.shape[0]
    tm = _pick(n, (1024, 512, 256, 128, 64))
    spec = pl.BlockSpec((tm, D_MODEL), lambda i: (i, 0))
    second = pl.BlockSpec((tm, D_MODEL), lambda i: (i + n // tm, 0))
    return pl.pallas_call(
        _combine_kernel, grid=(n // tm,), in_specs=[spec, spec, second], out_specs=spec,
        out_shape=jax.ShapeDtypeStruct((n, D_MODEL), F32),
        compiler_params=_cparams("parallel"), name="moe_combine",
    )(h, y2, y2)


def _sc_gather(table, idx):
    rows = idx.shape[0]
    d = table.shape[1]
    info = plsc.get_sparse_core_info()
    workers = info.num_cores * info.num_subcores
    assert rows % (workers * SC_CHUNK) == 0, (rows, workers)
    per_worker = rows // workers
    mesh = plsc.VectorSubcoreMesh(core_axis_name="c", subcore_axis_name="s")

    @functools.partial(
        pl.kernel, mesh=mesh, out_type=jax.ShapeDtypeStruct((rows, d), table.dtype),
        scratch_types=[pltpu.VMEM((SC_CHUNK,), jnp.int32), pltpu.VMEM((SC_CHUNK, d), table.dtype),
                       pltpu.SemaphoreType.DMA],
        name="sc_gather")
    def gather(table_hbm, idx_hbm, out_hbm, idx_v, rows_v, sem):
        base = (lax.axis_index("s") * info.num_cores + lax.axis_index("c")) * per_worker

        @pl.loop(0, per_worker // SC_CHUNK)
        def _(i):
            off = pl.multiple_of(base + i * SC_CHUNK, 8)
            pltpu.sync_copy(idx_hbm.at[pl.ds(off, SC_CHUNK)], idx_v)
            pltpu.async_copy(table_hbm.at[idx_v], rows_v, sem).wait()
            pltpu.sync_copy(rows_v, out_hbm.at[pl.ds(off, SC_CHUNK)])

    return gather(table, idx)


def _moe(h, nw, router, wg, wu, wd, defer_combine=False):
    n = h.shape[0]
    tm = _pick(n, (512, 64))
    wr = jnp.pad(router.astype(F32), ((0, 0), (0, LANE - N_EXPERTS)))
    xn, info = _router(h, nw, wr)
    expert = info[:, 0:2].astype(jnp.int32).reshape(-1)
    gate = info[:, 2:4].reshape(-1)
    n_assign = 2 * n
    order = jnp.argsort(expert)
    onehot = (expert[:, None] == jnp.arange(N_EXPERTS, dtype=jnp.int32)[None, :]).astype(jnp.int32)
    running = jnp.cumsum(onehot, axis=0)
    counts = running[-1]
    padded = (counts + tm - 1) // tm * tm
    pad_end = jnp.cumsum(padded)
    pad_start = pad_end - padded
    start = jnp.cumsum(counts) - counts
    n_blocks = -(-n_assign // tm) + N_EXPERTS
    cap = n_blocks * tm
    block_start = jnp.arange(n_blocks, dtype=jnp.int32) * tm
    block_expert = jnp.minimum(jnp.sum(block_start[:, None] >= pad_end[None, :], axis=1), N_EXPERTS - 1)
    block_expert = block_expert.astype(jnp.int32)
    rank = (block_start - pad_start[block_expert])[:, None] + jnp.arange(tm, dtype=jnp.int32)[None, :]
    valid = (rank < counts[block_expert][:, None]).reshape(cap)
    src = order[jnp.clip(start[block_expert][:, None] + rank, 0, n_assign - 1).reshape(cap)]
    row_token = jnp.where(valid, src // 2, 0)
    row_gate = jnp.where(valid, gate[src], 0.0)
    dest = jnp.sum(onehot * (pad_start[None, :] + running - 1), axis=1)
    x_rows = _sc_gather(xn, row_token)
    y_rows = _experts(block_expert, x_rows, row_gate[:, None], wg, wu, wd, tm)
    y2 = _sc_gather(y_rows, dest.reshape(n, 2).T.reshape(n_assign))
    return (h, y2) if defer_combine else _combine(h, y2)


def _pad_cols(a, width):
    return jnp.pad(a, ((0, 0), (0, width - a.shape[1])))


def _row(a, width=None):
    a = a.reshape(1, -1).astype(F32)
    return a if width is None else _pad_cols(a, width)


def _rope_tables(tp):
    pos = jnp.arange(tp, dtype=F32)
    inv_freq = ROPE_THETA ** (-jnp.arange(0, MLA_ROPE, 2, dtype=F32) / MLA_ROPE)
    ang = pos[:, None] * inv_freq[None, :]
    cos, sin = jnp.cos(ang), jnp.sin(ang)
    ones = jnp.ones((tp, MLA_NOPE), F32)
    zeros = jnp.zeros((tp, MLA_SLOT - MLA_NOPE - MLA_ROPE), F32)
    c1 = jnp.concatenate([ones, cos, cos, zeros], axis=1)
    s1 = jnp.concatenate([0.0 * ones, -sin, sin, zeros], axis=1)
    return c1, s1


def _mla_weights(w_uq, w_ukv):
    half = MLA_ROPE // 2
    scale = (MLA_NOPE + MLA_ROPE) ** -0.5 * math.log2(math.e)
    zq =jnp.zeros((w_uq.shape[0], MLA_SLOT - MLA_NOPE - MLA_ROPE), F32)
    zn = jnp.zeros((w_uq.shape[0], MLA_NOPE), F32)
    zk = jnp.zeros((w_ukv.shape[0], MLA_SLOT - MLA_NOPE), F32)
    wqa, wqb, wk, wv = [], [], [], []
    for h in range(N_HEADS):
        q = w_uq[:, h * 96:(h + 1) * 96] * scale
        nope, x1, x2 = q[:, :MLA_NOPE], q[:, MLA_NOPE:MLA_NOPE + half], q[:, MLA_NOPE + half:]
        wqa += [nope, x1, x2, zq]
        wqb += [zn, x2, x1, zq]
        kv = w_ukv[:, h * 128:(h + 1) * 128]
        wk += [kv[:, :MLA_NOPE], zk]
        wv += [kv[:, MLA_NOPE:], zk]
    cat = lambda parts: jnp.concatenate(parts, axis=1).astype(BF16)
    ea = np.zeros((MLA_SLOT, MLA_SLOT), np.float32)
    eb = np.zeros((MLA_SLOT, MLA_SLOT), np.float32)
    for i in range(MLA_ROPE):
        ea[i, MLA_NOPE + i] = 1.0
        eb[(i + half) % MLA_ROPE, MLA_NOPE + i] = 1.0
    return cat(wqa), cat(wqb), cat(wk), cat(wv), jnp.asarray(ea, BF16), jnp.asarray(eb, BF16)


def _branch_weights(w_branch):
    wb = w_branch.astype(BF16)
    z = jnp.zeros((MLA_SLOT - MLA_V, D_MODEL), BF16)
    parts = []
    for h in range(N_HEADS):
        parts += [wb[1, h * MLA_V:(h + 1) * MLA_V], z]
    return wb[0], jnp.concatenate(parts, axis=0), wb[2], wb[3]


def _token_mixing(h, b, tp, p, l, tables):
    w_in = p["w_in"][l]
    col = lambda i, j: w_in[:, _OFF[i]:_OFF[j]]
    wg = jnp.concatenate([col(0, 3), col(4, 5), _pad_cols(col(3, 4), LANE)], axis=1).astype(BF16)
    wm = jnp.concatenate([col(5, 7), _pad_cols(col(7, 8), LANE)], axis=1).astype(BF16)
    wd = jnp.concatenate([col(8, 10), _pad_cols(col(10, 12), LANE)], axis=1).astype(BF16)
    wr = _pad_cols(col(12, 13), W_RWKV).astype(BF16)
    nw = _row(p["norm_mix"][l])
    xg, xm, xd, xr = _inproj(h, nw, wg, wm, wd, wr)
    shape3 = lambda a: a.reshape(b, tp, a.shape[1])

    gup = jnp.pad(p["gla_gate_up"][l], ((0, LANE - 16), (0, 0))).astype(BF16)
    y_gla = _gla(shape3(xg), gup, _row(p["gla_gate_bias"][l]), _row(jnp.tile(p["gla_norm"][l], N_HEADS)))

    wqa, wqb, wk, wv, ea, eb = _mla_weights(p["mla_w_uq"][l], p["mla_w_ukv"][l])
    q, k, v = _mla_prep(shape3(xm), _row(p["mla_q_norm"][l]), _row(p["mla_kv_norm"][l]),
                        wqa, wqb, wk, wv, ea, eb, *tables)
    y_mla = _flash(q, k, v)

    lanes4 = lambda a: jnp.pad(a.reshape(1, N_HEADS).astype(F32), ((0, 0), (N_HEADS, LANE - 2 * N_HEADS)))
    y_gdn = _gdn(shape3(xd), p["gdn_conv"][l].astype(F32), lanes4(p["gdn_a_log"][l]),
                 lanes4(p["gdn_dt_bias"][l]), _row(jnp.tile(p["gdn_norm"][l], N_HEADS)))

    w2a2 = jnp.zeros((LANE, 512), F32)
    w2a2 = w2a2.at[0:64, 0:256].set(p["rwkv_w2"][l]).at[64:128, 256:512].set(p["rwkv_a2"][l]).astype(BF16)
    g2 = jnp.pad(p["rwkv_g2"][l], ((0, 256 - 160), (0, 0))).astype(BF16)
    y_rwkv = _rwkv(shape3(xr), _row(p["rwkv_mu"][l], W_RWKV), w2a2, _row(p["rwkv_w0"][l]),
                   _row(p["rwkv_a0"][l]), g2, _row(p["rwkv_k_k"][l]), _row(p["rwkv_k_a"][l]),
                   _row(p["rwkv_r_k"][l]), _row(p["rwkv_ln_w"][l]), _row(p["rwkv_ln_b"][l]))

    flat = lambda a: a.reshape(b * tp, a.shape[2])
    ys = [flat(y_gla), flat(y_mla), flat(y_gdn), flat(y_rwkv)]
    return _merge(h, nw, col(13, 14).astype(BF16), ys, _branch_weights(p["w_branch"][l]),
                  p["w_out"][l].astype(BF16))


def kernel(x, meta_tokens, norm_mix, w_in, gla_gate_up, gla_gate_bias, gla_norm, mla_q_norm, mla_w_uq, mla_kv_norm, mla_w_ukv, gdn_conv, gdn_a_log, gdn_dt_bias, gdn_norm, rwkv_mu, rwkv_w0, rwkv_w2, rwkv_a0, rwkv_a2, rwkv_g2, rwkv_k_k, rwkv_k_a, rwkv_r_k, rwkv_ln_w, rwkv_ln_b, w_branch, w_out, norm_ffn, ffn_w_gate, ffn_w_up, ffn_w_down, moe_router, moe_w_gate, moe_w_up, moe_w_down, norm_final):
    p = dict(norm_mix=norm_mix, w_in=w_in, gla_gate_up=gla_gate_up, gla_gate_bias=gla_gate_bias,
             gla_norm=gla_norm, mla_q_norm=mla_q_norm, mla_w_uq=mla_w_uq, mla_kv_norm=mla_kv_norm,
             mla_w_ukv=mla_w_ukv, gdn_conv=gdn_conv, gdn_a_log=gdn_a_log, gdn_dt_bias=gdn_dt_bias,
             gdn_norm=gdn_norm, rwkv_mu=rwkv_mu, rwkv_w0=rwkv_w0, rwkv_w2=rwkv_w2, rwkv_a0=rwkv_a0,
             rwkv_a2=rwkv_a2, rwkv_g2=rwkv_g2, rwkv_k_k=rwkv_k_k, rwkv_k_a=rwkv_k_a, rwkv_r_k=rwkv_r_k,
             rwkv_ln_w=rwkv_ln_w, rwkv_ln_b=rwkv_ln_b, w_branch=w_branch, w_out=w_out)
    b, seq, d = x.shape
    t_real = N_META + seq
    tp = -(-t_real // 128) * 128
    meta = jnp.broadcast_to(meta_tokens[None].astype(x.dtype), (b, N_META, d))
    h = jnp.concatenate([meta, x, jnp.zeros((b, tp - t_real, d), x.dtype)], axis=1).reshape(b * tp, d)
    tables = _rope_tables(tp)
    depth = norm_mix.shape[0]
    y2 = None
    for l in range(depth):
        h = _token_mixing(h, b, tp, p, l, tables)
        nw = _row(norm_ffn[l])
        if l % 2 == 0:
            h = _ffn(h, nw, ffn_w_gate[l // 2].astype(BF16), ffn_w_up[l // 2].astype(BF16),
                     ffn_w_down[l // 2].astype(BF16))
        elif l == depth - 1:
            h, y2 = _moe(h, nw, moe_router[l // 2], moe_w_gate[l // 2].astype(BF16),
                         moe_w_up[l // 2].astype(BF16), moe_w_down[l // 2].astype(BF16), defer_combine=True)
        else:
            h = _moe(h, nw, moe_router[l // 2], moe_w_gate[l // 2].astype(BF16),
                     moe_w_up[l // 2].astype(BF16), moe_w_down[l // 2].astype(BF16))
    return _final(h, _row(norm_final), b, tp, seq, y2)
```

```python
import functools
import math

import jax
import jax.numpy as jnp
import numpy as np
from jax import lax
from jax.experimental import pallas as pl
from jax.experimental.pallas import tpu as pltpu
from jax.experimental.pallas import tpu_sc as plsc

F32 = jnp.float32
BF16 = jnp.bfloat16

D_MODEL = 1024
N_META = 16
N_HEADS = 4
GLA_DK = 32
GLA_DV = 64
GLA_TAU = 16.0
MLA_NOPE = 64
MLA_ROPE = 32
MLA_V = 64
MLA_SLOT = 128
ROPE_THETA = 10000.0
GDN_DK = 64
GDN_CONV = 4
RWKV_N = 64
RWKV_LN_EPS = RWKV_N * 1e-5
CHUNK = 64
SUB = 16
PREP_GROUP = 5
GLA_STEP = (2, 320)
DELTA_STEP = (4, 128)
N_EXPERTS = 8
SC_CHUNK = 64
NORM_EPS = 1e-6
L2_EPS = 1e-6
NEG_INF = -1e30
EXP_CLAMP = 80.0

LANE = 128
VMEM_LIMIT = 56 * 1024 * 1024

_OFF = np.cumsum([0, 128, 128, 256, 16, 256, 256, 128, 32, 768, 256, 4, 4, 1056, 4096]).tolist()
W_GLA, W_MLA, W_GDN, W_RWKV = 896, 512, 1152, 1152


def _cparams(*sem):
    return pltpu.CompilerParams(dimension_semantics=sem, vmem_limit_bytes=VMEM_LIMIT)


def _pick(n, prefs):
    for p in prefs:
        if n % p == 0:
            return p
    raise ValueError(f"no tile for {n}")


def _dot(a, b):
    return jnp.dot(a.astype(BF16), b.astype(BF16), preferred_element_type=F32)


def _dot_nt(a, b):
    return lax.dot_general(a.astype(BF16), b.astype(BF16), (((1,), (1,)), ((), ())),
                           preferred_element_type=F32)


def _dot_tn(a, b):
    return lax.dot_general(a.astype(BF16), b.astype(BF16), (((0,), (0,)), ((), ())),
                           preferred_element_type=F32)


def _split3(x):
    hi = x.astype(BF16)
    r1 = x - hi.astype(F32)
    mid = r1.astype(BF16)
    lo = (r1 - mid.astype(F32)).astype(BF16)
    return hi, mid, lo


def _dot01_left(m01, x):
    return sum(jnp.dot(m01, p, preferred_element_type=F32) for p in _split3(x))


def _dot01_right(x, m01):
    return sum(jnp.dot(p, m01, preferred_element_type=F32) for p in _split3(x))


def _seg_sum(x, m01):
    hi = x.astype(BF16)
    lo = (x - hi.astype(F32)).astype(BF16)
    return jnp.dot(hi, m01, preferred_element_type=F32) + jnp.dot(lo, m01, preferred_element_type=F32)


def _iota(shape, dim):
    return lax.broadcasted_iota(jnp.int32, shape, dim)


def _div(x, w):
    return x >> int(math.log2(w))


def _ltri(n):
    return (_iota((n, n), 0) >= _iota((n, n), 1)).astype(BF16)


def _head_ones(n, w):
    return (_div(_iota((n, n), 0), w) == _div(_iota((n, n), 1), w)).astype(BF16)


def _lane_masks(width, w):
    lane = _div(_iota((1, width), 1), w)
    return [(lane == h).astype(F32) for h in range(width // w)]


def _stack(x, masks):
    return jnp.concatenate([x * m for m in masks], axis=0)


def _unstack(y, n):
    out = y[0:n]
    for h in range(1, y.shape[0] // n):
        out = out + y[h * n:(h + 1) * n]
    return out


def _rms(x, w):
    return x * lax.rsqrt(jnp.mean(x * x, axis=-1, keepdims=True) + NORM_EPS) * w


def _sigmoid(x):
    return 1.0 / (1.0 + jnp.exp(-x))


def _silu(x):
    return x * _sigmoid(x)


def _softplus(x):
    return jnp.maximum(x, 0.0) + jnp.log(1.0 + jnp.exp(-jnp.abs(x)))


def _neumann_inverse(xs, eye):
    n = xs[0].shape[0]
    ts = [eye + x for x in xs]
    ps = [_dot(x, x) for x in xs]
    for _ in range(int(math.log2(CHUNK)) - 2):
        yield
        both = [_dot(jnp.concatenate([t, p], axis=0), p) for t, p in zip(ts, ps)]
        ts = [t + tp[:n] for t, tp in zip(ts, both)]
        ps = [tp[n:] for tp in both]
    yield
    return [t + _dot(t, p) for t, p in zip(ts, ps)]


def _for_chunks(n, body, group):
    def trip(i, carry):
        body([i * group + g for g in range(group)])
        return carry

    if n >= group:
        lax.fori_loop(0, n // group, trip, 0)
    if n % group:
        body(list(range(n - n % group, n)))


def _inproj_kernel(h_ref, nw_ref, wg_ref, wm_ref, wd_ref, wr_ref, og_ref, om_ref, od_ref, or_ref):
    xb = _rms(h_ref[...], nw_ref[...]).astype(BF16)
    og_ref[...] = jnp.dot(xb, wg_ref[...], preferred_element_type=F32)
    om_ref[...] = jnp.dot(xb, wm_ref[...], preferred_element_type=F32)
    od_ref[...] = jnp.dot(xb, wd_ref[...], preferred_element_type=F32)
    or_ref[...] = jnp.dot(xb, wr_ref[...], preferred_element_type=F32)


def _inproj(h, nw, wg, wm, wd, wr):
    n = h.shape[0]
    tm = _pick(n, (512, 256, 128, 64))
    full = lambda a: pl.BlockSpec(a.shape, lambda i: (0, 0))
    row = lambda w: pl.BlockSpec((tm, w), lambda i: (i, 0))
    return pl.pallas_call(
        _inproj_kernel,
        grid=(n // tm,),
        in_specs=[row(D_MODEL), full(nw), full(wg), full(wm), full(wd), full(wr)],
        out_specs=[row(W_GLA), row(W_MLA), row(W_GDN), row(W_RWKV)],
        out_shape=[jax.ShapeDtypeStruct((n, w), F32) for w in (W_GLA, W_MLA, W_GDN, W_RWKV)],
        compiler_params=_cparams("parallel"),
        name="inproj",
    )(h, nw, wg, wm, wd, wr)


def _merge_kernel(h_ref, nw_ref, wgate_ref, yg_ref, ym_ref, yd_ref, yr_ref,
                  wbg_ref, wbm_ref, wbd_ref, wbr_ref, wout_ref, out_ref):
    x = h_ref[...]
    xb = _rms(x, nw_ref[...]).astype(BF16)
    acc = jnp.zeros(x.shape, F32)
    branches = ((yg_ref, wbg_ref), (ym_ref, wbm_ref), (yd_ref, wbd_ref), (yr_ref, wbr_ref))
    for i, (y_ref, wb_ref) in enumerate(branches):
        logits = jnp.dot(xb, wgate_ref[:, i * D_MODEL:(i + 1) * D_MODEL], preferred_element_type=F32)
        proj = jnp.dot(y_ref[...], wb_ref[...], preferred_element_type=F32)
        acc = acc + _sigmoid(logits) * proj
    out_ref[...] = x + jnp.dot(acc.astype(BF16), wout_ref[...], preferred_element_type=F32)


def _merge(h, nw, wgate, ys, wbs, wout):
    n = h.shape[0]
    tm = _pick(n, (512, 256, 128, 64))
    full = lambda a: pl.BlockSpec(a.shape, lambda i: (0, 0))
    row = lambda w: pl.BlockSpec((tm, w), lambda i: (i, 0))
    return pl.pallas_call(
        _merge_kernel,
        grid=(n // tm,),
        in_specs=[row(D_MODEL), full(nw), full(wgate)] + [row(y.shape[1]) for y in ys]
        + [full(w) for w in wbs] + [full(wout)],
        out_specs=row(D_MODEL),
        out_shape=jax.ShapeDtypeStruct((n, D_MODEL), F32),
        compiler_params=_cparams("parallel"),
        name="merge",
    )(h, nw, wgate, *ys, *wbs, wout)


def _ffn_kernel(h_ref, nw_ref, wg_ref, wu_ref, wd_ref, out_ref, xb_ref, acc_ref):
    f = pl.program_id(1)

    @pl.when(f == 0)
    def _():
        xb_ref[...] = _rms(h_ref[...], nw_ref[...]).astype(BF16)
        acc_ref[...] = jnp.zeros(acc_ref.shape, F32)

    xb = xb_ref[...]
    a = jnp.dot(xb, wg_ref[...], preferred_element_type=F32)
    b = jnp.dot(xb, wu_ref[...], preferred_element_type=F32)
    acc_ref[...] += jnp.dot((_silu(a) * b).astype(BF16), wd_ref[...], preferred_element_type=F32)

    @pl.when(f == pl.num_programs(1) - 1)
    def _():
        out_ref[...] = h_ref[...] + acc_ref[...]


def _ffn(h, nw, wg, wu, wd):
    n = h.shape[0]
    dff = wg.shape[1]
    tm = _pick(n, (512, 256, 128, 64))
    tf = _pick(dff, (1408, 512, 256, 128))
    return pl.pallas_call(
        _ffn_kernel,
        grid=(n // tm, dff // tf),
        in_specs=[pl.BlockSpec((tm, D_MODEL), lambda i, f: (i, 0)),
                  pl.BlockSpec(nw.shape, lambda i, f: (0, 0)),
                  pl.BlockSpec((D_MODEL, tf), lambda i, f: (0, f)),
                  pl.BlockSpec((D_MODEL, tf), lambda i, f: (0, f)),
                  pl.BlockSpec((tf, D_MODEL), lambda i, f: (f, 0))],
        out_specs=pl.BlockSpec((tm, D_MODEL), lambda i, f: (i, 0)),
        out_shape=jax.ShapeDtypeStruct((n, D_MODEL), F32),
        scratch_shapes=[pltpu.VMEM((tm, D_MODEL), BF16), pltpu.VMEM((tm, D_MODEL), F32)],
        compiler_params=_cparams("parallel", "arbitrary"),
        name="ffn",
    )(h, nw, wg, wu, wd)


def _final_kernel(nw_ref, *refs):
    out_ref = refs[-1]
    x = refs[0][...]
    for extra in refs[1:-1]:
        x = x + extra[...]
    out_ref[...] = _rms(x, nw_ref[...])


def _final(h, nw, b, tp, seq, y2=None):
    tq = _pick(seq, (1024, 512, 256, 128, 64))
    n = b * tp
    rows = lambda shift: pl.BlockSpec((pl.Element(tq), pl.Element(D_MODEL)),
                                      lambda i, j: (pl.multiple_of(shift + i * tp + N_META + j * tq, 8), 0))
    operands = [h] if y2 is None else [h, y2, y2]
    shifts = [0] if y2 is None else [0, 0, n]
    return pl.pallas_call(
        _final_kernel,
        grid=(b, seq // tq),
        in_specs=[pl.BlockSpec(nw.shape, lambda i, j: (0, 0))] + [rows(s) for s in shifts],
        out_specs=pl.BlockSpec((None, tq, D_MODEL), lambda i, j: (i, j, 0)),
        out_shape=jax.ShapeDtypeStruct((b, seq, D_MODEL), F32),
        compiler_params=_cparams("parallel", "parallel"),
        name="final_norm",
    )(nw, *operands)


_DONE = object()


def _interleave(generators):
    live = list(generators)
    while live:
        live = [g for g in live if next(g, _DONE) is not _DONE]


def _mixer_kernel(body, n_params, x_ref, *refs):
    params, y_ref, scratch = refs[:n_params], refs[n_params], refs[n_params + 1:]
    parts = [body(x_ref.at[b], *params, y_ref.at[b], *[s.at[b] for s in scratch]) for b in range(x_ref.shape[0])]
    n_chunks = x_ref.shape[1] // CHUNK
    if n_chunks <= PREP_GROUP:
        half = max(len(parts) // 2, 1)
        pending = []
        for group in [parts[i:i + half] for i in range(0, len(parts), half)]:
            _interleave([preprocess() for preprocess, _, _, _ in group] + pending)
            pending = [prepare(list(range(n_chunks))) for _, prepare, _, _ in group]
        _interleave(pending)
    else:
        for preprocess, _, _, _ in parts:
            _interleave([preprocess()])
        _for_chunks(n_chunks, lambda cis: [_interleave([prepare(cis)]) for _, prepare, _, _ in parts], PREP_GROUP)
    _for_chunks(n_chunks, lambda cis: [_interleave([advance(ci) for _, _, advance, _ in parts]) for ci in cis], 1)
    for _, _, _, finish in parts:
        finish()


def _mixer_block(tp, rows):
    return _pick(tp, (rows, 128, 64))


def _mixer_call(body, name, x, params, scratch, tb, nb):
    b, tp, width = x.shape
    nb = nb if b % nb == 0 else 1
    full = lambda a: pl.BlockSpec(a.shape, lambda i, j: (0, 0))
    return pl.pallas_call(
        functools.partial(_mixer_kernel, body, len(params)),
        grid=(b // nb, tp // tb),
        in_specs=[pl.BlockSpec((nb, tb, width), lambda i, j: (i, j, 0))] + [full(a) for a in params],
        out_specs=pl.BlockSpec((nb, tb, 256), lambda i, j: (i, j, 0)),
        out_shape=jax.ShapeDtypeStruct((b, tp, 256), BF16),
        scratch_shapes=[pltpu.VMEM((nb,) + shape, dtype) for shape, dtype in scratch],
        compiler_params=_cparams("parallel", "arbitrary"),
        name=name,
    )(x, *params)


def _gla_body(x_ref, gup_ref, gb_ref, nw_ref, y_ref, st_ref, la_ref, o_s, qg_s, kv_s, gt_s):
    @pl.when(pl.program_id(1) == 0)
    def _():
        st_ref[...] = jnp.zeros(st_ref.shape, F32)

    def preprocess():
        z = _dot(x_ref[:, 768:896], gup_ref[...]) + gb_ref[...]
        yield
        la_ref[...] = -_softplus(-z) * (1.0 / GLA_TAU)

    ltri = _ltri(CHUNK)
    qmasks = _lane_masks(N_HEADS * GLA_DK, GLA_DK)
    vmasks = _lane_masks(N_HEADS * GLA_DV, GLA_DV)
    bd = (_div(_iota((256, 128), 0), GLA_DV) == _div(_iota((256, 128), 1), GLA_DK)).astype(F32)
    hsum = _head_ones(N_HEADS * GLA_DV, GLA_DV)
    nw = nw_ref[...]

    n4 = N_HEADS * GLA_DV

    def prepare(cis):
        n = range(len(cis))
        rows = [pl.ds(pl.multiple_of(ci * CHUNK, CHUNK), CHUNK) for ci in cis]
        g = [_dot01_left(ltri, la_ref[r, :]) for r in rows]
        q = [x_ref[r, 0:128] * (GLA_DK ** -0.5) for r in rows]
        k = [x_ref[r, 128:256] for r in rows]
        v = [x_ref[r, 256:512] for r in rows]
        intra = [[] for _ in n]
        for s in range(CHUNK // SUB):
            lo, hi = s * SUB, (s + 1) * SUB
            sc = []
            for i in n:
                gs = jnp.zeros((1, 128), F32) if s == 0 else g[i][lo - 1:lo]
                qs = q[i][lo:hi] * jnp.exp(g[i][lo:hi] - gs)
                kt = k[i][:hi] * jnp.exp(jnp.minimum(gs - g[i][:hi], EXP_CLAMP))
                sc.append(_dot_nt(_stack(qs, qmasks), kt))
            causal = _iota(sc[0].shape, 1) <= lo + (_iota(sc[0].shape, 0) & (SUB - 1))
            yield
            p = [_dot(jnp.where(causal, sc[i], 0.0), v[i][:hi]) for i in n]
            for i in n:
                intra[i].append(sum(p[i][h * SUB:(h + 1) * SUB] * vmasks[h] for h in range(N_HEADS)))
            yield
        for i in n:
            g_last = g[i][CHUNK - 1:CHUNK]
            o_s[rows[i], :] = jnp.concatenate(intra[i], axis=0)
            qg_s[rows[i], :] = (q[i] * jnp.exp(g[i])).astype(BF16)
            kv_s[pl.ds(pl.multiple_of(cis[i] * n4, n4), n4), :] = bd * _dot_tn(v[i], k[i] * jnp.exp(g_last - g[i]))
            gt_s[pl.ds(pl.multiple_of(cis[i] * 8, 8), 8), :] = jnp.broadcast_to(jnp.exp(g_last), (8, 128))

    def advance(ci):
        rows = pl.ds(pl.multiple_of(ci * CHUNK, CHUNK), CHUNK)
        st = st_ref[...]
        inter = _dot_nt(qg_s[rows, :], st)
        yield
        o_s[rows, :] += inter
        st_ref[...] = (st * gt_s[pl.ds(pl.multiple_of(ci * 8, 8), 1), :]
                       + kv_s[pl.ds(pl.multiple_of(ci * n4, n4), n4), :])

    def finish():
        o = o_s[...]
        ms = _seg_sum(o * o, hsum) * (1.0 / GLA_DV)
        y_ref[...] = (o * lax.rsqrt(ms + NORM_EPS) * nw * _silu(x_ref[:, 512:768])).astype(y_ref.dtype)

    return preprocess, prepare, advance, finish


def _gla(x, gup, gb, nw):
    nb, rows = GLA_STEP
    tb = _mixer_block(x.shape[1], rows)
    nc = tb // CHUNK
    scratch = [((256, 128), F32), ((tb, 128), F32), ((tb, 256), F32), ((tb, 128), BF16),
               ((nc * N_HEADS * GLA_DV, 128), F32), ((nc * 8, 128), F32)]
    return _mixer_call(_gla_body, "gla", x, (gup, gb, nw), scratch, tb, nb)


def _gdn_body(x_ref, cw_ref, alog_ref, dtb_ref, nw_ref, y_ref,
              s_ref, xp_ref, q_ref, k_ref, v_ref, beta_ref, gd_ref,
              u_s, w_s, attn_s, qd_s, ke_s, gt_s, o_s):
    tb = x_ref.shape[0]
    first = pl.program_id(1) == 0

    @pl.when(first)
    def _():
        s_ref[...] = jnp.zeros(s_ref.shape, F32)
        xp_ref[0:8, :] = jnp.zeros((8, 768), F32)

    @pl.when(jnp.logical_not(first))
    def _():
        xp_ref[0:8, :] = xp_ref[tb:tb + 8, :]

    hsum = _head_ones(256, GDN_DK)

    def preprocess():
        xp_ref[8:tb + 8, :] = x_ref[:, 0:768]
        yield
        conv = cw_ref[0:1, :] * xp_ref[8 - (GDN_CONV - 1):8 - (GDN_CONV - 1) + tb, :]
        for j in range(1, GDN_CONV):
            yield
            conv = conv + cw_ref[j:j + 1, :] * xp_ref[8 - (GDN_CONV - 1) + j:8 - (GDN_CONV - 1) + j + tb, :]
        yield
        c = _silu(conv)
        q = c[:, 0:256]
        k = c[:, 256:512]
        yield
        q_ref[...] = q * lax.rsqrt(_seg_sum(q * q, hsum) + L2_EPS) * (GDN_DK ** -0.5)
        yield
        k_ref[...] = k * lax.rsqrt(_seg_sum(k * k, hsum) + L2_EPS)
        v_ref[...] = c[:, 512:768]
        yield
        gates = x_ref[:, 1024:1152]
        beta_ref[...] = _sigmoid(gates)
        gd_ref[...] = -jnp.exp(alog_ref[...]) * _softplus(gates + dtb_ref[...])

    ltri = _ltri(CHUNK)
    masks = _lane_masks(256, GDN_DK)
    expand = lambda off: (_iota((128, 256), 0) == _div(_iota((128, 256), 1), GDN_DK) + off).astype(BF16)
    exp_beta, exp_g = expand(0), expand(N_HEADS)
    r = _iota((256, 256), 0)
    cidx = _iota((256, 256), 1)
    same = _div(r, CHUNK) == _div(cidx, CHUNK)
    incl = jnp.logical_and(same, r >= cidx)
    strict = jnp.logical_and(same, r > cidx)
    eye = (r == cidx).astype(F32)
    bd = same.astype(F32)
    nw = nw_ref[...]

    def prepare(cis):
        n = range(len(cis))
        rows = [pl.ds(pl.multiple_of(ci * CHUNK, CHUNK), CHUNK) for ci in cis]
        srows = [pl.ds(pl.multiple_of(ci * (N_HEADS * CHUNK), N_HEADS * CHUNK), N_HEADS * CHUNK) for ci in cis]
        q = [q_ref[r, :] for r in rows]
        k = [k_ref[r, :] for r in rows]
        v = [v_ref[r, :] for r in rows]
        bexp = [_dot01_right(beta_ref[r, :], exp_beta) for r in rows]
        gcum = [_dot01_left(ltri, gd_ref[r, :]) for r in rows]
        yield
        gexp = [_dot01_right(g, exp_g) for g in gcum]
        kst = [_stack(x, masks) for x in k]
        kb = [k[i] * bexp[i] for i in n]
        n4 = N_HEADS * CHUNK
        yield
        both = [_dot_nt(jnp.concatenate([_stack(kb[i], masks), _stack(q[i], masks)], axis=0), kst[i]) for i in n]
        kk = [x[:n4] for x in both]
        qk = [x[n4:] for x in both]
        yield
        dec = []
        for g in gexp:
            gcol = jnp.sum(_stack(g, masks), axis=1, keepdims=True) * (1.0 / GDN_DK)
            grow = jnp.sum(eye * gcol, axis=0, keepdims=True)
            dec.append(jnp.exp(jnp.minimum(gcol - grow, 0.0)))
        for i in n:
            attn_s[srows[i], :] = jnp.where(incl, qk[i] * dec[i], 0.0).astype(BF16)
        yield
        t = yield from _neumann_inverse([-jnp.where(strict, kk[i] * dec[i], 0.0) for i in n], eye)
        yield
        u = [_dot(t[i], _stack(v[i] * bexp[i], masks)) for i in n]
        w = [_dot(t[i], _stack(kb[i] * jnp.exp(gexp[i]), masks)) for i in n]
        yield
        for i in n:
            u_s[rows[i], :] = _unstack(u[i], CHUNK)
            w_s[rows[i], :] = _unstack(w[i], CHUNK).astype(BF16)
            g_last = gexp[i][CHUNK - 1:CHUNK]
            qd_s[rows[i], :] = (q[i] * jnp.exp(gexp[i])).astype(BF16)
            ke_s[rows[i], :] = (k[i] * jnp.exp(g_last - gexp[i])).astype(BF16)
            gt_s[pl.ds(pl.multiple_of(cis[i] * 8, 8), 8), :] = jnp.broadcast_to(jnp.exp(g_last), (8, 256))

    def advance(ci):
        rows = pl.ds(pl.multiple_of(ci * CHUNK, CHUNK), CHUNK)
        srows = pl.ds(pl.multiple_of(ci * (N_HEADS * CHUNK), N_HEADS * CHUNK), N_HEADS * CHUNK)
        s = s_ref[...]
        sb = s.astype(BF16)
        ws = jnp.dot(w_s[rows, :], sb, preferred_element_type=F32)
        qs = jnp.dot(qd_s[rows, :], sb, preferred_element_type=F32)
        yield
        v_new = u_s[rows, :] - ws
        av = _dot(attn_s[srows, :], _stack(v_new, masks))
        kv = _dot_tn(ke_s[rows, :], v_new)
        yield
        o_s[rows, :] = qs + _unstack(av, CHUNK)
        s_ref[...] = s * gt_s[pl.ds(pl.multiple_of(ci * 8, 8), 1), :] + bd * kv

    def finish():
        o = o_s[...]
        ms = _seg_sum(o * o, hsum) * (1.0 / GDN_DK)
        y_ref[...] = (o * lax.rsqrt(ms + NORM_EPS) * nw * _silu(x_ref[:, 768:1024])).astype(y_ref.dtype)

    return preprocess, prepare, advance, finish


def _gdn(x, cw, alog, dtb, nw):
    nb, rows = DELTA_STEP
    tb = _mixer_block(x.shape[1], rows)
    nc = tb // CHUNK
    scratch = [((256, 256), F32), ((tb + 8, 768), F32),
               ((tb, 256), F32), ((tb, 256), F32), ((tb, 256), F32), ((tb, 128), F32), ((tb, 128), F32),
               ((tb, 256), F32), ((tb, 256), BF16), ((nc * N_HEADS * CHUNK, 256), BF16),
               ((tb, 256), BF16), ((tb, 256), BF16), ((nc * 8, 256), F32), ((tb, 256), F32)]
    return _mixer_call(_gdn_body, "gdn", x, (cw, alog, dtb, nw), scratch, tb, nb)


def _rwkv_body(x_ref, mu_ref, w2a2_ref, w0_ref, a0_ref, g2_ref, kk_ref, ka_ref, rk_ref,
               lnw_ref, lnb_ref, y_ref,
               s_ref, xp_ref, r_s, k_s, v_s, kk_s, b_s, lw_s, g_s,
               at_s, z_s, arb_s, yv_s, rt_s, be_s, vk_s, gt_s, y_s):
    tb = x_ref.shape[0]
    first = pl.program_id(1) == 0

    @pl.when(first)
    def _():
        s_ref[...] = jnp.zeros(s_ref.shape, F32)
        xp_ref[0:8, :] = jnp.zeros((8, W_RWKV), F32)

    @pl.when(jnp.logical_not(first))
    def _():
        xp_ref[0:8, :] = xp_ref[tb:tb + 8, :]

    hsum = _head_ones(256, RWKV_N)

    def preprocess():
        x = x_ref[...]
        xp_ref[8:tb + 8, :] = x
        yield
        z = x + (xp_ref[7:tb + 7, :] - x) * mu_ref[...]
        r = z[:, 0:256]
        k = z[:, 256:512]
        wa = z[:, 768:896]
        wa = jnp.where(_iota(wa.shape, 1) < 64, jnp.tanh(wa), wa)
        yield
        pre = _dot(wa, w2a2_ref[...])
        yield
        w_log = -_softplus(-(w0_ref[...] + pre[:, 0:256])) - 0.5
        lw_s[...] = -jnp.exp(w_log)
        yield
        a = _sigmoid(a0_ref[...] + pre[:, 256:512])
        kkv = k * kk_ref[...]
        yield
        kkn = kkv * lax.rsqrt(_seg_sum(kkv * kkv, hsum) + L2_EPS)
        yield
        r_s[...] = r
        k_s[...] = k * (1.0 + (a - 1.0) * ka_ref[...])
        v_s[...] = z[:, 512:768]
        kk_s[...] = kkn
        b_s[...] = kkn * a
        yield
        g_s[...] = _dot(_sigmoid(z[:, 896:1152]), g2_ref[...])

    ltri = _ltri(CHUNK)
    masks = _lane_masks(256, RWKV_N)
    rr = _iota((256, 256), 0)
    cc = _iota((256, 256), 1)
    same = _div(rr, CHUNK) == _div(cc, CHUNK)
    incl = jnp.logical_and(same, rr >= cc)
    strict = jnp.logical_and(same, rr > cc)
    eye = (rr == cc).astype(F32)
    bd = same.astype(F32)
    rk = rk_ref[...]
    lnw = lnw_ref[...]
    lnb = lnb_ref[...]

    n4 = N_HEADS * CHUNK

    def prepare(cis):
        n = range(len(cis))
        rows = [pl.ds(pl.multiple_of(ci * CHUNK, CHUNK), CHUNK) for ci in cis]
        srows = [pl.ds(pl.multiple_of(ci * n4, n4), n4) for ci in cis]
        lw = [lw_s[r, :] for r in rows]
        gl = [_dot01_left(ltri, x) for x in lw]
        v = [v_s[r, :] for r in rows]
        k = [k_s[r, :] for r in rows]
        b = [b_s[r, :] for r in rows]
        yield
        e_neg = [jnp.exp(-g) for g in gl]
        a_st = [_stack(-kk_s[rows[i], :] * jnp.exp(gl[i] - lw[i]), masks) for i in n]
        r_t = [r_s[rows[i], :] * jnp.exp(gl[i]) for i in n]
        yield
        amat = []
        for i in n:
            lhs = jnp.concatenate([a_st[i], _stack(r_t[i], masks)], axis=0)
            rhs = jnp.concatenate([_stack(b[i] * e_neg[i], masks), _stack(k[i] * e_neg[i], masks)], axis=0)
            amat.append(_dot_nt(lhs, rhs))
        yield
        vst = [_stack(x, masks) for x in v]
        both = [_dot(jnp.concatenate([jnp.where(strict, amat[i][0:n4, n4:], 0.0),
                                      jnp.where(incl, amat[i][n4:, n4:], 0.0)], axis=0), vst[i]) for i in n]
        av = [x[:n4] for x in both]
        yv = [x[n4:] for x in both]
        yield
        t = yield from _neumann_inverse([jnp.where(strict, m[0:n4, 0:n4], 0.0) for m in amat], eye)
        yield
        z = [_dot(t[i], av[i]) for i in n]
        at = [_dot(t[i], a_st[i]) for i in n]
        yield
        for i in n:
            g_last = gl[i][CHUNK - 1:CHUNK]
            e_end = jnp.exp(g_last - gl[i])
            z_s[srows[i], :] = z[i]
            at_s[srows[i], :] = at[i].astype(BF16)
            arb_s[srows[i], :] = jnp.where(incl, amat[i][n4:, 0:n4], 0.0).astype(BF16)
            yv_s[rows[i], :] = _unstack(yv[i], CHUNK)
            rt_s[rows[i], :] = r_t[i].astype(BF16)
            be_s[rows[i], :] = (b[i] * e_end).astype(BF16)
            vk_s[srows[i], :] = bd * _dot_tn(v[i], k[i] * e_end)
            gt_s[pl.ds(pl.multiple_of(cis[i] * 8, 8), 8), :] = jnp.broadcast_to(jnp.exp(g_last), (8, 256))

    def advance(ci):
        rows = pl.ds(pl.multiple_of(ci * CHUNK, CHUNK), CHUNK)
        srows = pl.ds(pl.multiple_of(ci * n4, n4), n4)
        s = s_ref[...]
        sb = s.astype(BF16)
        u_st = _dot_nt(at_s[srows, :], sb) + z_s[srows, :]
        rs = _dot_nt(rt_s[rows, :], sb)
        yield
        au = _dot(arb_s[srows, :], u_st)
        ub = _dot_tn(_unstack(u_st, CHUNK), be_s[rows, :])
        yield
        y_s[rows, :] = rs + _unstack(au, CHUNK) + yv_s[rows, :]
        s_ref[...] = s * gt_s[pl.ds(pl.multiple_of(ci * 8, 8), 1), :] + bd * ub + vk_s[srows, :]

    def finish():
        y = y_s[...]
        v = v_s[...]
        mean = _seg_sum(y, hsum) * (1.0 / RWKV_N)
        d = y - mean
        var = _seg_sum(d * d, hsum) * (1.0 / RWKV_N)
        yn = d * lax.rsqrt(var + RWKV_LN_EPS) * lnw + lnb
        bonus = _seg_sum(r_s[...] * k_s[...] * rk, hsum) * v
        y_ref[...] = ((yn + bonus) * g_s[...]).astype(y_ref.dtype)

    return preprocess, prepare, advance, finish


def _rwkv(x, mu, w2a2, w0, a0, g2, kk, ka, rk, lnw, lnb):
    nb, rows = DELTA_STEP
    tb = _mixer_block(x.shape[1], rows)
    nc = tb // CHUNK
    stacked = lambda dt: ((nc * N_HEADS * CHUNK, 256), dt)
    scratch = ([((256, 256), F32), ((tb + 8, W_RWKV), F32)] + [((tb, 256), F32)] * 7
               + [stacked(BF16), stacked(F32), stacked(BF16), ((tb, 256), F32), ((tb, 256), BF16),
                  ((tb, 256), BF16), stacked(F32), ((nc * 8, 256), F32), ((tb, 256), F32)])
    return _mixer_call(_rwkv_body, "rwkv", x, (mu, w2a2, w0, a0, g2, kk, ka, rk, lnw, lnb), scratch, tb, nb)


def _mla_prep_kernel(x_ref, qnw_ref, kvnw_ref, wqa_ref, wqb_ref, wk_ref, wv_ref, ea_ref, eb_ref,
                     c1_ref, s1_ref, q_ref, k_ref, v_ref):
    x = x_ref[...]
    qn = _rms(x[:, 0:256], qnw_ref[...]).astype(BF16)
    kvn = _rms(x[:, 256:384], kvnw_ref[...]).astype(BF16)
    kpe = x[:, 384:512].astype(BF16)
    c1 = c1_ref[...]
    s1 = s1_ref[...]
    qa = jnp.dot(qn, wqa_ref[...], preferred_element_type=F32)
    qb = jnp.dot(qn, wqb_ref[...], preferred_element_type=F32)
    kn = jnp.dot(kvn, wk_ref[...], preferred_element_type=F32)
    kp = (jnp.dot(kpe, ea_ref[...], preferred_element_type=F32) * c1
          + jnp.dot(kpe, eb_ref[...], preferred_element_type=F32) * s1)
    ones_lane = ((_iota((1, N_HEADS * MLA_SLOT), 1) & (MLA_SLOT - 1)) == MLA_V).astype(F32)
    v_ref[...] = (jnp.dot(kvn, wv_ref[...], preferred_element_type=F32) + ones_lane).astype(BF16)
    for h in range(N_HEADS):
        sl = slice(h * MLA_SLOT, (h + 1) * MLA_SLOT)
        q_ref[:, sl] = (qa[:, sl] * c1 + qb[:, sl] * s1).astype(BF16)
        k_ref[:, sl] = (kn[:, sl] + kp).astype(BF16)


def _mla_prep(x, qnw, kvnw, wqa, wqb, wk, wv, ea, eb, c1, s1):
    b, tp, _ = x.shape
    tm = _pick(tp, (640, 128, 64))
    full = lambda a: pl.BlockSpec(a.shape, lambda i, j: (0, 0))
    wide = N_HEADS * MLA_SLOT
    out = pl.BlockSpec((None, tm, wide), lambda i, j: (i, j, 0))
    tab = pl.BlockSpec((tm, MLA_SLOT), lambda i, j: (j, 0))
    return pl.pallas_call(
        _mla_prep_kernel,
        grid=(b, tp // tm),
        in_specs=[pl.BlockSpec((None, tm, W_MLA), lambda i, j: (i, j, 0))]
        + [full(a) for a in (qnw, kvnw, wqa, wqb, wk, wv, ea, eb)] + [tab, tab],
        out_specs=[out, out, out],
        out_shape=[jax.ShapeDtypeStruct((b, tp, wide), BF16)] * 3,
        compiler_params=_cparams("parallel", "parallel"),
        name="mla_prep",
    )(x, qnw, kvnw, wqa, wqb, wk, wv, ea, eb, c1, s1)


FLASH_HEADS = 4


def _flash_kernel(q_ref, k_ref, v_ref, o_ref, m_ref, acc_ref):
    qi = pl.program_id(2)
    t = q_ref.shape[0]
    m_ref[...] = jnp.full(m_ref.shape, NEG_INF, F32)
    acc_ref[...] = jnp.zeros(acc_ref.shape, F32)

    def block(start, width, diagonal):
        rows = pl.ds(pl.multiple_of(start, t), width)
        for h in range(FLASH_HEADS):
            sl = slice(h * MLA_SLOT, (h + 1) * MLA_SLOT)
            s = lax.dot_general(q_ref[:, sl], k_ref[rows, sl], (((1,), (1,)), ((), ())),
                                preferred_element_type=F32)
            if diagonal:
                s = jnp.where(_iota(s.shape, 0) >= _iota(s.shape, 1), s, NEG_INF)
            m_old = m_ref[h]
            m_new = jnp.maximum(m_old, jnp.max(s, axis=-1, keepdims=True))
            p = jnp.concatenate([jnp.exp2(s[:, c * LANE:(c + 1) * LANE] - m_new)
                                 for c in range(s.shape[1] // LANE)], axis=1).astype(BF16)
            acc_ref[h] = (jnp.exp2(m_old - m_new) * acc_ref[h]
                          + jnp.dot(p, v_ref[rows, sl], preferred_element_type=F32))
            m_ref[h] = m_new

    def double_block(j, carry):
        block(j * (2 * t), 2 * t, False)
        return carry

    lax.fori_loop(0, qi // 2, double_block, 0)

    @pl.when(qi % 2 == 1)
    def _():
        block((qi - 1) * t, t, False)

    block(qi * t, t, True)
    for h in range(FLASH_HEADS):
        acc = acc_ref[h]
        o_ref[:, h * MLA_SLOT:(h + 1) * MLA_SLOT] = (acc / acc[:, MLA_V:MLA_V + 1]).astype(o_ref.dtype)


def _flash(q, k, v):
    b, tp, wide = q.shape
    t = _pick(tp, (640, 128, 64))
    w = FLASH_HEADS * MLA_SLOT
    qspec = pl.BlockSpec((None, t, w), lambda i, h, qi: (i, qi, h))
    kspec = pl.BlockSpec((None, tp, w), lambda i, h, qi: (i, 0, h))
    return pl.pallas_call(
        _flash_kernel,
        grid=(b, wide // w, tp // t),
        in_specs=[qspec, kspec, kspec],
        out_specs=qspec,
        out_shape=jax.ShapeDtypeStruct((b, tp, wide), BF16),
        scratch_shapes=[pltpu.VMEM((FLASH_HEADS, t, LANE), F32), pltpu.VMEM((FLASH_HEADS, t, MLA_SLOT), F32)],
        compiler_params=_cparams("parallel", "parallel", "arbitrary"),
        name="mla_flash",
    )(q, k, v)


def _router_kernel(h_ref, nw_ref, wr_ref, xn_ref, info_ref):
    xn = _rms(h_ref[...], nw_ref[...])
    bits = pltpu.bitcast(xn.astype(BF16).astype(F32), jnp.uint32)
    half = D_MODEL // 2
    xn_ref[...] = (bits[:, :half] >> 16) | bits[:, half:]
    logits = jnp.dot(xn, wr_ref[...], preferred_element_type=F32, precision=lax.Precision.HIGHEST)
    lane = _iota(logits.shape, 1).astype(F32)
    valid = lane < N_EXPERTS
    l0 = jnp.where(valid, logits, NEG_INF)
    m1 = jnp.max(l0, axis=-1, keepdims=True)
    i1 = jnp.min(jnp.where(l0 == m1, lane, float(LANE)), axis=-1, keepdims=True)
    l1 = jnp.where(lane == i1, NEG_INF, l0)
    m2 = jnp.max(l1, axis=-1, keepdims=True)
    i2 = jnp.min(jnp.where(l1 == m2, lane, float(LANE)), axis=-1, keepdims=True)
    e2 = jnp.exp(m2 - m1)
    g1 = 1.0 / (1.0 + e2)
    g2 = e2 / (1.0 + e2)
    info = jnp.where(lane == 0, i1, 0.0)
    info = jnp.where(lane == 1, i2, info)
    info = jnp.where(lane == 2, g1, info)
    info = jnp.where(lane == 3, g2, info)
    info_ref[...] = info


def _router(h, nw, wr):
    n = h.shape[0]
    tm = _pick(n, (512, 256, 128, 64))
    return pl.pallas_call(
        _router_kernel,
        grid=(n // tm,),
        in_specs=[pl.BlockSpec((tm, D_MODEL), lambda i: (i, 0)), pl.BlockSpec(nw.shape, lambda i: (0, 0)),
                  pl.BlockSpec(wr.shape, lambda i: (0, 0))],
        out_specs=[pl.BlockSpec((tm, D_MODEL // 2), lambda i: (i, 0)), pl.BlockSpec((tm, LANE), lambda i: (i, 0))],
        out_shape=[jax.ShapeDtypeStruct((n, D_MODEL // 2), jnp.uint32), jax.ShapeDtypeStruct((n, LANE), F32)],
        compiler_params=_cparams("parallel"),
        name="moe_router",
    )(h, nw, wr)


def _expert_kernel(be_ref, x_ref, gate_ref, wg_ref, wu_ref, wd_ref, out_ref, acc_ref):
    f = pl.program_id(1)

    @pl.when(f == 0)
    def _():
        acc_ref[...] = jnp.zeros(acc_ref.shape, F32)

    packed = x_ref[...]
    low = pltpu.bitcast(packed << 16, F32)
    high = pltpu.bitcast(packed & jnp.uint32(0xFFFF0000), F32)
    xb = jnp.concatenate([low, high], axis=1).astype(BF16)
    a = jnp.dot(xb, wg_ref[...], preferred_element_type=F32)
    b = jnp.dot(xb, wu_ref[...], preferred_element_type=F32)
    acc_ref[...] += jnp.dot((_silu(a) * b).astype(BF16), wd_ref[...], preferred_element_type=F32)

    @pl.when(f == pl.num_programs(1) - 1)
    def _():
        out_ref[...] = acc_ref[...] * gate_ref[...]


def _experts(block_expert, x_rows, row_gate, wg, wu, wd, tm):
    cap = x_rows.shape[0]
    dff = wg.shape[2]
    tf = _pick(dff, (1792, 512, 256, 128))
    grid_spec = pltpu.PrefetchScalarGridSpec(
        num_scalar_prefetch=1,
        grid=(cap // tm, dff // tf),
        in_specs=[pl.BlockSpec((tm, D_MODEL // 2), lambda i, f, be: (i, 0)),
                  pl.BlockSpec((tm, 1), lambda i, f, be: (i, 0)),
                  pl.BlockSpec((None, D_MODEL, tf), lambda i, f, be: (be[i], 0, f)),
                  pl.BlockSpec((None, D_MODEL, tf), lambda i, f, be: (be[i], 0, f)),
                  pl.BlockSpec((None, tf, D_MODEL), lambda i, f, be: (be[i], f, 0))],
        out_specs=pl.BlockSpec((tm, D_MODEL), lambda i, f, be: (i, 0)),
        scratch_shapes=[pltpu.VMEM((tm, D_MODEL), F32)],
    )
    return pl.pallas_call(
        _expert_kernel,
        grid_spec=grid_spec,
        out_shape=jax.ShapeDtypeStruct((cap, D_MODEL), F32),
        compiler_params=_cparams("parallel", "arbitrary"),
        name="moe_experts",
    )(block_expert, x_rows, row_gate, wg, wu, wd)


def _combine_kernel(h_ref, ya_ref, yb_ref, out_ref):
    out_ref[...] = h_ref[...] + (ya_ref[...] + yb_ref[...])


def _combine(h, y2):
    n = h.shape[0]
    tm = _pick(n, (1024, 512, 256, 128, 64))
    spec = pl.BlockSpec((tm, D_MODEL), lambda i: (i, 0))
    second = pl.BlockSpec((tm, D_MODEL), lambda i: (i + n // tm, 0))
    return pl.pallas_call(
        _combine_kernel, grid=(n // tm,), in_specs=[spec, spec, second], out_specs=spec,
        out_shape=jax.ShapeDtypeStruct((n, D_MODEL), F32),
        compiler_params=_cparams("parallel"), name="moe_combine",
    )(h, y2, y2)


def _sc_gather(table, idx):
    rows = idx.shape[0]
    d = table.shape[1]
    info = plsc.get_sparse_core_info()
    workers = info.num_cores * info.num_subcores
    assert rows % (workers * SC_CHUNK) == 0, (rows, workers)
    per_worker = rows // workers
    mesh = plsc.VectorSubcoreMesh(core_axis_name="c", subcore_axis_name="s")

    @functools.partial(
        pl.kernel, mesh=mesh, out_type=jax.ShapeDtypeStruct((rows, d), table.dtype),
        scratch_types=[pltpu.VMEM((SC_CHUNK,), jnp.int32), pltpu.VMEM((SC_CHUNK, d), table.dtype),
                       pltpu.SemaphoreType.DMA],
        name="sc_gather")
    def gather(table_hbm, idx_hbm, out_hbm, idx_v, rows_v, sem):
        base = (lax.axis_index("s") * info.num_cores + lax.axis_index("c")) * per_worker

        @pl.loop(0, per_worker // SC_CHUNK)
        def _(i):
            off = pl.multiple_of(base + i * SC_CHUNK, 8)
            pltpu.sync_copy(idx_hbm.at[pl.ds(off, SC_CHUNK)], idx_v)
            pltpu.async_copy(table_hbm.at[idx_v], rows_v, sem).wait()
            pltpu.sync_copy(rows_v, out_hbm.at[pl.ds(off, SC_CHUNK)])

    return gather(table, idx)


def _moe(h, nw, router, wg, wu, wd, defer_combine=False):
    n = h.shape[0]
    tm = _pick(n, (512, 64))
    wr = jnp.pad(router.astype(F32), ((0, 0), (0, LANE - N_EXPERTS)))
    xn, info = _router(h, nw, wr)
    expert = info[:, 0:2].astype(jnp.int32).reshape(-1)
    gate = info[:, 2:4].reshape(-1)
    n_assign = 2 * n
    order = jnp.argsort(expert)
    onehot = (expert[:, None] == jnp.arange(N_EXPERTS, dtype=jnp.int32)[None, :]).astype(jnp.int32)
    running = jnp.cumsum(onehot, axis=0)
    counts = running[-1]
    padded = (counts + tm - 1) // tm * tm
    pad_end = jnp.cumsum(padded)
    pad_start = pad_end - padded
    start = jnp.cumsum(counts) - counts
    n_blocks = -(-n_assign // tm) + N_EXPERTS
    cap = n_blocks * tm
    block_start = jnp.arange(n_blocks, dtype=jnp.int32) * tm
    block_expert = jnp.minimum(jnp.sum(block_start[:, None] >= pad_end[None, :], axis=1), N_EXPERTS - 1)
    block_expert = block_expert.astype(jnp.int32)
    rank = (block_start - pad_start[block_expert])[:, None] + jnp.arange(tm, dtype=jnp.int32)[None, :]
    valid = (rank < counts[block_expert][:, None]).reshape(cap)
    src = order[jnp.clip(start[block_expert][:, None] + rank, 0, n_assign - 1).reshape(cap)]
    row_token = jnp.where(valid, src // 2, 0)
    row_gate = jnp.where(valid, gate[src], 0.0)
    dest = jnp.sum(onehot * (pad_start[None, :] + running - 1), axis=1)
    x_rows = _sc_gather(xn, row_token)
    y_rows = _experts(block_expert, x_rows, row_gate[:, None], wg, wu, wd, tm)
    y2 = _sc_gather(y_rows, dest.reshape(n, 2).T.reshape(n_assign))
    return (h, y2) if defer_combine else _combine(h, y2)


def _pad_cols(a, width):
    return jnp.pad(a, ((0, 0), (0, width - a.shape[1])))


def _row(a, width=None):
    a = a.reshape(1, -1).astype(F32)
    return a if width is None else _pad_cols(a, width)


def _rope_tables(tp):
    pos = jnp.arange(tp, dtype=F32)
    inv_freq = ROPE_THETA ** (-jnp.arange(0, MLA_ROPE, 2, dtype=F32) / MLA_ROPE)
    ang = pos[:, None] * inv_freq[None, :]
    cos, sin = jnp.cos(ang), jnp.sin(ang)
    ones = jnp.ones((tp, MLA_NOPE), F32)
    zeros = jnp.zeros((tp, MLA_SLOT - MLA_NOPE - MLA_ROPE), F32)
    c1 = jnp.concatenate([ones, cos, cos, zeros], axis=1)
    s1 = jnp.concatenate([0.0 * ones, -sin, sin, zeros], axis=1)
    return c1, s1


def _mla_weights(w_uq, w_ukv):
    half = MLA_ROPE // 2
    scale = (MLA_NOPE + MLA_ROPE) ** -0.5 * math.log2(math.e)
    zq =jnp.zeros((w_uq.shape[0], MLA_SLOT - MLA_NOPE - MLA_ROPE), F32)
    zn = jnp.zeros((w_uq.shape[0], MLA_NOPE), F32)
    zk = jnp.zeros((w_ukv.shape[0], MLA_SLOT - MLA_NOPE), F32)
    wqa, wqb, wk, wv = [], [], [], []
    for h in range(N_HEADS):
        q = w_uq[:, h * 96:(h + 1) * 96] * scale
        nope, x1, x2 = q[:, :MLA_NOPE], q[:, MLA_NOPE:MLA_NOPE + half], q[:, MLA_NOPE + half:]
        wqa += [nope, x1, x2, zq]
        wqb += [zn, x2, x1, zq]
        kv = w_ukv[:, h * 128:(h + 1) * 128]
        wk += [kv[:, :MLA_NOPE], zk]
        wv += [kv[:, MLA_NOPE:], zk]
    cat = lambda parts: jnp.concatenate(parts, axis=1).astype(BF16)
    ea = np.zeros((MLA_SLOT, MLA_SLOT), np.float32)
    eb = np.zeros((MLA_SLOT, MLA_SLOT), np.float32)
    for i in range(MLA_ROPE):
        ea[i, MLA_NOPE + i] = 1.0
        eb[(i + half) % MLA_ROPE, MLA_NOPE + i] = 1.0
    return cat(wqa), cat(wqb), cat(wk), cat(wv), jnp.asarray(ea, BF16), jnp.asarray(eb, BF16)


def _branch_weights(w_branch):
    wb = w_branch.astype(BF16)
    z = jnp.zeros((MLA_SLOT - MLA_V, D_MODEL), BF16)
    parts = []
    for h in range(N_HEADS):
        parts += [wb[1, h * MLA_V:(h + 1) * MLA_V], z]
    return wb[0], jnp.concatenate(parts, axis=0), wb[2], wb[3]


def _token_mixing(h, b, tp, p, l, tables):
    w_in = p["w_in"][l]
    col = lambda i, j: w_in[:, _OFF[i]:_OFF[j]]
    wg = jnp.concatenate([col(0, 3), col(4, 5), _pad_cols(col(3, 4), LANE)], axis=1).astype(BF16)
    wm = jnp.concatenate([col(5, 7), _pad_cols(col(7, 8), LANE)], axis=1).astype(BF16)
    wd = jnp.concatenate([col(8, 10), _pad_cols(col(10, 12), LANE)], axis=1).astype(BF16)
    wr = _pad_cols(col(12, 13), W_RWKV).astype(BF16)
    nw = _row(p["norm_mix"][l])
    xg, xm, xd, xr = _inproj(h, nw, wg, wm, wd, wr)
    shape3 = lambda a: a.reshape(b, tp, a.shape[1])

    gup = jnp.pad(p["gla_gate_up"][l], ((0, LANE - 16), (0, 0))).astype(BF16)
    y_gla = _gla(shape3(xg), gup, _row(p["gla_gate_bias"][l]), _row(jnp.tile(p["gla_norm"][l], N_HEADS)))

    wqa, wqb, wk, wv, ea, eb = _mla_weights(p["mla_w_uq"][l], p["mla_w_ukv"][l])
    q, k, v = _mla_prep(shape3(xm), _row(p["mla_q_norm"][l]), _row(p["mla_kv_norm"][l]),
                        wqa, wqb, wk, wv, ea, eb, *tables)
    y_mla = _flash(q, k, v)

    lanes4 = lambda a: jnp.pad(a.reshape(1, N_HEADS).astype(F32), ((0, 0), (N_HEADS, LANE - 2 * N_HEADS)))
    y_gdn = _gdn(shape3(xd), p["gdn_conv"][l].astype(F32), lanes4(p["gdn_a_log"][l]),
                 lanes4(p["gdn_dt_bias"][l]), _row(jnp.tile(p["gdn_norm"][l], N_HEADS)))

    w2a2 = jnp.zeros((LANE, 512), F32)
    w2a2 = w2a2.at[0:64, 0:256].set(p["rwkv_w2"][l]).at[64:128, 256:512].set(p["rwkv_a2"][l]).astype(BF16)
    g2 = jnp.pad(p["rwkv_g2"][l], ((0, 256 - 160), (0, 0))).astype(BF16)
    y_rwkv = _rwkv(shape3(xr), _row(p["rwkv_mu"][l], W_RWKV), w2a2, _row(p["rwkv_w0"][l]),
                   _row(p["rwkv_a0"][l]), g2, _row(p["rwkv_k_k"][l]), _row(p["rwkv_k_a"][l]),
                   _row(p["rwkv_r_k"][l]), _row(p["rwkv_ln_w"][l]), _row(p["rwkv_ln_b"][l]))

    flat = lambda a: a.reshape(b * tp, a.shape[2])
    ys = [flat(y_gla), flat(y_mla), flat(y_gdn), flat(y_rwkv)]
    return _merge(h, nw, col(13, 14).astype(BF16), ys, _branch_weights(p["w_branch"][l]),
                  p["w_out"][l].astype(BF16))


def kernel(x, meta_tokens, norm_mix, w_in, gla_gate_up, gla_gate_bias, gla_norm, mla_q_norm, mla_w_uq, mla_kv_norm, mla_w_ukv, gdn_conv, gdn_a_log, gdn_dt_bias, gdn_norm, rwkv_mu, rwkv_w0, rwkv_w2, rwkv_a0, rwkv_a2, rwkv_g2, rwkv_k_k, rwkv_k_a, rwkv_r_k, rwkv_ln_w, rwkv_ln_b, w_branch, w_out, norm_ffn, ffn_w_gate, ffn_w_up, ffn_w_down, moe_router, moe_w_gate, moe_w_up, moe_w_down, norm_final):
    p = dict(norm_mix=norm_mix, w_in=w_in, gla_gate_up=gla_gate_up, gla_gate_bias=gla_gate_bias,
             gla_norm=gla_norm, mla_q_norm=mla_q_norm, mla_w_uq=mla_w_uq, mla_kv_norm=mla_kv_norm,
             mla_w_ukv=mla_w_ukv, gdn_conv=gdn_conv, gdn_a_log=gdn_a_log, gdn_dt_bias=gdn_dt_bias,
             gdn_norm=gdn_norm, rwkv_mu=rwkv_mu, rwkv_w0=rwkv_w0, rwkv_w2=rwkv_w2, rwkv_a0=rwkv_a0,
             rwkv_a2=rwkv_a2, rwkv_g2=rwkv_g2, rwkv_k_k=rwkv_k_k, rwkv_k_a=rwkv_k_a, rwkv_r_k=rwkv_r_k,
             rwkv_ln_w=rwkv_ln_w, rwkv_ln_b=rwkv_ln_b, w_branch=w_branch, w_out=w_out)
    b, seq, d = x.shape
    t_real = N_META + seq
    tp = -(-t_real // 128) * 128
    meta = jnp.broadcast_to(meta_tokens[None].astype(x.dtype), (b, N_META, d))
    h = jnp.concatenate([meta, x, jnp.zeros((b, tp - t_real, d), x.dtype)], axis=1).reshape(b * tp, d)
    tables = _rope_tables(tp)
    depth = norm_mix.shape[0]
    y2 = None
    for l in range(depth):
        h = _token_mixing(h, b, tp, p, l, tables)
        nw = _row(norm_ffn[l])
        if l % 2 == 0:
            h = _ffn(h, nw, ffn_w_gate[l // 2].astype(BF16), ffn_w_up[l // 2].astype(BF16),
                     ffn_w_down[l // 2].astype(BF16))
        elif l == depth - 1:
            h, y2 = _moe(h, nw, moe_router[l // 2], moe_w_gate[l // 2].astype(BF16),
                         moe_w_up[l // 2].astype(BF16), moe_w_down[l // 2].astype(BF16), defer_combine=True)
        else:
            h = _moe(h, nw, moe_router[l // 2], moe_w_gate[l // 2].astype(BF16),
                     moe_w_up[l // 2].astype(BF16), moe_w_down[l // 2].astype(BF16))
    return _final(h, _row(norm_final), b, tp, seq, y2)
```

```python
import functools
import math

import jax
import jax.numpy as jnp
import numpy as np
from jax import lax
from jax.experimental import pallas as pl
from jax.experimental.pallas import tpu as pltpu
from jax.experimental.pallas import tpu_sc as plsc

F32 = jnp.float32
BF16 = jnp.bfloat16

D_MODEL = 1024
N_META = 16
N_HEADS = 4
GLA_DK = 32
GLA_DV = 64
GLA_TAU = 16.0
MLA_NOPE = 64
MLA_ROPE = 32
MLA_V = 64
MLA_SLOT = 128
ROPE_THETA = 10000.0
GDN_DK = 64
GDN_CONV = 4
RWKV_N = 64
RWKV_LN_EPS = RWKV_N * 1e-5
CHUNK = 64
SUB = 16
PREP_GROUP = 5
GLA_STEP = (2, 320)
DELTA_STEP = (4, 128)
N_EXPERTS = 8
SC_CHUNK = 64
NORM_EPS = 1e-6
L2_EPS = 1e-6
NEG_INF = -1e30
EXP_CLAMP = 80.0

LANE = 128
VMEM_LIMIT = 56 * 1024 * 1024

_OFF = np.cumsum([0, 128, 128, 256, 16, 256, 256, 128, 32, 768, 256, 4, 4, 1056, 4096]).tolist()
W_GLA, W_MLA, W_GDN, W_RWKV = 896, 512, 1152, 1152


def _cparams(*sem):
    return pltpu.CompilerParams(dimension_semantics=sem, vmem_limit_bytes=VMEM_LIMIT)


def _pick(n, prefs):
    for p in prefs:
        if n % p == 0:
            return p
    raise ValueError(f"no tile for {n}")


def _dot(a, b):
    return jnp.dot(a.astype(BF16), b.astype(BF16), preferred_element_type=F32)


def _dot_nt(a, b):
    return lax.dot_general(a.astype(BF16), b.astype(BF16), (((1,), (1,)), ((), ())),
                           preferred_element_type=F32)


def _dot_tn(a, b):
    return lax.dot_general(a.astype(BF16), b.astype(BF16), (((0,), (0,)), ((), ())),
                           preferred_element_type=F32)


def _split3(x):
    hi = x.astype(BF16)
    r1 = x - hi.astype(F32)
    mid = r1.astype(BF16)
    lo = (r1 - mid.astype(F32)).astype(BF16)
    return hi, mid, lo


def _dot01_left(m01, x):
    return sum(jnp.dot(m01, p, preferred_element_type=F32) for p in _split3(x))


def _dot01_right(x, m01):
    return sum(jnp.dot(p, m01, preferred_element_type=F32) for p in _split3(x))


def _seg_sum(x, m01):
    hi = x.astype(BF16)
    lo = (x - hi.astype(F32)).astype(BF16)
    return jnp.dot(hi, m01, preferred_element_type=F32) + jnp.dot(lo, m01, preferred_element_type=F32)


def _iota(shape, dim):
    return lax.broadcasted_iota(jnp.int32, shape, dim)


def _div(x, w):
    return x >> int(math.log2(w))


def _ltri(n):
    return (_iota((n, n), 0) >= _iota((n, n), 1)).astype(BF16)


def _head_ones(n, w):
    return (_div(_iota((n, n), 0), w) == _div(_iota((n, n), 1), w)).astype(BF16)


def _lane_masks(width, w):
    lane = _div(_iota((1, width), 1), w)
    return [(lane == h).astype(F32) for h in range(width // w)]


def _stack(x, masks):
    return jnp.concatenate([x * m for m in masks], axis=0)


def _unstack(y, n):
    out = y[0:n]
    for h in range(1, y.shape[0] // n):
        out = out + y[h * n:(h + 1) * n]
    return out


def _rms(x, w):
    return x * lax.rsqrt(jnp.mean(x * x, axis=-1, keepdims=True) + NORM_EPS) * w


def _sigmoid(x):
    return 1.0 / (1.0 + jnp.exp(-x))


def _silu(x):
    return x * _sigmoid(x)


def _softplus(x):
    return jnp.maximum(x, 0.0) + jnp.log(1.0 + jnp.exp(-jnp.abs(x)))


def _neumann_inverse(xs, eye):
    n = xs[0].shape[0]
    ts = [eye + x for x in xs]
    ps = [_dot(x, x) for x in xs]
    for _ in range(int(math.log2(CHUNK)) - 2):
        yield
        both = [_dot(jnp.concatenate([t, p], axis=0), p) for t, p in zip(ts, ps)]
        ts = [t + tp[:n] for t, tp in zip(ts, both)]
        ps = [tp[n:] for tp in both]
    yield
    return [t + _dot(t, p) for t, p in zip(ts, ps)]


def _for_chunks(n, body, group):
    def trip(i, carry):
        body([i * group + g for g in range(group)])
        return carry

    if n >= group:
        lax.fori_loop(0, n // group, trip, 0)
    if n % group:
        body(list(range(n - n % group, n)))


def _inproj_kernel(h_ref, nw_ref, wg_ref, wm_ref, wd_ref, wr_ref, og_ref, om_ref, od_ref, or_ref):
    xb = _rms(h_ref[...], nw_ref[...]).astype(BF16)
    og_ref[...] = jnp.dot(xb, wg_ref[...], preferred_element_type=F32)
    om_ref[...] = jnp.dot(xb, wm_ref[...], preferred_element_type=F32)
    od_ref[...] = jnp.dot(xb, wd_ref[...], preferred_element_type=F32)
    or_ref[...] = jnp.dot(xb, wr_ref[...], preferred_element_type=F32)


def _inproj(h, nw, wg, wm, wd, wr):
    n = h.shape[0]
    tm = _pick(n, (512, 256, 128, 64))
    full = lambda a: pl.BlockSpec(a.shape, lambda i: (0, 0))
    row = lambda w: pl.BlockSpec((tm, w), lambda i: (i, 0))
    return pl.pallas_call(
        _inproj_kernel,
        grid=(n // tm,),
        in_specs=[row(D_MODEL), full(nw), full(wg), full(wm), full(wd), full(wr)],
        out_specs=[row(W_GLA), row(W_MLA), row(W_GDN), row(W_RWKV)],
        out_shape=[jax.ShapeDtypeStruct((n, w), F32) for w in (W_GLA, W_MLA, W_GDN, W_RWKV)],
        compiler_params=_cparams("parallel"),
        name="inproj",
    )(h, nw, wg, wm, wd, wr)


def _merge_kernel(h_ref, nw_ref, wgate_ref, yg_ref, ym_ref, yd_ref, yr_ref,
                  wbg_ref, wbm_ref, wbd_ref, wbr_ref, wout_ref, out_ref):
    x = h_ref[...]
    xb = _rms(x, nw_ref[...]).astype(BF16)
    acc = jnp.zeros(x.shape, F32)
    branches = ((yg_ref, wbg_ref), (ym_ref, wbm_ref), (yd_ref, wbd_ref), (yr_ref, wbr_ref))
    for i, (y_ref, wb_ref) in enumerate(branches):
        logits = jnp.dot(xb, wgate_ref[:, i * D_MODEL:(i + 1) * D_MODEL], preferred_element_type=F32)
        proj = jnp.dot(y_ref[...], wb_ref[...], preferred_element_type=F32)
        acc = acc + _sigmoid(logits) * proj
    out_ref[...] = x + jnp.dot(acc.astype(BF16), wout_ref[...], preferred_element_type=F32)


def _merge(h, nw, wgate, ys, wbs, wout):
    n = h.shape[0]
    tm = _pick(n, (512, 256, 128, 64))
    full = lambda a: pl.BlockSpec(a.shape, lambda i: (0, 0))
    row = lambda w: pl.BlockSpec((tm, w), lambda i: (i, 0))
    return pl.pallas_call(
        _merge_kernel,
        grid=(n // tm,),
        in_specs=[row(D_MODEL), full(nw), full(wgate)] + [row(y.shape[1]) for y in ys]
        + [full(w) for w in wbs] + [full(wout)],
        out_specs=row(D_MODEL),
        out_shape=jax.ShapeDtypeStruct((n, D_MODEL), F32),
        compiler_params=_cparams("parallel"),
        name="merge",
    )(h, nw, wgate, *ys, *wbs, wout)


def _ffn_kernel(h_ref, nw_ref, wg_ref, wu_ref, wd_ref, out_ref, xb_ref, acc_ref):
    f = pl.program_id(1)

    @pl.when(f == 0)
    def _():
        xb_ref[...] = _rms(h_ref[...], nw_ref[...]).astype(BF16)
        acc_ref[...] = jnp.zeros(acc_ref.shape, F32)

    xb = xb_ref[...]
    a = jnp.dot(xb, wg_ref[...], preferred_element_type=F32)
    b = jnp.dot(xb, wu_ref[...], preferred_element_type=F32)
    acc_ref[...] += jnp.dot((_silu(a) * b).astype(BF16), wd_ref[...], preferred_element_type=F32)

    @pl.when(f == pl.num_programs(1) - 1)
    def _():
        out_ref[...] = h_ref[...] + acc_ref[...]


def _ffn(h, nw, wg, wu, wd):
    n = h.shape[0]
    dff = wg.shape[1]
    tm = _pick(n, (512, 256, 128, 64))
    tf = _pick(dff, (1408, 512, 256, 128))
    return pl.pallas_call(
        _ffn_kernel,
        grid=(n // tm, dff // tf),
        in_specs=[pl.BlockSpec((tm, D_MODEL), lambda i, f: (i, 0)),
                  pl.BlockSpec(nw.shape, lambda i, f: (0, 0)),
                  pl.BlockSpec((D_MODEL, tf), lambda i, f: (0, f)),
                  pl.BlockSpec((D_MODEL, tf), lambda i, f: (0, f)),
                  pl.BlockSpec((tf, D_MODEL), lambda i, f: (f, 0))],
        out_specs=pl.BlockSpec((tm, D_MODEL), lambda i, f: (i, 0)),
        out_shape=jax.ShapeDtypeStruct((n, D_MODEL), F32),
        scratch_shapes=[pltpu.VMEM((tm, D_MODEL), BF16), pltpu.VMEM((tm, D_MODEL), F32)],
        compiler_params=_cparams("parallel", "arbitrary"),
        name="ffn",
    )(h, nw, wg, wu, wd)


def _final_kernel(nw_ref, *refs):
    out_ref = refs[-1]
    x = refs[0][...]
    for extra in refs[1:-1]:
        x = x + extra[...]
    out_ref[...] = _rms(x, nw_ref[...])


def _final(h, nw, b, tp, seq, y2=None):
    tq = _pick(seq, (1024, 512, 256, 128, 64))
    n = b * tp
    rows = lambda shift: pl.BlockSpec((pl.Element(tq), pl.Element(D_MODEL)),
                                      lambda i, j: (pl.multiple_of(shift + i * tp + N_META + j * tq, 8), 0))
    operands = [h] if y2 is None else [h, y2, y2]
    shifts = [0] if y2 is None else [0, 0, n]
    return pl.pallas_call(
        _final_kernel,
        grid=(b, seq // tq),
        in_specs=[pl.BlockSpec(nw.shape, lambda i, j: (0, 0))] + [rows(s) for s in shifts],
        out_specs=pl.BlockSpec((None, tq, D_MODEL), lambda i, j: (i, j, 0)),
        out_shape=jax.ShapeDtypeStruct((b, seq, D_MODEL), F32),
        compiler_params=_cparams("parallel", "parallel"),
        name="final_norm",
    )(nw, *operands)


_DONE = object()


def _interleave(generators):
    live = list(generators)
    while live:
        live = [g for g in live if next(g, _DONE) is not _DONE]


def _mixer_kernel(body, n_params, x_ref, *refs):
    params, y_ref, scratch = refs[:n_params], refs[n_params], refs[n_params + 1:]
    parts = [body(x_ref.at[b], *params, y_ref.at[b], *[s.at[b] for s in scratch]) for b in range(x_ref.shape[0])]
    n_chunks = x_ref.shape[1] // CHUNK
    if n_chunks <= PREP_GROUP:
        half = max(len(parts) // 2, 1)
        pending = []
        for group in [parts[i:i + half] for i in range(0, len(parts), half)]:
            _interleave([preprocess() for preprocess, _, _, _ in group] + pending)
            pending = [prepare(list(range(n_chunks))) for _, prepare, _, _ in group]
        _interleave(pending)
    else:
        for preprocess, _, _, _ in parts:
            _interleave([preprocess()])
        _for_chunks(n_chunks, lambda cis: [_interleave([prepare(cis)]) for _, prepare, _, _ in parts], PREP_GROUP)
    _for_chunks(n_chunks, lambda cis: [_interleave([advance(ci) for _, _, advance, _ in parts]) for ci in cis], 1)
    for _, _, _, finish in parts:
        finish()


def _mixer_block(tp, rows):
    return _pick(tp, (rows, 128, 64))


def _mixer_call(body, name, x, params, scratch, tb, nb):
    b, tp, width = x.shape
    nb = nb if b % nb == 0 else 1
    full = lambda a: pl.BlockSpec(a.shape, lambda i, j: (0, 0))
    return pl.pallas_call(
        functools.partial(_mixer_kernel, body, len(params)),
        grid=(b // nb, tp // tb),
        in_specs=[pl.BlockSpec((nb, tb, width), lambda i, j: (i, j, 0))] + [full(a) for a in params],
        out_specs=pl.BlockSpec((nb, tb, 256), lambda i, j: (i, j, 0)),
        out_shape=jax.ShapeDtypeStruct((b, tp, 256), BF16),
        scratch_shapes=[pltpu.VMEM((nb,) + shape, dtype) for shape, dtype in scratch],
        compiler_params=_cparams("parallel", "arbitrary"),
        name=name,
    )(x, *params)


def _gla_body(x_ref, gup_ref, gb_ref, nw_ref, y_ref, st_ref, la_ref, o_s, qg_s, kv_s, gt_s):
    @pl.when(pl.program_id(1) == 0)
    def _():
        st_ref[...] = jnp.zeros(st_ref.shape, F32)

    def preprocess():
        z = _dot(x_ref[:, 768:896], gup_ref[...]) + gb_ref[...]
        yield
        la_ref[...] = -_softplus(-z) * (1.0 / GLA_TAU)

    ltri = _ltri(CHUNK)
    qmasks = _lane_masks(N_HEADS * GLA_DK, GLA_DK)
    vmasks = _lane_masks(N_HEADS * GLA_DV, GLA_DV)
    bd = (_div(_iota((256, 128), 0), GLA_DV) == _div(_iota((256, 128), 1), GLA_DK)).astype(F32)
    hsum = _head_ones(N_HEADS * GLA_DV, GLA_DV)
    nw = nw_ref[...]

    n4 = N_HEADS * GLA_DV

    def prepare(cis):
        n = range(len(cis))
        rows = [pl.ds(pl.multiple_of(ci * CHUNK, CHUNK), CHUNK) for ci in cis]
        g = [_dot01_left(ltri, la_ref[r, :]) for r in rows]
        q = [x_ref[r, 0:128] * (GLA_DK ** -0.5) for r in rows]
        k = [x_ref[r, 128:256] for r in rows]
        v = [x_ref[r, 256:512] for r in rows]
        intra = [[] for _ in n]
        for s in range(CHUNK // SUB):
            lo, hi = s * SUB, (s + 1) * SUB
            sc = []
            for i in n:
                gs = jnp.zeros((1, 128), F32) if s == 0 else g[i][lo - 1:lo]
                qs = q[i][lo:hi] * jnp.exp(g[i][lo:hi] - gs)
                kt = k[i][:hi] * jnp.exp(jnp.minimum(gs - g[i][:hi], EXP_CLAMP))
                sc.append(_dot_nt(_stack(qs, qmasks), kt))
            causal = _iota(sc[0].shape, 1) <= lo + (_iota(sc[0].shape, 0) & (SUB - 1))
            yield
            p = [_dot(jnp.where(causal, sc[i], 0.0), v[i][:hi]) for i in n]
            for i in n:
                intra[i].append(sum(p[i][h * SUB:(h + 1) * SUB] * vmasks[h] for h in range(N_HEADS)))
            yield
        for i in n:
            g_last = g[i][CHUNK - 1:CHUNK]
            o_s[rows[i], :] = jnp.concatenate(intra[i], axis=0)
            qg_s[rows[i], :] = (q[i] * jnp.exp(g[i])).astype(BF16)
            kv_s[pl.ds(pl.multiple_of(cis[i] * n4, n4), n4), :] = bd * _dot_tn(v[i], k[i] * jnp.exp(g_last - g[i]))
            gt_s[pl.ds(pl.multiple_of(cis[i] * 8, 8), 8), :] = jnp.broadcast_to(jnp.exp(g_last), (8, 128))

    def advance(ci):
        rows = pl.ds(pl.multiple_of(ci * CHUNK, CHUNK), CHUNK)
        st = st_ref[...]
        inter = _dot_nt(qg_s[rows, :], st)
        yield
        o_s[rows, :] += inter
        st_ref[...] = (st * gt_s[pl.ds(pl.multiple_of(ci * 8, 8), 1), :]
                       + kv_s[pl.ds(pl.multiple_of(ci * n4, n4), n4), :])

    def finish():
        o = o_s[...]
        ms = _seg_sum(o * o, hsum) * (1.0 / GLA_DV)
        y_ref[...] = (o * lax.rsqrt(ms + NORM_EPS) * nw * _silu(x_ref[:, 512:768])).astype(y_ref.dtype)

    return preprocess, prepare, advance, finish


def _gla(x, gup, gb, nw):
    nb, rows = GLA_STEP
    tb = _mixer_block(x.shape[1], rows)
    nc = tb // CHUNK
    scratch = [((256, 128), F32), ((tb, 128), F32), ((tb, 256), F32), ((tb, 128), BF16),
               ((nc * N_HEADS * GLA_DV, 128), F32), ((nc * 8, 128), F32)]
    return _mixer_call(_gla_body, "gla", x, (gup, gb, nw), scratch, tb, nb)


def _gdn_body(x_ref, cw_ref, alog_ref, dtb_ref, nw_ref, y_ref,
              s_ref, xp_ref, q_ref, k_ref, v_ref, beta_ref, gd_ref,
              u_s, w_s, attn_s, qd_s, ke_s, gt_s, o_s):
    tb = x_ref.shape[0]
    first = pl.program_id(1) == 0

    @pl.when(first)
    def _():
        s_ref[...] = jnp.zeros(s_ref.shape, F32)
        xp_ref[0:8, :] = jnp.zeros((8, 768), F32)

    @pl.when(jnp.logical_not(first))
    def _():
        xp_ref[0:8, :] = xp_ref[tb:tb + 8, :]

    hsum = _head_ones(256, GDN_DK)

    def preprocess():
        xp_ref[8:tb + 8, :] = x_ref[:, 0:768]
        yield
        conv = cw_ref[0:1, :] * xp_ref[8 - (GDN_CONV - 1):8 - (GDN_CONV - 1) + tb, :]
        for j in range(1, GDN_CONV):
            yield
            conv = conv + cw_ref[j:j + 1, :] * xp_ref[8 - (GDN_CONV - 1) + j:8 - (GDN_CONV - 1) + j + tb, :]
        yield
        c = _silu(conv)
        q = c[:, 0:256]
        k = c[:, 256:512]
        yield
        q_ref[...] = q * lax.rsqrt(_seg_sum(q * q, hsum) + L2_EPS) * (GDN_DK ** -0.5)
        yield
        k_ref[...] = k * lax.rsqrt(_seg_sum(k * k, hsum) + L2_EPS)
        v_ref[...] = c[:, 512:768]
        yield
        gates = x_ref[:, 1024:1152]
        beta_ref[...] = _sigmoid(gates)
        gd_ref[...] = -jnp.exp(alog_ref[...]) * _softplus(gates + dtb_ref[...])

    ltri = _ltri(CHUNK)
    masks = _lane_masks(256, GDN_DK)
    expand = lambda off: (_iota((128, 256), 0) == _div(_iota((128, 256), 1), GDN_DK) + off).astype(BF16)
    exp_beta, exp_g = expand(0), expand(N_HEADS)
    r = _iota((256, 256), 0)
    cidx = _iota((256, 256), 1)
    same = _div(r, CHUNK) == _div(cidx, CHUNK)
    incl = jnp.logical_and(same, r >= cidx)
    strict = jnp.logical_and(same, r > cidx)
    eye = (r == cidx).astype(F32)
    bd = same.astype(F32)
    nw = nw_ref[...]

    def prepare(cis):
        n = range(len(cis))
        rows = [pl.ds(pl.multiple_of(ci * CHUNK, CHUNK), CHUNK) for ci in cis]
        srows = [pl.ds(pl.multiple_of(ci * (N_HEADS * CHUNK), N_HEADS * CHUNK), N_HEADS * CHUNK) for ci in cis]
        q = [q_ref[r, :] for r in rows]
        k = [k_ref[r, :] for r in rows]
        v = [v_ref[r, :] for r in rows]
        bexp = [_dot01_right(beta_ref[r, :], exp_beta) for r in rows]
        gcum = [_dot01_left(ltri, gd_ref[r, :]) for r in rows]
        yield
        gexp = [_dot01_right(g, exp_g) for g in gcum]
        kst = [_stack(x, masks) for x in k]
        kb = [k[i] * bexp[i] for i in n]
        n4 = N_HEADS * CHUNK
        yield
        both = [_dot_nt(jnp.concatenate([_stack(kb[i], masks), _stack(q[i], masks)], axis=0), kst[i]) for i in n]
        kk = [x[:n4] for x in both]
        qk = [x[n4:] for x in both]
        yield
        dec = []
        for g in gexp:
            gcol = jnp.sum(_stack(g, masks), axis=1, keepdims=True) * (1.0 / GDN_DK)
            grow = jnp.sum(eye * gcol, axis=0, keepdims=True)
            dec.append(jnp.exp(jnp.minimum(gcol - grow, 0.0)))
        for i in n:
            attn_s[srows[i], :] = jnp.where(incl, qk[i] * dec[i], 0.0).astype(BF16)
        yield
        t = yield from _neumann_inverse([-jnp.where(strict, kk[i] * dec[i], 0.0) for i in n], eye)
        yield
        u = [_dot(t[i], _stack(v[i] * bexp[i], masks)) for i in n]
        w = [_dot(t[i], _stack(kb[i] * jnp.exp(gexp[i]), masks)) for i in n]
        yield
        for i in n:
            u_s[rows[i], :] = _unstack(u[i], CHUNK)
            w_s[rows[i], :] = _unstack(w[i], CHUNK).astype(BF16)
            g_last = gexp[i][CHUNK - 1:CHUNK]
            qd_s[rows[i], :] = (q[i] * jnp.exp(gexp[i])).astype(BF16)
            ke_s[rows[i], :] = (k[i] * jnp.exp(g_last - gexp[i])).astype(BF16)
            gt_s[pl.ds(pl.multiple_of(cis[i] * 8, 8), 8), :] = jnp.broadcast_to(jnp.exp(g_last), (8, 256))

    def advance(ci):
        rows = pl.ds(pl.multiple_of(ci * CHUNK, CHUNK), CHUNK)
        srows = pl.ds(pl.multiple_of(ci * (N_HEADS * CHUNK), N_HEADS * CHUNK), N_HEADS * CHUNK)
        s = s_ref[...]
        sb = s.astype(BF16)
        ws = jnp.dot(w_s[rows, :], sb, preferred_element_type=F32)
        qs = jnp.dot(qd_s[rows, :], sb, preferred_element_type=F32)
        yield
        v_new = u_s[rows, :] - ws
        av = _dot(attn_s[srows, :], _stack(v_new, masks))
        kv = _dot_tn(ke_s[rows, :], v_new)
        yield
        o_s[rows, :] = qs + _unstack(av, CHUNK)
        s_ref[...] = s * gt_s[pl.ds(pl.multiple_of(ci * 8, 8), 1), :] + bd * kv

    def finish():
        o = o_s[...]
        ms = _seg_sum(o * o, hsum) * (1.0 / GDN_DK)
        y_ref[...] = (o * lax.rsqrt(ms + NORM_EPS) * nw * _silu(x_ref[:, 768:1024])).astype(y_ref.dtype)

    return preprocess, prepare, advance, finish


def _gdn(x, cw, alog, dtb, nw):
    nb, rows = DELTA_STEP
    tb = _mixer_block(x.shape[1], rows)
    nc = tb // CHUNK
    scratch = [((256, 256), F32), ((tb + 8, 768), F32),
               ((tb, 256), F32), ((tb, 256), F32), ((tb, 256), F32), ((tb, 128), F32), ((tb, 128), F32),
               ((tb, 256), F32), ((tb, 256), BF16), ((nc * N_HEADS * CHUNK, 256), BF16),
               ((tb, 256), BF16), ((tb, 256), BF16), ((nc * 8, 256), F32), ((tb, 256), F32)]
    return _mixer_call(_gdn_body, "gdn", x, (cw, alog, dtb, nw), scratch, tb, nb)


def _rwkv_body(x_ref, mu_ref, w2a2_ref, w0_ref, a0_ref, g2_ref, kk_ref, ka_ref, rk_ref,
               lnw_ref, lnb_ref, y_ref,
               s_ref, xp_ref, r_s, k_s, v_s, kk_s, b_s, lw_s, g_s,
               at_s, z_s, arb_s, yv_s, rt_s, be_s, vk_s, gt_s, y_s):
    tb = x_ref.shape[0]
    first = pl.program_id(1) == 0

    @pl.when(first)
    def _():
        s_ref[...] = jnp.zeros(s_ref.shape, F32)
        xp_ref[0:8, :] = jnp.zeros((8, W_RWKV), F32)

    @pl.when(jnp.logical_not(first))
    def _():
        xp_ref[0:8, :] = xp_ref[tb:tb + 8, :]

    hsum = _head_ones(256, RWKV_N)

    def preprocess():
        x = x_ref[...]
        xp_ref[8:tb + 8, :] = x
        yield
        z = x + (xp_ref[7:tb + 7, :] - x) * mu_ref[...]
        r = z[:, 0:256]
        k = z[:, 256:512]
        wa = z[:, 768:896]
        wa = jnp.where(_iota(wa.shape, 1) < 64, jnp.tanh(wa), wa)
        yield
        pre = _dot(wa, w2a2_ref[...])
        yield
        w_log = -_softplus(-(w0_ref[...] + pre[:, 0:256])) - 0.5
        lw_s[...] = -jnp.exp(w_log)
        yield
        a = _sigmoid(a0_ref[...] + pre[:, 256:512])
        kkv = k * kk_ref[...]
        yield
        kkn = kkv * lax.rsqrt(_seg_sum(kkv * kkv, hsum) + L2_EPS)
        yield
        r_s[...] = r
        k_s[...] = k * (1.0 + (a - 1.0) * ka_ref[...])
        v_s[...] = z[:, 512:768]
        kk_s[...] = kkn
        b_s[...] = kkn * a
        yield
        g_s[...] = _dot(_sigmoid(z[:, 896:1152]), g2_ref[...])

    ltri = _ltri(CHUNK)
    masks = _lane_masks(256, RWKV_N)
    rr = _iota((256, 256), 0)
    cc = _iota((256, 256), 1)
    same = _div(rr, CHUNK) == _div(cc, CHUNK)
    incl = jnp.logical_and(same, rr >= cc)
    strict = jnp.logical_and(same, rr > cc)
    eye = (rr == cc).astype(F32)
    bd = same.astype(F32)
    rk = rk_ref[...]
    lnw = lnw_ref[...]
    lnb = lnb_ref[...]

    n4 = N_HEADS * CHUNK

    def prepare(cis):
        n = range(len(cis))
        rows = [pl.ds(pl.multiple_of(ci * CHUNK, CHUNK), CHUNK) for ci in cis]
        srows = [pl.ds(pl.multiple_of(ci * n4, n4), n4) for ci in cis]
        lw = [lw_s[r, :] for r in rows]
        gl = [_dot01_left(ltri, x) for x in lw]
        v = [v_s[r, :] for r in rows]
        k = [k_s[r, :] for r in rows]
        b = [b_s[r, :] for r in rows]
        yield
        e_neg = [jnp.exp(-g) for g in gl]
        a_st = [_stack(-kk_s[rows[i], :] * jnp.exp(gl[i] - lw[i]), masks) for i in n]
        r_t = [r_s[rows[i], :] * jnp.exp(gl[i]) for i in n]
        yield
        amat = []
        for i in n:
            lhs = jnp.concatenate([a_st[i], _stack(r_t[i], masks)], axis=0)
            rhs = jnp.concatenate([_stack(b[i] * e_neg[i], masks), _stack(k[i] * e_neg[i], masks)], axis=0)
            amat.append(_dot_nt(lhs, rhs))
        yield
        vst = [_stack(x, masks) for x in v]
        both = [_dot(jnp.concatenate([jnp.where(strict, amat[i][0:n4, n4:], 0.0),
                                      jnp.where(incl, amat[i][n4:, n4:], 0.0)], axis=0), vst[i]) for i in n]
        av = [x[:n4] for x in both]
        yv = [x[n4:] for x in both]
        yield
        t = yield from _neumann_inverse([jnp.where(strict, m[0:n4, 0:n4], 0.0) for m in amat], eye)
        yield
        z = [_dot(t[i], av[i]) for i in n]
        at = [_dot(t[i], a_st[i]) for i in n]
        yield
        for i in n:
            g_last = gl[i][CHUNK - 1:CHUNK]
            e_end = jnp.exp(g_last - gl[i])
            z_s[srows[i], :] = z[i]
            at_s[srows[i], :] = at[i].astype(BF16)
            arb_s[srows[i], :] = jnp.where(incl, amat[i][n4:, 0:n4], 0.0).astype(BF16)
            yv_s[rows[i], :] = _unstack(yv[i], CHUNK)
            rt_s[rows[i], :] = r_t[i].astype(BF16)
            be_s[rows[i], :] = (b[i] * e_end).astype(BF16)
            vk_s[srows[i], :] = bd * _dot_tn(v[i], k[i] * e_end)
            gt_s[pl.ds(pl.multiple_of(cis[i] * 8, 8), 8), :] = jnp.broadcast_to(jnp.exp(g_last), (8, 256))

    def advance(ci):
        rows = pl.ds(pl.multiple_of(ci * CHUNK, CHUNK), CHUNK)
        srows = pl.ds(pl.multiple_of(ci * n4, n4), n4)
        s = s_ref[...]
        sb = s.astype(BF16)
        u_st = _dot_nt(at_s[srows, :], sb) + z_s[srows, :]
        rs = _dot_nt(rt_s[rows, :], sb)
        yield
        au = _dot(arb_s[srows, :], u_st)
        ub = _dot_tn(_unstack(u_st, CHUNK), be_s[rows, :])
        yield
        y_s[rows, :] = rs + _unstack(au, CHUNK) + yv_s[rows, :]
        s_ref[...] = s * gt_s[pl.ds(pl.multiple_of(ci * 8, 8), 1), :] + bd * ub + vk_s[srows, :]

    def finish():
        y = y_s[...]
        v = v_s[...]
        mean = _seg_sum(y, hsum) * (1.0 / RWKV_N)
        d = y - mean
        var = _seg_sum(d * d, hsum) * (1.0 / RWKV_N)
        yn = d * lax.rsqrt(var + RWKV_LN_EPS) * lnw + lnb
        bonus = _seg_sum(r_s[...] * k_s[...] * rk, hsum) * v
        y_ref[...] = ((yn + bonus) * g_s[...]).astype(y_ref.dtype)

    return preprocess, prepare, advance, finish


def _rwkv(x, mu, w2a2, w0, a0, g2, kk, ka, rk, lnw, lnb):
    nb, rows = DELTA_STEP
    tb = _mixer_block(x.shape[1], rows)
    nc = tb // CHUNK
    stacked = lambda dt: ((nc * N_HEADS * CHUNK, 256), dt)
    scratch = ([((256, 256), F32), ((tb + 8, W_RWKV), F32)] + [((tb, 256), F32)] * 7
               + [stacked(BF16), stacked(F32), stacked(BF16), ((tb, 256), F32), ((tb, 256), BF16),
                  ((tb, 256), BF16), stacked(F32), ((nc * 8, 256), F32), ((tb, 256), F32)])
    return _mixer_call(_rwkv_body, "rwkv", x, (mu, w2a2, w0, a0, g2, kk, ka, rk, lnw, lnb), scratch, tb, nb)


def _mla_prep_kernel(x_ref, qnw_ref, kvnw_ref, wqa_ref, wqb_ref, wk_ref, wv_ref, ea_ref, eb_ref,
                     c1_ref, s1_ref, q_ref, k_ref, v_ref):
    x = x_ref[...]
    qn = _rms(x[:, 0:256], qnw_ref[...]).astype(BF16)
    kvn = _rms(x[:, 256:384], kvnw_ref[...]).astype(BF16)
    kpe = x[:, 384:512].astype(BF16)
    c1 = c1_ref[...]
    s1 = s1_ref[...]
    qa = jnp.dot(qn, wqa_ref[...], preferred_element_type=F32)
    qb = jnp.dot(qn, wqb_ref[...], preferred_element_type=F32)
    kn = jnp.dot(kvn, wk_ref[...], preferred_element_type=F32)
    kp = (jnp.dot(kpe, ea_ref[...], preferred_element_type=F32) * c1
          + jnp.dot(kpe, eb_ref[...], preferred_element_type=F32) * s1)
    ones_lane = ((_iota((1, N_HEADS * MLA_SLOT), 1) & (MLA_SLOT - 1)) == MLA_V).astype(F32)
    v_ref[...] = (jnp.dot(kvn, wv_ref[...], preferred_element_type=F32) + ones_lane).astype(BF16)
    for h in range(N_HEADS):
        sl = slice(h * MLA_SLOT, (h + 1) * MLA_SLOT)
        q_ref[:, sl] = (qa[:, sl] * c1 + qb[:, sl] * s1).astype(BF16)
        k_ref[:, sl] = (kn[:, sl] + kp).astype(BF16)


def _mla_prep(x, qnw, kvnw, wqa, wqb, wk, wv, ea, eb, c1, s1):
    b, tp, _ = x.shape
    tm = _pick(tp, (640, 128, 64))
    full = lambda a: pl.BlockSpec(a.shape, lambda i, j: (0, 0))
    wide = N_HEADS * MLA_SLOT
    out = pl.BlockSpec((None, tm, wide), lambda i, j: (i, j, 0))
    tab = pl.BlockSpec((tm, MLA_SLOT), lambda i, j: (j, 0))
    return pl.pallas_call(
        _mla_prep_kernel,
        grid=(b, tp // tm),
        in_specs=[pl.BlockSpec((None, tm, W_MLA), lambda i, j: (i, j, 0))]
        + [full(a) for a in (qnw, kvnw, wqa, wqb, wk, wv, ea, eb)] + [tab, tab],
        out_specs=[out, out, out],
        out_shape=[jax.ShapeDtypeStruct((b, tp, wide), BF16)] * 3,
        compiler_params=_cparams("parallel", "parallel"),
        name="mla_prep",
    )(x, qnw, kvnw, wqa, wqb, wk, wv, ea, eb, c1, s1)


FLASH_HEADS = 4


def _flash_kernel(q_ref, k_ref, v_ref, o_ref, m_ref, acc_ref):
    qi = pl.program_id(2)
    t = q_ref.shape[0]
    m_ref[...] = jnp.full(m_ref.shape, NEG_INF, F32)
    acc_ref[...] = jnp.zeros(acc_ref.shape, F32)

    def block(start, width, diagonal):
        rows = pl.ds(pl.multiple_of(start, t), width)
        for h in range(FLASH_HEADS):
            sl = slice(h * MLA_SLOT, (h + 1) * MLA_SLOT)
            s = lax.dot_general(q_ref[:, sl], k_ref[rows, sl], (((1,), (1,)), ((), ())),
                                preferred_element_type=F32)
            if diagonal:
                s = jnp.where(_iota(s.shape, 0) >= _iota(s.shape, 1) - (width - t), s, NEG_INF)
            m_old = m_ref[h]
            m_new = jnp.maximum(m_old, jnp.max(s, axis=-1, keepdims=True))
            p = jnp.concatenate([jnp.exp2(s[:, c * LANE:(c + 1) * LANE] - m_new)
                                 for c in range(s.shape[1] // LANE)], axis=1).astype(BF16)
            acc_ref[h] = (jnp.exp2(m_old - m_new) * acc_ref[h]
                          + jnp.dot(p, v_ref[rows, sl], preferred_element_type=F32))
            m_ref[h] = m_new

    def double_block(j, carry):
        block(j * (2 * t), 2 * t, False)
        return carry

    lax.fori_loop(0, qi // 2, double_block, 0)

    @pl.when(qi % 2 == 1)
    def _():
        block((qi - 1) * t, 2 * t, True)

    @pl.when(qi % 2 == 0)
    def _():
        block(qi * t, t, True)

    for h in range(FLASH_HEADS):
        acc = acc_ref[h]
        o_ref[:, h * MLA_SLOT:(h + 1) * MLA_SLOT] = (acc / acc[:, MLA_V:MLA_V + 1]).astype(o_ref.dtype)


def _flash(q, k, v):
    b, tp, wide = q.shape
    t = _pick(tp, (640, 128, 64))
    w = FLASH_HEADS * MLA_SLOT
    qspec = pl.BlockSpec((None, t, w), lambda i, h, qi: (i, qi, h))
    kspec = pl.BlockSpec((None, tp, w), lambda i, h, qi: (i, 0, h))
    return pl.pallas_call(
        _flash_kernel,
        grid=(b, wide // w, tp // t),
        in_specs=[qspec, kspec, kspec],
        out_specs=qspec,
        out_shape=jax.ShapeDtypeStruct((b, tp, wide), BF16),
        scratch_shapes=[pltpu.VMEM((FLASH_HEADS, t, LANE), F32), pltpu.VMEM((FLASH_HEADS, t, MLA_SLOT), F32)],
        compiler_params=_cparams("parallel", "parallel", "arbitrary"),
        name="mla_flash",
    )(q, k, v)


def _router_kernel(h_ref, nw_ref, wr_ref, xn_ref, info_ref):
    xn = _rms(h_ref[...], nw_ref[...])
    bits = pltpu.bitcast(xn.astype(BF16).astype(F32), jnp.uint32)
    half = D_MODEL // 2
    xn_ref[...] = (bits[:, :half] >> 16) | bits[:, half:]
    logits = jnp.dot(xn, wr_ref[...], preferred_element_type=F32, precision=lax.Precision.HIGHEST)
    lane = _iota(logits.shape, 1).astype(F32)
    valid = lane < N_EXPERTS
    l0 = jnp.where(valid, logits, NEG_INF)
    m1 = jnp.max(l0, axis=-1, keepdims=True)
    i1 = jnp.min(jnp.where(l0 == m1, lane, float(LANE)), axis=-1, keepdims=True)
    l1 = jnp.where(lane == i1, NEG_INF, l0)
    m2 = jnp.max(l1, axis=-1, keepdims=True)
    i2 = jnp.min(jnp.where(l1 == m2, lane, float(LANE)), axis=-1, keepdims=True)
    e2 = jnp.exp(m2 - m1)
    g1 = 1.0 / (1.0 + e2)
    g2 = e2 / (1.0 + e2)
    info = jnp.where(lane == 0, i1, 0.0)
    info = jnp.where(lane == 1, i2, info)
    info = jnp.where(lane == 2, g1, info)
    info = jnp.where(lane == 3, g2, info)
    info_ref[...] = info


def _router(h, nw, wr):
    n = h.shape[0]
    tm = _pick(n, (512, 256, 128, 64))
    return pl.pallas_call(
        _router_kernel,
        grid=(n // tm,),
        in_specs=[pl.BlockSpec((tm, D_MODEL), lambda i: (i, 0)), pl.BlockSpec(nw.shape, lambda i: (0, 0)),
                  pl.BlockSpec(wr.shape, lambda i: (0, 0))],
        out_specs=[pl.BlockSpec((tm, D_MODEL // 2), lambda i: (i, 0)), pl.BlockSpec((tm, LANE), lambda i: (i, 0))],
        out_shape=[jax.ShapeDtypeStruct((n, D_MODEL // 2), jnp.uint32), jax.ShapeDtypeStruct((n, LANE), F32)],
        compiler_params=_cparams("parallel"),
        name="moe_router",
    )(h, nw, wr)


def _expert_kernel(be_ref, x_ref, gate_ref, wg_ref, wu_ref, wd_ref, out_ref, acc_ref):
    f = pl.program_id(1)

    @pl.when(f == 0)
    def _():
        acc_ref[...] = jnp.zeros(acc_ref.shape, F32)

    packed = x_ref[...]
    low = pltpu.bitcast(packed << 16, F32)
    high = pltpu.bitcast(packed & jnp.uint32(0xFFFF0000), F32)
    xb = jnp.concatenate([low, high], axis=1).astype(BF16)
    a = jnp.dot(xb, wg_ref[...], preferred_element_type=F32)
    b = jnp.dot(xb, wu_ref[...], preferred_element_type=F32)
    acc_ref[...] += jnp.dot((_silu(a) * b).astype(BF16), wd_ref[...], preferred_element_type=F32)

    @pl.when(f == pl.num_programs(1) - 1)
    def _():
        out_ref[...] = acc_ref[...] * gate_ref[...]


def _experts(block_expert, x_rows, row_gate, wg, wu, wd, tm):
    cap = x_rows.shape[0]
    dff = wg.shape[2]
    tf = _pick(dff, (1792, 512, 256, 128))
    grid_spec = pltpu.PrefetchScalarGridSpec(
        num_scalar_prefetch=1,
        grid=(cap // tm, dff // tf),
        in_specs=[pl.BlockSpec((tm, D_MODEL // 2), lambda i, f, be: (i, 0)),
                  pl.BlockSpec((tm, 1), lambda i, f, be: (i, 0)),
                  pl.BlockSpec((None, D_MODEL, tf), lambda i, f, be: (be[i], 0, f)),
                  pl.BlockSpec((None, D_MODEL, tf), lambda i, f, be: (be[i], 0, f)),
                  pl.BlockSpec((None, tf, D_MODEL), lambda i, f, be: (be[i], f, 0))],
        out_specs=pl.BlockSpec((tm, D_MODEL), lambda i, f, be: (i, 0)),
        scratch_shapes=[pltpu.VMEM((tm, D_MODEL), F32)],
    )
    return pl.pallas_call(
        _expert_kernel,
        grid_spec=grid_spec,
        out_shape=jax.ShapeDtypeStruct((cap, D_MODEL), F32),
        compiler_params=_cparams("parallel", "arbitrary"),
        name="moe_experts",
    )(block_expert, x_rows, row_gate, wg, wu, wd)


def _combine_kernel(h_ref, ya_ref, yb_ref, out_ref):
    out_ref[...] = h_ref[...] + (ya_ref[...] + yb_ref[...])


def _combine(h, y2):
    n = h.shape[0]
    tm = _pick(n, (1024, 512, 256, 128, 64))
    spec = pl.BlockSpec((tm, D_MODEL), lambda i: (i, 0))
    second = pl.BlockSpec((tm, D_MODEL), lambda i: (i + n // tm, 0))
    return pl.pallas_call(
        _combine_kernel, grid=(n // tm,), in_specs=[spec, spec, second], out_specs=spec,
        out_shape=jax.ShapeDtypeStruct((n, D_MODEL), F32),
        compiler_params=_cparams("parallel"), name="moe_combine",
    )(h, y2, y2)


def _sc_gather(table, idx):
    rows = idx.shape[0]
    d = table.shape[1]
    info = plsc.get_sparse_core_info()
    workers = info.num_cores * info.num_subcores
    assert rows % (workers * SC_CHUNK) == 0, (rows, workers)
    per_worker = rows // workers
    mesh = plsc.VectorSubcoreMesh(core_axis_name="c", subcore_axis_name="s")

    @functools.partial(
        pl.kernel, mesh=mesh, out_type=jax.ShapeDtypeStruct((rows, d), table.dtype),
        scratch_types=[pltpu.VMEM((SC_CHUNK,), jnp.int32), pltpu.VMEM((SC_CHUNK, d), table.dtype),
                       pltpu.SemaphoreType.DMA],
        name="sc_gather")
    def gather(table_hbm, idx_hbm, out_hbm, idx_v, rows_v, sem):
        base = (lax.axis_index("s") * info.num_cores + lax.axis_index("c")) * per_worker

        @pl.loop(0, per_worker // SC_CHUNK)
        def _(i):
            off = pl.multiple_of(base + i * SC_CHUNK, 8)
            pltpu.sync_copy(idx_hbm.at[pl.ds(off, SC_CHUNK)], idx_v)
            pltpu.async_copy(table_hbm.at[idx_v], rows_v, sem).wait()
            pltpu.sync_copy(rows_v, out_hbm.at[pl.ds(off, SC_CHUNK)])

    return gather(table, idx)


def _moe(h, nw, router, wg, wu, wd, defer_combine=False):
    n = h.shape[0]
    tm = _pick(n, (512, 64))
    wr = jnp.pad(router.astype(F32), ((0, 0), (0, LANE - N_EXPERTS)))
    xn, info = _router(h, nw, wr)
    expert = info[:, 0:2].astype(jnp.int32).reshape(-1)
    gate = info[:, 2:4].reshape(-1)
    n_assign = 2 * n
    order = jnp.argsort(expert)
    onehot = (expert[:, None] == jnp.arange(N_EXPERTS, dtype=jnp.int32)[None, :]).astype(jnp.int32)
    running = jnp.cumsum(onehot, axis=0)
    counts = running[-1]
    padded = (counts + tm - 1) // tm * tm
    pad_end = jnp.cumsum(padded)
    pad_start = pad_end - padded
    start = jnp.cumsum(counts) - counts
    n_blocks = -(-n_assign // tm) + N_EXPERTS
    cap = n_blocks * tm
    block_start = jnp.arange(n_blocks, dtype=jnp.int32) * tm
    block_expert = jnp.minimum(jnp.sum(block_start[:, None] >= pad_end[None, :], axis=1), N_EXPERTS - 1)
    block_expert = block_expert.astype(jnp.int32)
    rank = (block_start - pad_start[block_expert])[:, None] + jnp.arange(tm, dtype=jnp.int32)[None, :]
    valid = (rank < counts[block_expert][:, None]).reshape(cap)
    src = order[jnp.clip(start[block_expert][:, None] + rank, 0, n_assign - 1).reshape(cap)]
    row_token = jnp.where(valid, src // 2, 0)
    row_gate = jnp.where(valid, gate[src], 0.0)
    dest = jnp.sum(onehot * (pad_start[None, :] + running - 1), axis=1)
    x_rows = _sc_gather(xn, row_token)
    y_rows = _experts(block_expert, x_rows, row_gate[:, None], wg, wu, wd, tm)
    y2 = _sc_gather(y_rows, dest.reshape(n, 2).T.reshape(n_assign))
    return (h, y2) if defer_combine else _combine(h, y2)


def _pad_cols(a, width):
    return jnp.pad(a, ((0, 0), (0, width - a.shape[1])))


def _row(a, width=None):
    a = a.reshape(1, -1).astype(F32)
    return a if width is None else _pad_cols(a, width)


def _rope_tables(tp):
    pos = jnp.arange(tp, dtype=F32)
    inv_freq = ROPE_THETA ** (-jnp.arange(0, MLA_ROPE, 2, dtype=F32) / MLA_ROPE)
    ang = pos[:, None] * inv_freq[None, :]
    cos, sin = jnp.cos(ang), jnp.sin(ang)
    ones = jnp.ones((tp, MLA_NOPE), F32)
    zeros = jnp.zeros((tp, MLA_SLOT - MLA_NOPE - MLA_ROPE), F32)
    c1 = jnp.concatenate([ones, cos, cos, zeros], axis=1)
    s1 = jnp.concatenate([0.0 * ones, -sin, sin, zeros], axis=1)
    return c1, s1


def _mla_weights(w_uq, w_ukv):
    half = MLA_ROPE // 2
    scale = (MLA_NOPE + MLA_ROPE) ** -0.5 * math.log2(math.e)
    zq =jnp.zeros((w_uq.shape[0], MLA_SLOT - MLA_NOPE - MLA_ROPE), F32)
    zn = jnp.zeros((w_uq.shape[0], MLA_NOPE), F32)
    zk = jnp.zeros((w_ukv.shape[0], MLA_SLOT - MLA_NOPE), F32)
    wqa, wqb, wk, wv = [], [], [], []
    for h in range(N_HEADS):
        q = w_uq[:, h * 96:(h + 1) * 96] * scale
        nope, x1, x2 = q[:, :MLA_NOPE], q[:, MLA_NOPE:MLA_NOPE + half], q[:, MLA_NOPE + half:]
        wqa += [nope, x1, x2, zq]
        wqb += [zn, x2, x1, zq]
        kv = w_ukv[:, h * 128:(h + 1) * 128]
        wk += [kv[:, :MLA_NOPE], zk]
        wv += [kv[:, MLA_NOPE:], zk]
    cat = lambda parts: jnp.concatenate(parts, axis=1).astype(BF16)
    ea = np.zeros((MLA_SLOT, MLA_SLOT), np.float32)
    eb = np.zeros((MLA_SLOT, MLA_SLOT), np.float32)
    for i in range(MLA_ROPE):
        ea[i, MLA_NOPE + i] = 1.0
        eb[(i + half) % MLA_ROPE, MLA_NOPE + i] = 1.0
    return cat(wqa), cat(wqb), cat(wk), cat(wv), jnp.asarray(ea, BF16), jnp.asarray(eb, BF16)


def _branch_weights(w_branch):
    wb = w_branch.astype(BF16)
    z = jnp.zeros((MLA_SLOT - MLA_V, D_MODEL), BF16)
    parts = []
    for h in range(N_HEADS):
        parts += [wb[1, h * MLA_V:(h + 1) * MLA_V], z]
    return wb[0], jnp.concatenate(parts, axis=0), wb[2], wb[3]


def _token_mixing(h, b, tp, p, l, tables):
    w_in = p["w_in"][l]
    col = lambda i, j: w_in[:, _OFF[i]:_OFF[j]]
    wg = jnp.concatenate([col(0, 3), col(4, 5), _pad_cols(col(3, 4), LANE)], axis=1).astype(BF16)
    wm = jnp.concatenate([col(5, 7), _pad_cols(col(7, 8), LANE)], axis=1).astype(BF16)
    wd = jnp.concatenate([col(8, 10), _pad_cols(col(10, 12), LANE)], axis=1).astype(BF16)
    wr = _pad_cols(col(12, 13), W_RWKV).astype(BF16)
    nw = _row(p["norm_mix"][l])
    xg, xm, xd, xr = _inproj(h, nw, wg, wm, wd, wr)
    shape3 = lambda a: a.reshape(b, tp, a.shape[1])

    gup = jnp.pad(p["gla_gate_up"][l], ((0, LANE - 16), (0, 0))).astype(BF16)
    y_gla = _gla(shape3(xg), gup, _row(p["gla_gate_bias"][l]), _row(jnp.tile(p["gla_norm"][l], N_HEADS)))

    wqa, wqb, wk, wv, ea, eb = _mla_weights(p["mla_w_uq"][l], p["mla_w_ukv"][l])
    q, k, v = _mla_prep(shape3(xm), _row(p["mla_q_norm"][l]), _row(p["mla_kv_norm"][l]),
                        wqa, wqb, wk, wv, ea, eb, *tables)
    y_mla = _flash(q, k, v)

    lanes4 = lambda a: jnp.pad(a.reshape(1, N_HEADS).astype(F32), ((0, 0), (N_HEADS, LANE - 2 * N_HEADS)))
    y_gdn = _gdn(shape3(xd), p["gdn_conv"][l].astype(F32), lanes4(p["gdn_a_log"][l]),
                 lanes4(p["gdn_dt_bias"][l]), _row(jnp.tile(p["gdn_norm"][l], N_HEADS)))

    w2a2 = jnp.zeros((LANE, 512), F32)
    w2a2 = w2a2.at[0:64, 0:256].set(p["rwkv_w2"][l]).at[64:128, 256:512].set(p["rwkv_a2"][l]).astype(BF16)
    g2 = jnp.pad(p["rwkv_g2"][l], ((0, 256 - 160), (0, 0))).astype(BF16)
    y_rwkv = _rwkv(shape3(xr), _row(p["rwkv_mu"][l], W_RWKV), w2a2, _row(p["rwkv_w0"][l]),
                   _row(p["rwkv_a0"][l]), g2, _row(p["rwkv_k_k"][l]), _row(p["rwkv_k_a"][l]),
                   _row(p["rwkv_r_k"][l]), _row(p["rwkv_ln_w"][l]), _row(p["rwkv_ln_b"][l]))

    flat = lambda a: a.reshape(b * tp, a.shape[2])
    ys = [flat(y_gla), flat(y_mla), flat(y_gdn), flat(y_rwkv)]
    return _merge(h, nw, col(13, 14).astype(BF16), ys, _branch_weights(p["w_branch"][l]),
                  p["w_out"][l].astype(BF16))


def kernel(x, meta_tokens, norm_mix, w_in, gla_gate_up, gla_gate_bias, gla_norm, mla_q_norm, mla_w_uq, mla_kv_norm, mla_w_ukv, gdn_conv, gdn_a_log, gdn_dt_bias, gdn_norm, rwkv_mu, rwkv_w0, rwkv_w2, rwkv_a0, rwkv_a2, rwkv_g2, rwkv_k_k, rwkv_k_a, rwkv_r_k, rwkv_ln_w, rwkv_ln_b, w_branch, w_out, norm_ffn, ffn_w_gate, ffn_w_up, ffn_w_down, moe_router, moe_w_gate, moe_w_up, moe_w_down, norm_final):
    p = dict(norm_mix=norm_mix, w_in=w_in, gla_gate_up=gla_gate_up, gla_gate_bias=gla_gate_bias,
             gla_norm=gla_norm, mla_q_norm=mla_q_norm, mla_w_uq=mla_w_uq, mla_kv_norm=mla_kv_norm,
             mla_w_ukv=mla_w_ukv, gdn_conv=gdn_conv, gdn_a_log=gdn_a_log, gdn_dt_bias=gdn_dt_bias,
             gdn_norm=gdn_norm, rwkv_mu=rwkv_mu, rwkv_w0=rwkv_w0, rwkv_w2=rwkv_w2, rwkv_a0=rwkv_a0,
             rwkv_a2=rwkv_a2, rwkv_g2=rwkv_g2, rwkv_k_k=rwkv_k_k, rwkv_k_a=rwkv_k_a, rwkv_r_k=rwkv_r_k,
             rwkv_ln_w=rwkv_ln_w, rwkv_ln_b=rwkv_ln_b, w_branch=w_branch, w_out=w_out)
    b, seq, d = x.shape
    t_real = N_META + seq
    tp = -(-t_real // 128) * 128
    meta = jnp.broadcast_to(meta_tokens[None].astype(x.dtype), (b, N_META, d))
    h = jnp.concatenate([meta, x, jnp.zeros((b, tp - t_real, d), x.dtype)], axis=1).reshape(b * tp, d)
    tables = _rope_tables(tp)
    depth = norm_mix.shape[0]
    y2 = None
    for l in range(depth):
        h = _token_mixing(h, b, tp, p, l, tables)
        nw = _row(norm_ffn[l])
        if l % 2 == 0:
            h = _ffn(h, nw, ffn_w_gate[l // 2].astype(BF16), ffn_w_up[l // 2].astype(BF16),
                     ffn_w_down[l // 2].astype(BF16))
        elif l == depth - 1:
            h, y2 = _moe(h, nw, moe_router[l // 2], moe_w_gate[l // 2].astype(BF16),
                         moe_w_up[l // 2].astype(BF16), moe_w_down[l // 2].astype(BF16), defer_combine=True)
        else:
            h = _moe(h, nw, moe_router[l // 2], moe_w_gate[l // 2].astype(BF16),
                     moe_w_up[l // 2].astype(BF16), moe_w_down[l // 2].astype(BF16))
    return _final(h, _row(norm_final), b, tp, seq, y2)
```

```python
import functools
import math

import jax
import jax.numpy as jnp
import numpy as np
from jax import lax
from jax.experimental import pallas as pl
from jax.experimental.pallas import tpu as pltpu
from jax.experimental.pallas import tpu_sc as plsc

F32 = jnp.float32
BF16 = jnp.bfloat16

D_MODEL = 1024
N_META = 16
N_HEADS = 4
GLA_DK = 32
GLA_DV = 64
GLA_TAU = 16.0
MLA_NOPE = 64
MLA_ROPE = 32
MLA_V = 64
MLA_SLOT = 128
ROPE_THETA = 10000.0
GDN_DK = 64
GDN_CONV = 4
RWKV_N = 64
RWKV_LN_EPS = RWKV_N * 1e-5
CHUNK = 64
SUB = 16
PREP_GROUP = 5
GLA_STEP = (2, 320)
DELTA_STEP = (4, 128)
N_EXPERTS = 8
SC_CHUNK = 64
NORM_EPS = 1e-6
L2_EPS = 1e-6
NEG_INF = -1e30
EXP_CLAMP = 80.0

LANE = 128
VMEM_LIMIT = 56 * 1024 * 1024

_OFF = np.cumsum([0, 128, 128, 256, 16, 256, 256, 128, 32, 768, 256, 4, 4, 1056, 4096]).tolist()
W_GLA, W_MLA, W_GDN, W_RWKV = 896, 512, 1152, 1152


def _cparams(*sem):
    return pltpu.CompilerParams(dimension_semantics=sem, vmem_limit_bytes=VMEM_LIMIT)


def _pick(n, prefs):
    for p in prefs:
        if n % p == 0:
            return p
    raise ValueError(f"no tile for {n}")


def _dot(a, b):
    return jnp.dot(a.astype(BF16), b.astype(BF16), preferred_element_type=F32)


def _dot_nt(a, b):
    return lax.dot_general(a.astype(BF16), b.astype(BF16), (((1,), (1,)), ((), ())),
                           preferred_element_type=F32)


def _dot_tn(a, b):
    return lax.dot_general(a.astype(BF16), b.astype(BF16), (((0,), (0,)), ((), ())),
                           preferred_element_type=F32)


def _split3(x):
    hi = x.astype(BF16)
    r1 = x - hi.astype(F32)
    mid = r1.astype(BF16)
    lo = (r1 - mid.astype(F32)).astype(BF16)
    return hi, mid, lo


def _dot01_left(m01, x):
    return sum(jnp.dot(m01, p, preferred_element_type=F32) for p in _split3(x))


def _dot01_right(x, m01):
    return sum(jnp.dot(p, m01, preferred_element_type=F32) for p in _split3(x))


def _seg_sum(x, m01):
    hi = x.astype(BF16)
    lo = (x - hi.astype(F32)).astype(BF16)
    return jnp.dot(hi, m01, preferred_element_type=F32) + jnp.dot(lo, m01, preferred_element_type=F32)


def _iota(shape, dim):
    return lax.broadcasted_iota(jnp.int32, shape, dim)


def _div(x, w):
    return x >> int(math.log2(w))


def _ltri(n):
    return (_iota((n, n), 0) >= _iota((n, n), 1)).astype(BF16)


def _head_ones(n, w):
    return (_div(_iota((n, n), 0), w) == _div(_iota((n, n), 1), w)).astype(BF16)


def _lane_masks(width, w):
    lane = _div(_iota((1, width), 1), w)
    return [(lane == h).astype(F32) for h in range(width // w)]


def _stack(x, masks):
    return jnp.concatenate([x * m for m in masks], axis=0)


def _unstack(y, n):
    out = y[0:n]
    for h in range(1, y.shape[0] // n):
        out = out + y[h * n:(h + 1) * n]
    return out


def _rms(x, w):
    return x * lax.rsqrt(jnp.mean(x * x, axis=-1, keepdims=True) + NORM_EPS) * w


def _sigmoid(x):
    return 1.0 / (1.0 + jnp.exp(-x))


def _silu(x):
    return x * _sigmoid(x)


def _softplus(x):
    return jnp.maximum(x, 0.0) + jnp.log(1.0 + jnp.exp(-jnp.abs(x)))


def _neumann_inverse(xs, eye):
    n = xs[0].shape[0]
    ts = [eye + x for x in xs]
    ps = [_dot(x, x) for x in xs]
    for _ in range(int(math.log2(CHUNK)) - 2):
        yield
        both = [_dot(jnp.concatenate([t, p], axis=0), p) for t, p in zip(ts, ps)]
        ts = [t + tp[:n] for t, tp in zip(ts, both)]
        ps = [tp[n:] for tp in both]
    yield
    return [t + _dot(t, p) for t, p in zip(ts, ps)]


def _for_chunks(n, body, group):
    def trip(i, carry):
        body([i * group + g for g in range(group)])
        return carry

    if n >= group:
        lax.fori_loop(0, n // group, trip, 0)
    if n % group:
        body(list(range(n - n % group, n)))


def _inproj_kernel(h_ref, nw_ref, wg_ref, wm_ref, wd_ref, wr_ref, og_ref, om_ref, od_ref, or_ref):
    xb = _rms(h_ref[...], nw_ref[...]).astype(BF16)
    og_ref[...] = jnp.dot(xb, wg_ref[...], preferred_element_type=F32)
    om_ref[...] = jnp.dot(xb, wm_ref[...], preferred_element_type=F32)
    od_ref[...] = jnp.dot(xb, wd_ref[...], preferred_element_type=F32)
    or_ref[...] = jnp.dot(xb, wr_ref[...], preferred_element_type=F32)


def _inproj(h, nw, wg, wm, wd, wr):
    n = h.shape[0]
    tm = _pick(n, (512, 256, 128, 64))
    full = lambda a: pl.BlockSpec(a.shape, lambda i: (0, 0))
    row = lambda w: pl.BlockSpec((tm, w), lambda i: (i, 0))
    return pl.pallas_call(
        _inproj_kernel,
        grid=(n // tm,),
        in_specs=[row(D_MODEL), full(nw), full(wg), full(wm), full(wd), full(wr)],
        out_specs=[row(W_GLA), row(W_MLA), row(W_GDN), row(W_RWKV)],
        out_shape=[jax.ShapeDtypeStruct((n, w), F32) for w in (W_GLA, W_MLA, W_GDN, W_RWKV)],
        compiler_params=_cparams("parallel"),
        name="inproj",
    )(h, nw, wg, wm, wd, wr)


def _merge_kernel(h_ref, nw_ref, wgate_ref, yg_ref, ym_ref, yd_ref, yr_ref,
                  wbg_ref, wbm_ref, wbd_ref, wbr_ref, wout_ref, out_ref):
    x = h_ref[...]
    xb = _rms(x, nw_ref[...]).astype(BF16)
    acc = jnp.zeros(x.shape, F32)
    branches = ((yg_ref, wbg_ref), (ym_ref, wbm_ref), (yd_ref, wbd_ref), (yr_ref, wbr_ref))
    for i, (y_ref, wb_ref) in enumerate(branches):
        logits = jnp.dot(xb, wgate_ref[:, i * D_MODEL:(i + 1) * D_MODEL], preferred_element_type=F32)
        proj = jnp.dot(y_ref[...], wb_ref[...], preferred_element_type=F32)
        acc = acc + _sigmoid(logits) * proj
    out_ref[...] = x + jnp.dot(acc.astype(BF16), wout_ref[...], preferred_element_type=F32)


def _merge(h, nw, wgate, ys, wbs, wout):
    n = h.shape[0]
    tm = _pick(n, (1024, 512, 256, 128, 64))
    full = lambda a: pl.BlockSpec(a.shape, lambda i: (0, 0), pipeline_mode=pl.Buffered(1))
    row = lambda w: pl.BlockSpec((tm, w), lambda i: (i, 0))
    return pl.pallas_call(
        _merge_kernel,
        grid=(n // tm,),
        in_specs=[row(D_MODEL), full(nw), full(wgate)] + [row(y.shape[1]) for y in ys]
        + [full(w) for w in wbs] + [full(wout)],
        out_specs=row(D_MODEL),
        out_shape=jax.ShapeDtypeStruct((n, D_MODEL), F32),
        compiler_params=_cparams("parallel"),
        name="merge",
    )(h, nw, wgate, *ys, *wbs, wout)


def _ffn_kernel(h_ref, nw_ref, wg_ref, wu_ref, wd_ref, out_ref, xb_ref, acc_ref):
    f = pl.program_id(1)

    @pl.when(f == 0)
    def _():
        xb_ref[...] = _rms(h_ref[...], nw_ref[...]).astype(BF16)
        acc_ref[...] = jnp.zeros(acc_ref.shape, F32)

    xb = xb_ref[...]
    a = jnp.dot(xb, wg_ref[...], preferred_element_type=F32)
    b = jnp.dot(xb, wu_ref[...], preferred_element_type=F32)
    acc_ref[...] += jnp.dot((_silu(a) * b).astype(BF16), wd_ref[...], preferred_element_type=F32)

    @pl.when(f == pl.num_programs(1) - 1)
    def _():
        out_ref[...] = h_ref[...] + acc_ref[...]


def _ffn(h, nw, wg, wu, wd):
    n = h.shape[0]
    dff = wg.shape[1]
    tm = _pick(n, (512, 256, 128, 64))
    tf = _pick(dff, (1408, 512, 256, 128))
    return pl.pallas_call(
        _ffn_kernel,
        grid=(n // tm, dff // tf),
        in_specs=[pl.BlockSpec((tm, D_MODEL), lambda i, f: (i, 0)),
                  pl.BlockSpec(nw.shape, lambda i, f: (0, 0)),
                  pl.BlockSpec((D_MODEL, tf), lambda i, f: (0, f)),
                  pl.BlockSpec((D_MODEL, tf), lambda i, f: (0, f)),
                  pl.BlockSpec((tf, D_MODEL), lambda i, f: (f, 0))],
        out_specs=pl.BlockSpec((tm, D_MODEL), lambda i, f: (i, 0)),
        out_shape=jax.ShapeDtypeStruct((n, D_MODEL), F32),
        scratch_shapes=[pltpu.VMEM((tm, D_MODEL), BF16), pltpu.VMEM((tm, D_MODEL), F32)],
        compiler_params=_cparams("parallel", "arbitrary"),
        name="ffn",
    )(h, nw, wg, wu, wd)


def _final_kernel(nw_ref, *refs):
    out_ref = refs[-1]
    x = refs[0][...]
    for extra in refs[1:-1]:
        x = x + extra[...]
    out_ref[...] = _rms(x, nw_ref[...])


def _final(h, nw, b, tp, seq, y2=None):
    tq = _pick(seq, (1024, 512, 256, 128, 64))
    n = b * tp
    rows = lambda shift: pl.BlockSpec((pl.Element(tq), pl.Element(D_MODEL)),
                                      lambda i, j: (pl.multiple_of(shift + i * tp + N_META + j * tq, 8), 0))
    operands = [h] if y2 is None else [h, y2, y2]
    shifts = [0] if y2 is None else [0, 0, n]
    return pl.pallas_call(
        _final_kernel,
        grid=(b, seq // tq),
        in_specs=[pl.BlockSpec(nw.shape, lambda i, j: (0, 0))] + [rows(s) for s in shifts],
        out_specs=pl.BlockSpec((None, tq, D_MODEL), lambda i, j: (i, j, 0)),
        out_shape=jax.ShapeDtypeStruct((b, seq, D_MODEL), F32),
        compiler_params=_cparams("parallel", "parallel"),
        name="final_norm",
    )(nw, *operands)


_DONE = object()


def _interleave(generators):
    live = list(generators)
    while live:
        live = [g for g in live if next(g, _DONE) is not _DONE]


def _mixer_kernel(body, n_params, x_ref, *refs):
    params, y_ref, scratch = refs[:n_params], refs[n_params], refs[n_params + 1:]
    parts = [body(x_ref.at[b], *params, y_ref.at[b], *[s.at[b] for s in scratch]) for b in range(x_ref.shape[0])]
    n_chunks = x_ref.shape[1] // CHUNK
    if n_chunks <= PREP_GROUP:
        half = max(len(parts) // 2, 1)
        pending = []
        for group in [parts[i:i + half] for i in range(0, len(parts), half)]:
            _interleave([preprocess() for preprocess, _, _, _ in group] + pending)
            pending = [prepare(list(range(n_chunks))) for _, prepare, _, _ in group]
        _interleave(pending)
    else:
        for preprocess, _, _, _ in parts:
            _interleave([preprocess()])
        _for_chunks(n_chunks, lambda cis: [_interleave([prepare(cis)]) for _, prepare, _, _ in parts], PREP_GROUP)
    _for_chunks(n_chunks, lambda cis: [_interleave([advance(ci) for _, _, advance, _ in parts]) for ci in cis], 1)
    for _, _, _, finish in parts:
        finish()


def _mixer_block(tp, rows):
    return _pick(tp, (rows, 128, 64))


def _mixer_call(body, name, x, params, scratch, tb, nb):
    b, tp, width = x.shape
    nb = nb if b % nb == 0 else 1
    full = lambda a: pl.BlockSpec(a.shape, lambda i, j: (0, 0))
    return pl.pallas_call(
        functools.partial(_mixer_kernel, body, len(params)),
        grid=(b // nb, tp // tb),
        in_specs=[pl.BlockSpec((nb, tb, width), lambda i, j: (i, j, 0))] + [full(a) for a in params],
        out_specs=pl.BlockSpec((nb, tb, 256), lambda i, j: (i, j, 0)),
        out_shape=jax.ShapeDtypeStruct((b, tp, 256), BF16),
        scratch_shapes=[pltpu.VMEM((nb,) + shape, dtype) for shape, dtype in scratch],
        compiler_params=_cparams("parallel", "arbitrary"),
        name=name,
    )(x, *params)


def _gla_body(x_ref, gup_ref, gb_ref, nw_ref, y_ref, st_ref, la_ref, o_s, qg_s, kv_s, gt_s):
    @pl.when(pl.program_id(1) == 0)
    def _():
        st_ref[...] = jnp.zeros(st_ref.shape, F32)

    def preprocess():
        z = _dot(x_ref[:, 768:896], gup_ref[...]) + gb_ref[...]
        yield
        la_ref[...] = -_softplus(-z) * (1.0 / GLA_TAU)

    ltri = _ltri(CHUNK)
    qmasks = _lane_masks(N_HEADS * GLA_DK, GLA_DK)
    vmasks = _lane_masks(N_HEADS * GLA_DV, GLA_DV)
    bd = (_div(_iota((256, 128), 0), GLA_DV) == _div(_iota((256, 128), 1), GLA_DK)).astype(F32)
    hsum = _head_ones(N_HEADS * GLA_DV, GLA_DV)
    nw = nw_ref[...]

    n4 = N_HEADS * GLA_DV

    def prepare(cis):
        n = range(len(cis))
        rows = [pl.ds(pl.multiple_of(ci * CHUNK, CHUNK), CHUNK) for ci in cis]
        g = [_dot01_left(ltri, la_ref[r, :]) for r in rows]
        q = [x_ref[r, 0:128] * (GLA_DK ** -0.5) for r in rows]
        k = [x_ref[r, 128:256] for r in rows]
        v = [x_ref[r, 256:512] for r in rows]
        intra = [[] for _ in n]
        for s in range(CHUNK // SUB):
            lo, hi = s * SUB, (s + 1) * SUB
            sc = []
            for i in n:
                gs = jnp.zeros((1, 128), F32) if s == 0 else g[i][lo - 1:lo]
                qs = q[i][lo:hi] * jnp.exp(g[i][lo:hi] - gs)
                kt = k[i][:hi] * jnp.exp(jnp.minimum(gs - g[i][:hi], EXP_CLAMP))
                sc.append(_dot_nt(_stack(qs, qmasks), kt))
            causal = _iota(sc[0].shape, 1) <= lo + (_iota(sc[0].shape, 0) & (SUB - 1))
            yield
            p = [_dot(jnp.where(causal, sc[i], 0.0), v[i][:hi]) for i in n]
            for i in n:
                intra[i].append(sum(p[i][h * SUB:(h + 1) * SUB] * vmasks[h] for h in range(N_HEADS)))
            yield
        for i in n:
            g_last = g[i][CHUNK - 1:CHUNK]
            o_s[rows[i], :] = jnp.concatenate(intra[i], axis=0)
            qg_s[rows[i], :] = (q[i] * jnp.exp(g[i])).astype(BF16)
            kv_s[pl.ds(pl.multiple_of(cis[i] * n4, n4), n4), :] = bd * _dot_tn(v[i], k[i] * jnp.exp(g_last - g[i]))
            gt_s[pl.ds(pl.multiple_of(cis[i] * 8, 8), 8), :] = jnp.broadcast_to(jnp.exp(g_last), (8, 128))

    def advance(ci):
        rows = pl.ds(pl.multiple_of(ci * CHUNK, CHUNK), CHUNK)
        st = st_ref[...]
        inter = _dot_nt(qg_s[rows, :], st)
        yield
        o_s[rows, :] += inter
        st_ref[...] = (st * gt_s[pl.ds(pl.multiple_of(ci * 8, 8), 1), :]
                       + kv_s[pl.ds(pl.multiple_of(ci * n4, n4), n4), :])

    def finish():
        o = o_s[...]
        ms = _seg_sum(o * o, hsum) * (1.0 / GLA_DV)
        y_ref[...] = (o * lax.rsqrt(ms + NORM_EPS) * nw * _silu(x_ref[:, 512:768])).astype(y_ref.dtype)

    return preprocess, prepare, advance, finish


def _gla(x, gup, gb, nw):
    nb, rows = GLA_STEP
    tb = _mixer_block(x.shape[1], rows)
    nc = tb // CHUNK
    scratch = [((256, 128), F32), ((tb, 128), F32), ((tb, 256), F32), ((tb, 128), BF16),
               ((nc * N_HEADS * GLA_DV, 128), F32), ((nc * 8, 128), F32)]
    return _mixer_call(_gla_body, "gla", x, (gup, gb, nw), scratch, tb, nb)


def _gdn_body(x_ref, cw_ref, alog_ref, dtb_ref, nw_ref, y_ref,
              s_ref, xp_ref, q_ref, k_ref, v_ref, beta_ref, gd_ref,
              u_s, w_s, attn_s, qd_s, ke_s, gt_s, o_s):
    tb = x_ref.shape[0]
    first = pl.program_id(1) == 0

    @pl.when(first)
    def _():
        s_ref[...] = jnp.zeros(s_ref.shape, F32)
        xp_ref[0:8, :] = jnp.zeros((8, 768), F32)

    @pl.when(jnp.logical_not(first))
    def _():
        xp_ref[0:8, :] = xp_ref[tb:tb + 8, :]

    hsum = _head_ones(256, GDN_DK)

    def preprocess():
        xp_ref[8:tb + 8, :] = x_ref[:, 0:768]
        yield
        conv = cw_ref[0:1, :] * xp_ref[8 - (GDN_CONV - 1):8 - (GDN_CONV - 1) + tb, :]
        for j in range(1, GDN_CONV):
            yield
            conv = conv + cw_ref[j:j + 1, :] * xp_ref[8 - (GDN_CONV - 1) + j:8 - (GDN_CONV - 1) + j + tb, :]
        yield
        c = _silu(conv)
        q = c[:, 0:256]
        k = c[:, 256:512]
        yield
        q_ref[...] = q * lax.rsqrt(_seg_sum(q * q, hsum) + L2_EPS) * (GDN_DK ** -0.5)
        yield
        k_ref[...] = k * lax.rsqrt(_seg_sum(k * k, hsum) + L2_EPS)
        v_ref[...] = c[:, 512:768]
        yield
        gates = x_ref[:, 1024:1152]
        beta_ref[...] = _sigmoid(gates)
        gd_ref[...] = -jnp.exp(alog_ref[...]) * _softplus(gates + dtb_ref[...])

    ltri = _ltri(CHUNK)
    masks = _lane_masks(256, GDN_DK)
    expand = lambda off: (_iota((128, 256), 0) == _div(_iota((128, 256), 1), GDN_DK) + off).astype(BF16)
    exp_beta, exp_g = expand(0), expand(N_HEADS)
    r = _iota((256, 256), 0)
    cidx = _iota((256, 256), 1)
    same = _div(r, CHUNK) == _div(cidx, CHUNK)
    incl = jnp.logical_and(same, r >= cidx)
    strict = jnp.logical_and(same, r > cidx)
    eye = (r == cidx).astype(F32)
    bd = same.astype(F32)
    nw = nw_ref[...]

    def prepare(cis):
        n = range(len(cis))
        rows = [pl.ds(pl.multiple_of(ci * CHUNK, CHUNK), CHUNK) for ci in cis]
        srows = [pl.ds(pl.multiple_of(ci * (N_HEADS * CHUNK), N_HEADS * CHUNK), N_HEADS * CHUNK) for ci in cis]
        q = [q_ref[r, :] for r in rows]
        k = [k_ref[r, :] for r in rows]
        v = [v_ref[r, :] for r in rows]
        bexp = [_dot01_right(beta_ref[r, :], exp_beta) for r in rows]
        gcum = [_dot01_left(ltri, gd_ref[r, :]) for r in rows]
        yield
        gexp = [_dot01_right(g, exp_g) for g in gcum]
        kst = [_stack(x, masks) for x in k]
        kb = [k[i] * bexp[i] for i in n]
        n4 = N_HEADS * CHUNK
        yield
        both = [_dot_nt(jnp.concatenate([_stack(kb[i], masks), _stack(q[i], masks)], axis=0), kst[i]) for i in n]
        kk = [x[:n4] for x in both]
        qk = [x[n4:] for x in both]
        yield
        dec = []
        for g in gexp:
            gcol = jnp.sum(_stack(g, masks), axis=1, keepdims=True) * (1.0 / GDN_DK)
            grow = jnp.sum(eye * gcol, axis=0, keepdims=True)
            dec.append(jnp.exp(jnp.minimum(gcol - grow, 0.0)))
        for i in n:
            attn_s[srows[i], :] = jnp.where(incl, qk[i] * dec[i], 0.0).astype(BF16)
        yield
        t = yield from _neumann_inverse([-jnp.where(strict, kk[i] * dec[i], 0.0) for i in n], eye)
        yield
        u = [_dot(t[i], _stack(v[i] * bexp[i], masks)) for i in n]
        w = [_dot(t[i], _stack(kb[i] * jnp.exp(gexp[i]), masks)) for i in n]
        yield
        for i in n:
            u_s[rows[i], :] = _unstack(u[i], CHUNK)
            w_s[rows[i], :] = _unstack(w[i], CHUNK).astype(BF16)
            g_last = gexp[i][CHUNK - 1:CHUNK]
            qd_s[rows[i], :] = (q[i] * jnp.exp(gexp[i])).astype(BF16)
            ke_s[rows[i], :] = (k[i] * jnp.exp(g_last - gexp[i])).astype(BF16)
            gt_s[pl.ds(pl.multiple_of(cis[i] * 8, 8), 8), :] = jnp.broadcast_to(jnp.exp(g_last), (8, 256))

    def advance(ci):
        rows = pl.ds(pl.multiple_of(ci * CHUNK, CHUNK), CHUNK)
        srows = pl.ds(pl.multiple_of(ci * (N_HEADS * CHUNK), N_HEADS * CHUNK), N_HEADS * CHUNK)
        s = s_ref[...]
        sb = s.astype(BF16)
        ws = jnp.dot(w_s[rows, :], sb, preferred_element_type=F32)
        qs = jnp.dot(qd_s[rows, :], sb, preferred_element_type=F32)
        yield
        v_new = u_s[rows, :] - ws
        av = _dot(attn_s[srows, :], _stack(v_new, masks))
        kv = _dot_tn(ke_s[rows, :], v_new)
        yield
        o_s[rows, :] = qs + _unstack(av, CHUNK)
        s_ref[...] = s * gt_s[pl.ds(pl.multiple_of(ci * 8, 8), 1), :] + bd * kv

    def finish():
        o = o_s[...]
        ms = _seg_sum(o * o, hsum) * (1.0 / GDN_DK)
        y_ref[...] = (o * lax.rsqrt(ms + NORM_EPS) * nw * _silu(x_ref[:, 768:1024])).astype(y_ref.dtype)

    return preprocess, prepare, advance, finish


def _gdn(x, cw, alog, dtb, nw):
    nb, rows = DELTA_STEP
    tb = _mixer_block(x.shape[1], rows)
    nc = tb // CHUNK
    scratch = [((256, 256), F32), ((tb + 8, 768), F32),
               ((tb, 256), F32), ((tb, 256), F32), ((tb, 256), F32), ((tb, 128), F32), ((tb, 128), F32),
               ((tb, 256), F32), ((tb, 256), BF16), ((nc * N_HEADS * CHUNK, 256), BF16),
               ((tb, 256), BF16), ((tb, 256), BF16), ((nc * 8, 256), F32), ((tb, 256), F32)]
    return _mixer_call(_gdn_body, "gdn", x, (cw, alog, dtb, nw), scratch, tb, nb)


def _rwkv_body(x_ref, mu_ref, w2a2_ref, w0_ref, a0_ref, g2_ref, kk_ref, ka_ref, rk_ref,
               lnw_ref, lnb_ref, y_ref,
               s_ref, xp_ref, r_s, k_s, v_s, kk_s, b_s, lw_s, g_s,
               at_s, z_s, arb_s, yv_s, rt_s, be_s, vk_s, gt_s, y_s):
    tb = x_ref.shape[0]
    first = pl.program_id(1) == 0

    @pl.when(first)
    def _():
        s_ref[...] = jnp.zeros(s_ref.shape, F32)
        xp_ref[0:8, :] = jnp.zeros((8, W_RWKV), F32)

    @pl.when(jnp.logical_not(first))
    def _():
        xp_ref[0:8, :] = xp_ref[tb:tb + 8, :]

    hsum = _head_ones(256, RWKV_N)

    def preprocess():
        x = x_ref[...]
        xp_ref[8:tb + 8, :] = x
        yield
        z = x + (xp_ref[7:tb + 7, :] - x) * mu_ref[...]
        r = z[:, 0:256]
        k = z[:, 256:512]
        wa = z[:, 768:896]
        wa = jnp.where(_iota(wa.shape, 1) < 64, jnp.tanh(wa), wa)
        yield
        pre = _dot(wa, w2a2_ref[...])
        yield
        w_log = -_softplus(-(w0_ref[...] + pre[:, 0:256])) - 0.5
        lw_s[...] = -jnp.exp(w_log)
        yield
        a = _sigmoid(a0_ref[...] + pre[:, 256:512])
        kkv = k * kk_ref[...]
        yield
        kkn = kkv * lax.rsqrt(_seg_sum(kkv * kkv, hsum) + L2_EPS)
        yield
        r_s[...] = r
        k_s[...] = k * (1.0 + (a - 1.0) * ka_ref[...])
        v_s[...] = z[:, 512:768]
        kk_s[...] = kkn
        b_s[...] = kkn * a
        yield
        g_s[...] = _dot(_sigmoid(z[:, 896:1152]), g2_ref[...])

    ltri = _ltri(CHUNK)
    masks = _lane_masks(256, RWKV_N)
    rr = _iota((256, 256), 0)
    cc = _iota((256, 256), 1)
    same = _div(rr, CHUNK) == _div(cc, CHUNK)
    incl = jnp.logical_and(same, rr >= cc)
    strict = jnp.logical_and(same, rr > cc)
    eye = (rr == cc).astype(F32)
    bd = same.astype(F32)
    rk = rk_ref[...]
    lnw = lnw_ref[...]
    lnb = lnb_ref[...]

    n4 = N_HEADS * CHUNK

    def prepare(cis):
        n = range(len(cis))
        rows = [pl.ds(pl.multiple_of(ci * CHUNK, CHUNK), CHUNK) for ci in cis]
        srows = [pl.ds(pl.multiple_of(ci * n4, n4), n4) for ci in cis]
        lw = [lw_s[r, :] for r in rows]
        gl = [_dot01_left(ltri, x) for x in lw]
        v = [v_s[r, :] for r in rows]
        k = [k_s[r, :] for r in rows]
        b = [b_s[r, :] for r in rows]
        yield
        e_neg = [jnp.exp(-g) for g in gl]
        a_st = [_stack(-kk_s[rows[i], :] * jnp.exp(gl[i] - lw[i]), masks) for i in n]
        r_t = [r_s[rows[i], :] * jnp.exp(gl[i]) for i in n]
        yield
        amat = []
        for i in n:
            lhs = jnp.concatenate([a_st[i], _stack(r_t[i], masks)], axis=0)
            rhs = jnp.concatenate([_stack(b[i] * e_neg[i], masks), _stack(k[i] * e_neg[i], masks)], axis=0)
            amat.append(_dot_nt(lhs, rhs))
        yield
        vst = [_stack(x, masks) for x in v]
        both = [_dot(jnp.concatenate([jnp.where(strict, amat[i][0:n4, n4:], 0.0),
                                      jnp.where(incl, amat[i][n4:, n4:], 0.0)], axis=0), vst[i]) for i in n]
        av = [x[:n4] for x in both]
        yv = [x[n4:] for x in both]
        yield
        t = yield from _neumann_inverse([jnp.where(strict, m[0:n4, 0:n4], 0.0) for m in amat], eye)
        yield
        z = [_dot(t[i], av[i]) for i in n]
        at = [_dot(t[i], a_st[i]) for i in n]
        yield
        for i in n:
            g_last = gl[i][CHUNK - 1:CHUNK]
            e_end = jnp.exp(g_last - gl[i])
            z_s[srows[i], :] = z[i]
            at_s[srows[i], :] = at[i].astype(BF16)
            arb_s[srows[i], :] = jnp.where(incl, amat[i][n4:, 0:n4], 0.0).astype(BF16)
            yv_s[rows[i], :] = _unstack(yv[i], CHUNK)
            rt_s[rows[i], :] = r_t[i].astype(BF16)
            be_s[rows[i], :] = (b[i] * e_end).astype(BF16)
            vk_s[srows[i], :] = bd * _dot_tn(v[i], k[i] * e_end)
            gt_s[pl.ds(pl.multiple_of(cis[i] * 8, 8), 8), :] = jnp.broadcast_to(jnp.exp(g_last), (8, 256))

    def advance(ci):
        rows = pl.ds(pl.multiple_of(ci * CHUNK, CHUNK), CHUNK)
        srows = pl.ds(pl.multiple_of(ci * n4, n4), n4)
        s = s_ref[...]
        sb = s.astype(BF16)
        u_st = _dot_nt(at_s[srows, :], sb) + z_s[srows, :]
        rs = _dot_nt(rt_s[rows, :], sb)
        yield
        au = _dot(arb_s[srows, :], u_st)
        ub = _dot_tn(_unstack(u_st, CHUNK), be_s[rows, :])
        yield
        y_s[rows, :] = rs + _unstack(au, CHUNK) + yv_s[rows, :]
        s_ref[...] = s * gt_s[pl.ds(pl.multiple_of(ci * 8, 8), 1), :] + bd * ub + vk_s[srows, :]

    def finish():
        y = y_s[...]
        v = v_s[...]
        mean = _seg_sum(y, hsum) * (1.0 / RWKV_N)
        d = y - mean
        var = _seg_sum(d * d, hsum) * (1.0 / RWKV_N)
        yn = d * lax.rsqrt(var + RWKV_LN_EPS) * lnw + lnb
        bonus = _seg_sum(r_s[...] * k_s[...] * rk, hsum) * v
        y_ref[...] = ((yn + bonus) * g_s[...]).astype(y_ref.dtype)

    return preprocess, prepare, advance, finish


def _rwkv(x, mu, w2a2, w0, a0, g2, kk, ka, rk, lnw, lnb):
    nb, rows = DELTA_STEP
    tb = _mixer_block(x.shape[1], rows)
    nc = tb // CHUNK
    stacked = lambda dt: ((nc * N_HEADS * CHUNK, 256), dt)
    scratch = ([((256, 256), F32), ((tb + 8, W_RWKV), F32)] + [((tb, 256), F32)] * 7
               + [stacked(BF16), stacked(F32), stacked(BF16), ((tb, 256), F32), ((tb, 256), BF16),
                  ((tb, 256), BF16), stacked(F32), ((nc * 8, 256), F32), ((tb, 256), F32)])
    return _mixer_call(_rwkv_body, "rwkv", x, (mu, w2a2, w0, a0, g2, kk, ka, rk, lnw, lnb), scratch, tb, nb)


def _mla_prep_kernel(x_ref, qnw_ref, kvnw_ref, wqa_ref, wqb_ref, wk_ref, wv_ref, ea_ref, eb_ref,
                     c1_ref, s1_ref, q_ref, k_ref, v_ref):
    x = x_ref[...]
    qn = _rms(x[:, 0:256], qnw_ref[...]).astype(BF16)
    kvn = _rms(x[:, 256:384], kvnw_ref[...]).astype(BF16)
    kpe = x[:, 384:512].astype(BF16)
    c1 = c1_ref[...]
    s1 = s1_ref[...]
    qa = jnp.dot(qn, wqa_ref[...], preferred_element_type=F32)
    qb = jnp.dot(qn, wqb_ref[...], preferred_element_type=F32)
    kn = jnp.dot(kvn, wk_ref[...], preferred_element_type=F32)
    kp = (jnp.dot(kpe, ea_ref[...], preferred_element_type=F32) * c1
          + jnp.dot(kpe, eb_ref[...], preferred_element_type=F32) * s1)
    ones_lane = ((_iota((1, N_HEADS * MLA_SLOT), 1) & (MLA_SLOT - 1)) == MLA_V).astype(F32)
    v_ref[...] = (jnp.dot(kvn, wv_ref[...], preferred_element_type=F32) + ones_lane).astype(BF16)
    for h in range(N_HEADS):
        sl = slice(h * MLA_SLOT, (h + 1) * MLA_SLOT)
        q_ref[:, sl] = (qa[:, sl] * c1 + qb[:, sl] * s1).astype(BF16)
        k_ref[:, sl] = (kn[:, sl] + kp).astype(BF16)


def _mla_prep(x, qnw, kvnw, wqa, wqb, wk, wv, ea, eb, c1, s1):
    b, tp, _ = x.shape
    tm = _pick(tp, (640, 128, 64))
    full = lambda a: pl.BlockSpec(a.shape, lambda i, j: (0, 0))
    wide = N_HEADS * MLA_SLOT
    out = pl.BlockSpec((None, tm, wide), lambda i, j: (i, j, 0))
    tab = pl.BlockSpec((tm, MLA_SLOT), lambda i, j: (j, 0))
    return pl.pallas_call(
        _mla_prep_kernel,
        grid=(b, tp // tm),
        in_specs=[pl.BlockSpec((None, tm, W_MLA), lambda i, j: (i, j, 0))]
        + [full(a) for a in (qnw, kvnw, wqa, wqb, wk, wv, ea, eb)] + [tab, tab],
        out_specs=[out, out, out],
        out_shape=[jax.ShapeDtypeStruct((b, tp, wide), BF16)] * 3,
        compiler_params=_cparams("parallel", "parallel"),
        name="mla_prep",
    )(x, qnw, kvnw, wqa, wqb, wk, wv, ea, eb, c1, s1)


FLASH_HEADS = 4


def _flash_kernel(q_ref, k_ref, v_ref, o_ref, m_ref, acc_ref):
    qi = pl.program_id(2)
    t = q_ref.shape[0]
    m_ref[...] = jnp.full(m_ref.shape, NEG_INF, F32)
    acc_ref[...] = jnp.zeros(acc_ref.shape, F32)

    def block(start, width, diagonal):
        rows = pl.ds(pl.multiple_of(start, t), width)
        for h in range(FLASH_HEADS):
            sl = slice(h * MLA_SLOT, (h + 1) * MLA_SLOT)
            s = lax.dot_general(q_ref[:, sl], k_ref[rows, sl], (((1,), (1,)), ((), ())),
                                preferred_element_type=F32)
            if diagonal:
                s = jnp.where(_iota(s.shape, 0) >= _iota(s.shape, 1) - (width - t), s, NEG_INF)
            m_old = m_ref[h]
            m_new = jnp.maximum(m_old, jnp.max(s, axis=-1, keepdims=True))
            p = jnp.concatenate([jnp.exp2(s[:, c * LANE:(c + 1) * LANE] - m_new)
                                 for c in range(s.shape[1] // LANE)], axis=1).astype(BF16)
            acc_ref[h] = (jnp.exp2(m_old - m_new) * acc_ref[h]
                          + jnp.dot(p, v_ref[rows, sl], preferred_element_type=F32))
            m_ref[h] = m_new

    def double_block(j, carry):
        block(j * (2 * t), 2 * t, False)
        return carry

    lax.fori_loop(0, qi // 2, double_block, 0)

    @pl.when(qi % 2 == 1)
    def _():
        block((qi - 1) * t, 2 * t, True)

    @pl.when(qi % 2 == 0)
    def _():
        block(qi * t, t, True)

    for h in range(FLASH_HEADS):
        acc = acc_ref[h]
        o_ref[:, h * MLA_SLOT:(h + 1) * MLA_SLOT] = (acc / acc[:, MLA_V:MLA_V + 1]).astype(o_ref.dtype)


def _flash(q, k, v):
    b, tp, wide = q.shape
    t = _pick(tp, (640, 128, 64))
    w = FLASH_HEADS * MLA_SLOT
    qspec = pl.BlockSpec((None, t, w), lambda i, h, qi: (i, qi, h))
    kspec = pl.BlockSpec((None, tp, w), lambda i, h, qi: (i, 0, h))
    return pl.pallas_call(
        _flash_kernel,
        grid=(b, wide // w, tp // t),
        in_specs=[qspec, kspec, kspec],
        out_specs=qspec,
        out_shape=jax.ShapeDtypeStruct((b, tp, wide), BF16),
        scratch_shapes=[pltpu.VMEM((FLASH_HEADS, t, LANE), F32), pltpu.VMEM((FLASH_HEADS, t, MLA_SLOT), F32)],
        compiler_params=_cparams("parallel", "parallel", "arbitrary"),
        name="mla_flash",
    )(q, k, v)


def _router_kernel(h_ref, nw_ref, wr_ref, xn_ref, info_ref):
    xn = _rms(h_ref[...], nw_ref[...])
    bits = pltpu.bitcast(xn.astype(BF16).astype(F32), jnp.uint32)
    half = D_MODEL // 2
    xn_ref[...] = (bits[:, :half] >> 16) | bits[:, half:]
    logits = jnp.dot(xn, wr_ref[...], preferred_element_type=F32, precision=lax.Precision.HIGHEST)
    lane = _iota(logits.shape, 1).astype(F32)
    valid = lane < N_EXPERTS
    l0 = jnp.where(valid, logits, NEG_INF)
    m1 = jnp.max(l0, axis=-1, keepdims=True)
    i1 = jnp.min(jnp.where(l0 == m1, lane, float(LANE)), axis=-1, keepdims=True)
    l1 = jnp.where(lane == i1, NEG_INF, l0)
    m2 = jnp.max(l1, axis=-1, keepdims=True)
    i2 = jnp.min(jnp.where(l1 == m2, lane, float(LANE)), axis=-1, keepdims=True)
    e2 = jnp.exp(m2 - m1)
    g1 = 1.0 / (1.0 + e2)
    g2 = e2 / (1.0 + e2)
    info = jnp.where(lane == 0, i1, 0.0)
    info = jnp.where(lane == 1, i2, info)
    info = jnp.where(lane == 2, g1, info)
    info = jnp.where(lane == 3, g2, info)
    info_ref[...] = info


def _router(h, nw, wr):
    n = h.shape[0]
    tm = _pick(n, (512, 256, 128, 64))
    return pl.pallas_call(
        _router_kernel,
        grid=(n // tm,),
        in_specs=[pl.BlockSpec((tm, D_MODEL), lambda i: (i, 0)), pl.BlockSpec(nw.shape, lambda i: (0, 0)),
                  pl.BlockSpec(wr.shape, lambda i: (0, 0))],
        out_specs=[pl.BlockSpec((tm, D_MODEL // 2), lambda i: (i, 0)), pl.BlockSpec((tm, LANE), lambda i: (i, 0))],
        out_shape=[jax.ShapeDtypeStruct((n, D_MODEL // 2), jnp.uint32), jax.ShapeDtypeStruct((n, LANE), F32)],
        compiler_params=_cparams("parallel"),
        name="moe_router",
    )(h, nw, wr)


def _expert_kernel(be_ref, x_ref, gate_ref, wg_ref, wu_ref, wd_ref, out_ref, acc_ref):
    f = pl.program_id(1)

    @pl.when(f == 0)
    def _():
        acc_ref[...] = jnp.zeros(acc_ref.shape, F32)

    packed = x_ref[...]
    low = pltpu.bitcast(packed << 16, F32)
    high = pltpu.bitcast(packed & jnp.uint32(0xFFFF0000), F32)
    xb = jnp.concatenate([low, high], axis=1).astype(BF16)
    a = jnp.dot(xb, wg_ref[...], preferred_element_type=F32)
    b = jnp.dot(xb, wu_ref[...], preferred_element_type=F32)
    acc_ref[...] += jnp.dot((_silu(a) * b).astype(BF16), wd_ref[...], preferred_element_type=F32)

    @pl.when(f == pl.num_programs(1) - 1)
    def _():
        out_ref[...] = acc_ref[...] * gate_ref[...]


def _experts(block_expert, x_rows, row_gate, wg, wu, wd, tm):
    cap = x_rows.shape[0]
    dff = wg.shape[2]
    tf = _pick(dff, (1792, 512, 256, 128))
    grid_spec = pltpu.PrefetchScalarGridSpec(
        num_scalar_prefetch=1,
        grid=(cap // tm, dff // tf),
        in_specs=[pl.BlockSpec((tm, D_MODEL // 2), lambda i, f, be: (i, 0)),
                  pl.BlockSpec((tm, 1), lambda i, f, be: (i, 0)),
                  pl.BlockSpec((None, D_MODEL, tf), lambda i, f, be: (be[i], 0, f)),
                  pl.BlockSpec((None, D_MODEL, tf), lambda i, f, be: (be[i], 0, f)),
                  pl.BlockSpec((None, tf, D_MODEL), lambda i, f, be: (be[i], f, 0))],
        out_specs=pl.BlockSpec((tm, D_MODEL), lambda i, f, be: (i, 0)),
        scratch_shapes=[pltpu.VMEM((tm, D_MODEL), F32)],
    )
    return pl.pallas_call(
        _expert_kernel,
        grid_spec=grid_spec,
        out_shape=jax.ShapeDtypeStruct((cap, D_MODEL), F32),
        compiler_params=_cparams("parallel", "arbitrary"),
        name="moe_experts",
    )(block_expert, x_rows, row_gate, wg, wu, wd)


def _combine_kernel(h_ref, ya_ref, yb_ref, out_ref):
    out_ref[...] = h_ref[...] + (ya_ref[...] + yb_ref[...])


def _combine(h, y2):
    n = h.shape[0]
    tm = _pick(n, (1024, 512, 256, 128, 64))
    spec = pl.BlockSpec((tm, D_MODEL), lambda i: (i, 0))
    second = pl.BlockSpec((tm, D_MODEL), lambda i: (i + n // tm, 0))
    return pl.pallas_call(
        _combine_kernel, grid=(n // tm,), in_specs=[spec, spec, second], out_specs=spec,
        out_shape=jax.ShapeDtypeStruct((n, D_MODEL), F32),
        compiler_params=_cparams("parallel"), name="moe_combine",
    )(h, y2, y2)


def _sc_gather(table, idx):
    rows = idx.shape[0]
    d = table.shape[1]
    info = plsc.get_sparse_core_info()
    workers = info.num_cores * info.num_subcores
    assert rows % (workers * SC_CHUNK) == 0, (rows, workers)
    per_worker = rows // workers
    mesh = plsc.VectorSubcoreMesh(core_axis_name="c", subcore_axis_name="s")

    @functools.partial(
        pl.kernel, mesh=mesh, out_type=jax.ShapeDtypeStruct((rows, d), table.dtype),
        scratch_types=[pltpu.VMEM((SC_CHUNK,), jnp.int32), pltpu.VMEM((SC_CHUNK, d), table.dtype),
                       pltpu.SemaphoreType.DMA],
        name="sc_gather")
    def gather(table_hbm, idx_hbm, out_hbm, idx_v, rows_v, sem):
        base = (lax.axis_index("s") * info.num_cores + lax.axis_index("c")) * per_worker

        @pl.loop(0, per_worker // SC_CHUNK)
        def _(i):
            off = pl.multiple_of(base + i * SC_CHUNK, 8)
            pltpu.sync_copy(idx_hbm.at[pl.ds(off, SC_CHUNK)], idx_v)
            pltpu.async_copy(table_hbm.at[idx_v], rows_v, sem).wait()
            pltpu.sync_copy(rows_v, out_hbm.at[pl.ds(off, SC_CHUNK)])

    return gather(table, idx)


def _moe(h, nw, router, wg, wu, wd, defer_combine=False):
    n = h.shape[0]
    tm = _pick(n, (512, 64))
    wr = jnp.pad(router.astype(F32), ((0, 0), (0, LANE - N_EXPERTS)))
    xn, info = _router(h, nw, wr)
    expert = info[:, 0:2].astype(jnp.int32).reshape(-1)
    gate = info[:, 2:4].reshape(-1)
    n_assign = 2 * n
    order = jnp.argsort(expert)
    onehot = (expert[:, None] == jnp.arange(N_EXPERTS, dtype=jnp.int32)[None, :]).astype(jnp.int32)
    running = jnp.cumsum(onehot, axis=0)
    counts = running[-1]
    padded = (counts + tm - 1) // tm * tm
    pad_end = jnp.cumsum(padded)
    pad_start = pad_end - padded
    start = jnp.cumsum(counts) - counts
    n_blocks = -(-n_assign // tm) + N_EXPERTS
    cap = n_blocks * tm
    block_start = jnp.arange(n_blocks, dtype=jnp.int32) * tm
    block_expert = jnp.minimum(jnp.sum(block_start[:, None] >= pad_end[None, :], axis=1), N_EXPERTS - 1)
    block_expert = block_expert.astype(jnp.int32)
    rank = (block_start - pad_start[block_expert])[:, None] + jnp.arange(tm, dtype=jnp.int32)[None, :]
    valid = (rank < counts[block_expert][:, None]).reshape(cap)
    src = order[jnp.clip(start[block_expert][:, None] + rank, 0, n_assign - 1).reshape(cap)]
    row_token = jnp.where(valid, src // 2, 0)
    row_gate = jnp.where(valid, gate[src], 0.0)
    dest = jnp.sum(onehot * (pad_start[None, :] + running - 1), axis=1)
    x_rows = _sc_gather(xn, row_token)
    y_rows = _experts(block_expert, x_rows, row_gate[:, None], wg, wu, wd, tm)
    y2 = _sc_gather(y_rows, dest.reshape(n, 2).T.reshape(n_assign))
    return (h, y2) if defer_combine else _combine(h, y2)


def _pad_cols(a, width):
    return jnp.pad(a, ((0, 0), (0, width - a.shape[1])))


def _row(a, width=None):
    a = a.reshape(1, -1).astype(F32)
    return a if width is None else _pad_cols(a, width)


def _rope_tables(tp):
    pos = jnp.arange(tp, dtype=F32)
    inv_freq = ROPE_THETA ** (-jnp.arange(0, MLA_ROPE, 2, dtype=F32) / MLA_ROPE)
    ang = pos[:, None] * inv_freq[None, :]
    cos, sin = jnp.cos(ang), jnp.sin(ang)
    ones = jnp.ones((tp, MLA_NOPE), F32)
    zeros = jnp.zeros((tp, MLA_SLOT - MLA_NOPE - MLA_ROPE), F32)
    c1 = jnp.concatenate([ones, cos, cos, zeros], axis=1)
    s1 = jnp.concatenate([0.0 * ones, -sin, sin, zeros], axis=1)
    return c1, s1


def _mla_weights(w_uq, w_ukv):
    half = MLA_ROPE // 2
    scale = (MLA_NOPE + MLA_ROPE) ** -0.5 * math.log2(math.e)
    zq =jnp.zeros((w_uq.shape[0], MLA_SLOT - MLA_NOPE - MLA_ROPE), F32)
    zn = jnp.zeros((w_uq.shape[0], MLA_NOPE), F32)
    zk = jnp.zeros((w_ukv.shape[0], MLA_SLOT - MLA_NOPE), F32)
    wqa, wqb, wk, wv = [], [], [], []
    for h in range(N_HEADS):
        q = w_uq[:, h * 96:(h + 1) * 96] * scale
        nope, x1, x2 = q[:, :MLA_NOPE], q[:, MLA_NOPE:MLA_NOPE + half], q[:, MLA_NOPE + half:]
        wqa += [nope, x1, x2, zq]
        wqb += [zn, x2, x1, zq]
        kv = w_ukv[:, h * 128:(h + 1) * 128]
        wk += [kv[:, :MLA_NOPE], zk]
        wv += [kv[:, MLA_NOPE:], zk]
    cat = lambda parts: jnp.concatenate(parts, axis=1).astype(BF16)
    ea = np.zeros((MLA_SLOT, MLA_SLOT), np.float32)
    eb = np.zeros((MLA_SLOT, MLA_SLOT), np.float32)
    for i in range(MLA_ROPE):
        ea[i, MLA_NOPE + i] = 1.0
        eb[(i + half) % MLA_ROPE, MLA_NOPE + i] = 1.0
    return cat(wqa), cat(wqb), cat(wk), cat(wv), jnp.asarray(ea, BF16), jnp.asarray(eb, BF16)


def _branch_weights(w_branch):
    wb = w_branch.astype(BF16)
    z = jnp.zeros((MLA_SLOT - MLA_V, D_MODEL), BF16)
    parts = []
    for h in range(N_HEADS):
        parts += [wb[1, h * MLA_V:(h + 1) * MLA_V], z]
    return wb[0], jnp.concatenate(parts, axis=0), wb[2], wb[3]


def _token_mixing(h, b, tp, p, l, tables):
    w_in = p["w_in"][l]
    col = lambda i, j: w_in[:, _OFF[i]:_OFF[j]]
    wg = jnp.concatenate([col(0, 3), col(4, 5), _pad_cols(col(3, 4), LANE)], axis=1).astype(BF16)
    wm = jnp.concatenate([col(5, 7), _pad_cols(col(7, 8), LANE)], axis=1).astype(BF16)
    wd = jnp.concatenate([col(8, 10), _pad_cols(col(10, 12), LANE)], axis=1).astype(BF16)
    wr = _pad_cols(col(12, 13), W_RWKV).astype(BF16)
    nw = _row(p["norm_mix"][l])
    xg, xm, xd, xr = _inproj(h, nw, wg, wm, wd, wr)
    shape3 = lambda a: a.reshape(b, tp, a.shape[1])

    gup = jnp.pad(p["gla_gate_up"][l], ((0, LANE - 16), (0, 0))).astype(BF16)
    y_gla = _gla(shape3(xg), gup, _row(p["gla_gate_bias"][l]), _row(jnp.tile(p["gla_norm"][l], N_HEADS)))

    wqa, wqb, wk, wv, ea, eb = _mla_weights(p["mla_w_uq"][l], p["mla_w_ukv"][l])
    q, k, v = _mla_prep(shape3(xm), _row(p["mla_q_norm"][l]), _row(p["mla_kv_norm"][l]),
                        wqa, wqb, wk, wv, ea, eb, *tables)
    y_mla = _flash(q, k, v)

    lanes4 = lambda a: jnp.pad(a.reshape(1, N_HEADS).astype(F32), ((0, 0), (N_HEADS, LANE - 2 * N_HEADS)))
    y_gdn = _gdn(shape3(xd), p["gdn_conv"][l].astype(F32), lanes4(p["gdn_a_log"][l]),
                 lanes4(p["gdn_dt_bias"][l]), _row(jnp.tile(p["gdn_norm"][l], N_HEADS)))

    w2a2 = jnp.zeros((LANE, 512), F32)
    w2a2 = w2a2.at[0:64, 0:256].set(p["rwkv_w2"][l]).at[64:128, 256:512].set(p["rwkv_a2"][l]).astype(BF16)
    g2 = jnp.pad(p["rwkv_g2"][l], ((0, 256 - 160), (0, 0))).astype(BF16)
    y_rwkv = _rwkv(shape3(xr), _row(p["rwkv_mu"][l], W_RWKV), w2a2, _row(p["rwkv_w0"][l]),
                   _row(p["rwkv_a0"][l]), g2, _row(p["rwkv_k_k"][l]), _row(p["rwkv_k_a"][l]),
                   _row(p["rwkv_r_k"][l]), _row(p["rwkv_ln_w"][l]), _row(p["rwkv_ln_b"][l]))

    flat = lambda a: a.reshape(b * tp, a.shape[2])
    ys = [flat(y_gla), flat(y_mla), flat(y_gdn), flat(y_rwkv)]
    return _merge(h, nw, col(13, 14).astype(BF16), ys, _branch_weights(p["w_branch"][l]),
                  p["w_out"][l].astype(BF16))


def kernel(x, meta_tokens, norm_mix, w_in, gla_gate_up, gla_gate_bias, gla_norm, mla_q_norm, mla_w_uq, mla_kv_norm, mla_w_ukv, gdn_conv, gdn_a_log, gdn_dt_bias, gdn_norm, rwkv_mu, rwkv_w0, rwkv_w2, rwkv_a0, rwkv_a2, rwkv_g2, rwkv_k_k, rwkv_k_a, rwkv_r_k, rwkv_ln_w, rwkv_ln_b, w_branch, w_out, norm_ffn, ffn_w_gate, ffn_w_up, ffn_w_down, moe_router, moe_w_gate, moe_w_up, moe_w_down, norm_final):
    p = dict(norm_mix=norm_mix, w_in=w_in, gla_gate_up=gla_gate_up, gla_gate_bias=gla_gate_bias,
             gla_norm=gla_norm, mla_q_norm=mla_q_norm, mla_w_uq=mla_w_uq, mla_kv_norm=mla_kv_norm,
             mla_w_ukv=mla_w_ukv, gdn_conv=gdn_conv, gdn_a_log=gdn_a_log, gdn_dt_bias=gdn_dt_bias,
             gdn_norm=gdn_norm, rwkv_mu=rwkv_mu, rwkv_w0=rwkv_w0, rwkv_w2=rwkv_w2, rwkv_a0=rwkv_a0,
             rwkv_a2=rwkv_a2, rwkv_g2=rwkv_g2, rwkv_k_k=rwkv_k_k, rwkv_k_a=rwkv_k_a, rwkv_r_k=rwkv_r_k,
             rwkv_ln_w=rwkv_ln_w, rwkv_ln_b=rwkv_ln_b, w_branch=w_branch, w_out=w_out)
    b, seq, d = x.shape
    t_real = N_META + seq
    tp = -(-t_real // 128) * 128
    meta = jnp.broadcast_to(meta_tokens[None].astype(x.dtype), (b, N_META, d))
    h = jnp.concatenate([meta, x, jnp.zeros((b, tp - t_real, d), x.dtype)], axis=1).reshape(b * tp, d)
    tables = _rope_tables(tp)
    depth = norm_mix.shape[0]
    y2 = None
    for l in range(depth):
        h = _token_mixing(h, b, tp, p, l, tables)
        nw = _row(norm_ffn[l])
        if l % 2 == 0:
            h = _ffn(h, nw, ffn_w_gate[l // 2].astype(BF16), ffn_w_up[l // 2].astype(BF16),
                     ffn_w_down[l // 2].astype(BF16))
        elif l == depth - 1:
            h, y2 = _moe(h, nw, moe_router[l // 2], moe_w_gate[l // 2].astype(BF16),
                         moe_w_up[l // 2].astype(BF16), moe_w_down[l // 2].astype(BF16), defer_combine=True)
        else:
            h = _moe(h, nw, moe_router[l // 2], moe_w_gate[l // 2].astype(BF16),
                     moe_w_up[l // 2].astype(BF16), moe_w_down[l // 2].astype(BF16))
    return _final(h, _row(norm_final), b, tp, seq, y2)
```
